```python
import math
import jax, jax.numpy as jnp
from jax import lax
import numpy as np

D_MODEL = 2048
BATCH = 4
SEQ = 2048
DEPTH = 4
DEC_BATCH = 8
DEC_SEQ = 1
PAST_LEN = 16384
PAGE_SIZE = 128

FOX_HEADS = 8
FOX_HD = 128
FOX_W = FOX_HEADS * FOX_HD
Q_BLOCK = 128
CACHE_FORGET_BIAS = 9.0
S5_W = 512
S5_GROUP = 16
S5_GROUPS = S5_W // S5_GROUP
S5_STATE = 64
GLA_HEADS = 4
GLA_DK = 64
GLA_DV = 128
GLA_W = GLA_HEADS * GLA_DV
GLA_RANK = 16
GLA_TAU = 16.0
GLA_CHUNK = 16
MIX_W = FOX_W + S5_W + GLA_W
IN_COLS = 3 * FOX_W + FOX_HEADS + S5_W + 2 * GLA_HEADS * GLA_DK + GLA_W + GLA_RANK + GLA_W
MEM_TOKENS = 256
MEM_HEADS = 4
MEM_HD = 128
MEM_W = MEM_HEADS * MEM_HD
N_GROUPS = 4
EXP_PER_GROUP = 4
N_EXPERTS = N_GROUPS * EXP_PER_GROUP
D_EXPERT = 256
TOP_K = 2
EPS = 1e-6

kernel_name = 'hybrid_fox_s5_gla_hmoe_step'


def rmsnorm(x, g):
    xf = x.astype(jnp.float32)
    y = xf * lax.rsqrt(jnp.mean(xf * xf, axis=-1, keepdims=True) + EPS)
    return (y * g.astype(jnp.float32)).astype(x.dtype)


def project_mix(h, w_in, q_gain, k_gain, f_bias):
    B, T, _ = h.shape
    sizes = (FOX_W, FOX_W, FOX_W, FOX_HEADS, S5_W, GLA_HEADS * GLA_DK, GLA_HEADS * GLA_DK, GLA_W, GLA_RANK, GLA_W)
    cuts = np.cumsum(sizes)[:-1].tolist()
    fq, fk, fv, ff, su, gq, gk, gv, ga, go = jnp.split(h @ w_in, cuts, axis=-1)
    q = rmsnorm(fq.reshape(B, T, FOX_HEADS, FOX_HD), q_gain)
    k = rmsnorm(fk.reshape(B, T, FOX_HEADS, FOX_HD), k_gain)
    v = fv.reshape(B, T, FOX_HEADS, FOX_HD)
    logf = jax.nn.log_sigmoid((ff + f_bias).astype(jnp.float32))
    return q, k, v, logf, su, (gq, gk, gv, ga, go)


def suffix_logf(logf):
    return lax.cumsum(logf, axis=1, reverse=True) - logf


def fox_attend(q, k, v, r_q, r_k, q_pos, k_pos):
    s = jnp.einsum('bqhd,bkhd->bhqk', q.astype(jnp.float32), k.astype(jnp.float32)) * (FOX_HD ** -0.5)
    s = s + jnp.swapaxes(r_k, 1, 2)[:, :, None, :] - jnp.swapaxes(r_q, 1, 2)[:, :, :, None]
    s = jnp.where((k_pos[None, :] <= q_pos[:, None])[None, None], s, -jnp.inf)
    p = jax.nn.softmax(s, axis=-1)
    return jnp.einsum('bhqk,bkhd->bqhd', p, v.astype(jnp.float32)).astype(v.dtype)


def fox_prompt(q, k, v, logf):
    B, T = q.shape[:2]
    nb = T // Q_BLOCK
    r = suffix_logf(logf)
    pos = jnp.arange(T, dtype=jnp.int32)
    qb = q.reshape(B, nb, Q_BLOCK, FOX_HEADS, FOX_HD).transpose(1, 0, 2, 3, 4)
    rb = r.reshape(B, nb, Q_BLOCK, FOX_HEADS).transpose(1, 0, 2, 3)
    pb = pos.reshape(nb, Q_BLOCK)
    ob = lax.map(lambda a: fox_attend(a[0], k, v, a[1], r, a[2], pos), (qb, rb, pb))
    return ob.transpose(1, 0, 2, 3, 4).reshape(B, T, FOX_HEADS, FOX_HD)


def fox_sample(q, k_new, v_new, logf_new, k_past, v_past, logf_past):
    past, T = k_past.shape[1], q.shape[1]
    k = jnp.concatenate([k_past, k_new], axis=1)
    v = jnp.concatenate([v_past, v_new], axis=1)
    logf = jnp.concatenate([logf_past.astype(jnp.float32), logf_new], axis=1)
    r = suffix_logf(logf)
    k_pos = jnp.arange(past + T, dtype=jnp.int32)
    return fox_attend(q, k, v, r[:, past:], r, k_pos[past:], k_pos)


def gather_pages(pool, page_table):
    g = pool[page_table]
    return g.reshape((g.shape[0], g.shape[1] * g.shape[2]) + g.shape[3:])


def complex_affine_combine(e1, e2):
    a1r, a1i, b1r, b1i = e1
    a2r, a2i, b2r, b2i = e2
    return (a1r * a2r - a1i * a2i, a1r * a2i + a1i * a2r,
            a2r * b1r - a2i * b1i + b2r, a2r * b1i + a2i * b1r + b2i)


def s5_mix(u, h0_re, h0_im, a_re, a_im, log_dt, b_re, b_im, c_re, c_im, d_skip, w_glu, b_glu):
    B, T, _ = u.shape
    f32 = jnp.float32
    uf = u.astype(f32)
    a_re, a_im = a_re.astype(f32), a_im.astype(f32)
    dt = jnp.exp(log_dt.astype(f32))[:, None]
    mag = jnp.exp(a_re * dt)
    lam_re, lam_im = mag * jnp.cos(a_im * dt), mag * jnp.sin(a_im * dt)
    den = a_re * a_re + a_im * a_im
    z_re = ((lam_re - 1.0) * a_re + lam_im * a_im) / den
    z_im = (lam_im * a_re - (lam_re - 1.0) * a_im) / den
    b_re, b_im = b_re.astype(f32), b_im.astype(f32)
    bb_re = z_re[..., None] * b_re - z_im[..., None] * b_im
    bb_im = z_re[..., None] * b_im + z_im[..., None] * b_re
    ug = uf.reshape(B, T, S5_GROUPS, S5_GROUP)
    bu_re = jnp.einsum('gpc,btgc->btgp', bb_re, ug)
    bu_im = jnp.einsum('gpc,btgc->btgp', bb_im, ug)
    h0_re, h0_im = h0_re.astype(f32), h0_im.astype(f32)
    bu_re = bu_re.at[:, 0].add(lam_re * h0_re - lam_im * h0_im)
    bu_im = bu_im.at[:, 0].add(lam_re * h0_im + lam_im * h0_re)
    lam_re_t = jnp.broadcast_to(lam_re, bu_re.shape)
    lam_im_t = jnp.broadcast_to(lam_im, bu_im.shape)
    _, _, h_re, h_im = lax.associative_scan(complex_affine_combine, (lam_re_t, lam_im_t, bu_re, bu_im), axis=1)
    y = jnp.einsum('gcp,btgp->btgc', c_re.astype(f32), h_re) - jnp.einsum('gcp,btgp->btgc', c_im.astype(f32), h_im)
    y = y.reshape(B, T, S5_W) + d_skip.astype(f32) * uf
    y = jax.nn.gelu(y)
    y = y * jax.nn.sigmoid(y @ w_glu.astype(f32) + b_glu.astype(f32))
    return y.astype(u.dtype), h_re[:, -1], h_im[:, -1]


def gla_chunked(q, k, v, log_a, s0):
    B, T, H, DK = q.shape
    DV = v.shape[-1]
    pad = (-T) % GLA_CHUNK
    widths = ((0, 0), (0, pad), (0, 0), (0, 0))
    q, k, v, log_a = [jnp.pad(t.astype(jnp.float32), widths) for t in (q, k, v, log_a)]
    n = (T + pad) // GLA_CHUNK

    def chunks(t):
        return t.reshape(B, n, GLA_CHUNK, H, t.shape[-1]).transpose(1, 0, 3, 2, 4)

    qc, kc, vc, ac = chunks(q), chunks(k), chunks(v), chunks(log_a)
    bc = jnp.cumsum(ac, axis=3)
    tri = jnp.tril(jnp.ones((GLA_CHUNK, GLA_CHUNK), dtype=bool))

    def step(S, inp):
        qn, kn, vn, bn = inp
        o_inter = jnp.einsum('bhtd,bhdv->bhtv', qn * jnp.exp(bn), S)
        diff = bn[:, :, :, None, :] - bn[:, :, None, :, :]
        decay = jnp.exp(jnp.where(tri[:, :, None], diff, -jnp.inf))
        att = jnp.sum(qn[:, :, :, None, :] * kn[:, :, None, :, :] * decay, axis=-1)
        o = o_inter + jnp.einsum('bhts,bhsv->bhtv', att, vn)
        b_last = bn[:, :, -1]
        k_dec = kn * jnp.exp(b_last[:, :, None, :] - bn)
        S = jnp.exp(b_last)[..., None] * S + jnp.einsum('bhsd,bhsv->bhdv', k_dec, vn)
        return S, o

    S, o = lax.scan(step, s0.astype(jnp.float32), (qc, kc, vc, bc))
    o = o.transpose(1, 0, 3, 2, 4).reshape(B, n * GLA_CHUNK, H, DV)[:, :T]
    return o, S


def gla_mix(gq, gk, gv, ga, go, s0, w_a2, b_a, gain):
    B, T, _ = gq.shape
    q = gq.reshape(B, T, GLA_HEADS, GLA_DK) * (GLA_DK ** -0.5)
    k = gk.reshape(B, T, GLA_HEADS, GLA_DK)
    v = gv.reshape(B, T, GLA_HEADS, GLA_DV)
    log_a = (jax.nn.log_sigmoid((ga @ w_a2 + b_a).astype(jnp.float32)) / GLA_TAU).reshape(B, T, GLA_HEADS, GLA_DK)
    o, S = gla_chunked(q, k, v, log_a, s0)
    o = rmsnorm(o, gain) * jax.nn.silu(go.reshape(B, T, GLA_HEADS, GLA_DV).astype(jnp.float32))
    return o.reshape(B, T, GLA_W).astype(gq.dtype), S


def mix_merge(fox_o, s5_o, gla_o, g_fox_out, g_s5_out, w_out):
    B, T = fox_o.shape[:2]
    cat = jnp.concatenate([rmsnorm(fox_o.reshape(B, T, FOX_W), g_fox_out), rmsnorm(s5_o, g_s5_out), gla_o], axis=-1)
    return cat @ w_out


def memory_kv(mem, g_tok, w_k, w_v, k_gain):
    B, M, _ = mem.shape
    m = rmsnorm(mem, g_tok)
    k = rmsnorm((m @ w_k).reshape(B, M, MEM_HEADS, MEM_HD), k_gain)
    v = (m @ w_v).reshape(B, M, MEM_HEADS, MEM_HD)
    return k, v


def memory_attend(h, mk, mv, w_q, q_gain, w_o):
    B, T, _ = h.shape
    q = rmsnorm((h @ w_q).reshape(B, T, MEM_HEADS, MEM_HD), q_gain)
    s = jnp.einsum('bthd,bmhd->bhtm', q.astype(jnp.float32), mk.astype(jnp.float32)) * (MEM_HD ** -0.5)
    p = jax.nn.softmax(s, axis=-1)
    o = jnp.einsum('bhtm,bmhd->bthd', p, mv.astype(jnp.float32)).astype(h.dtype).reshape(B, T, MEM_W)
    return o @ w_o


def hier_moe(h, w_group, b_group, w_expert, b_expert, w_gate, w_up, w_down):
    B, T, D = h.shape
    f32 = jnp.float32
    x = h.reshape(B * T, D)
    p_group = jax.nn.softmax((x @ w_group + b_group).astype(f32), axis=-1)
    g_idx = jnp.argmax(p_group, axis=-1)
    g_w = jnp.max(p_group, axis=-1)
    le = (x @ w_expert + b_expert).astype(f32).reshape(-1, N_GROUPS, EXP_PER_GROUP)
    le = jnp.take_along_axis(le, g_idx[:, None, None], axis=1)[:, 0]
    top_w, top_i = lax.top_k(jax.nn.softmax(le, axis=-1), TOP_K)
    top_w = top_w / jnp.sum(top_w, axis=-1, keepdims=True)
    w_sel = jnp.sum(top_w[..., None] * jax.nn.one_hot(top_i, EXP_PER_GROUP, dtype=f32), axis=1)
    gate = (jax.nn.one_hot(g_idx, N_GROUPS, dtype=f32)[:, :, None] * w_sel[:, None, :]).reshape(-1, N_EXPERTS) * g_w[:, None]
    a = jax.nn.silu(jnp.einsum('nd,edf->nef', x, w_gate)) * jnp.einsum('nd,edf->nef', x, w_up)
    y = jnp.einsum('nef,efd->nd', a * gate[..., None].astype(a.dtype), w_down)
    return y.reshape(B, T, D).astype(h.dtype)


def setup_inputs(seed: int = 0) -> dict:
    key = jax.random.key(seed)
    ks = iter(jax.random.split(key, 64))
    f32 = jnp.float32

    def nrm(shape, scale=1.0):
        return jax.random.normal(next(ks), shape, f32) * scale

    def gain(shape):
        return 1.0 + nrm(shape, 0.05)

    n_pages = PAST_LEN // PAGE_SIZE
    n_used = DEC_BATCH * n_pages
    n_phys = n_used + max(1, n_used // 4)
    page_table = jax.random.permutation(next(ks), n_phys)[:n_used].reshape(DEC_BATCH, n_pages).astype(jnp.int32)
    n_idx = jnp.arange(S5_STATE, dtype=f32)
    L = DEPTH
    return {
        'x_prompt': nrm((BATCH, SEQ, D_MODEL)),
        'x_sample': nrm((DEC_BATCH, DEC_SEQ, D_MODEL)),
        'cache_fox_k': nrm((L, n_phys, PAGE_SIZE, FOX_HEADS, FOX_HD)),
        'cache_fox_v': nrm((L, n_phys, PAGE_SIZE, FOX_HEADS, FOX_HD)),
        'cache_fox_logf': jax.nn.log_sigmoid(CACHE_FORGET_BIAS + nrm((L, n_phys, PAGE_SIZE, FOX_HEADS), 0.5)),
        'state_s5_re': nrm((L, DEC_BATCH, S5_GROUPS, S5_STATE), 0.3),
        'state_s5_im': nrm((L, DEC_BATCH, S5_GROUPS, S5_STATE), 0.3),
        'state_gla': nrm((L, DEC_BATCH, GLA_HEADS, GLA_DK, GLA_DV)),
        'cache_mem_k': nrm((L, DEC_BATCH, MEM_TOKENS, MEM_HEADS, MEM_HD)),
        'cache_mem_v': nrm((L, DEC_BATCH, MEM_TOKENS, MEM_HEADS, MEM_HD)),
        'page_table': page_table,
        'mem_prompt': nrm((BATCH, MEM_TOKENS, D_MODEL)),
        'g_mix': gain((L, D_MODEL)),
        'w_in': nrm((L, D_MODEL, IN_COLS), D_MODEL ** -0.5),
        'fox_q_gain': gain((L, FOX_HD)),
        'fox_k_gain': gain((L, FOX_HD)),
        'fox_f_bias': 3.0 + nrm((L, FOX_HEADS), 0.5),
        's5_a_re': -0.5 + nrm((L, S5_GROUPS, S5_STATE), 0.01),
        's5_a_im': math.pi * n_idx + nrm((L, S5_GROUPS, S5_STATE), 0.01),
        's5_log_dt': math.log(1e-3) + (math.log(1e-1) - math.log(1e-3)) * jax.random.uniform(next(ks), (L, S5_GROUPS), f32),
        's5_b_re': nrm((L, S5_GROUPS, S5_STATE, S5_GROUP), 0.5),
        's5_b_im': nrm((L, S5_GROUPS, S5_STATE, S5_GROUP), 0.5),
        's5_c_re': nrm((L, S5_GROUPS, S5_GROUP, S5_STATE), 0.25),
        's5_c_im': nrm((L, S5_GROUPS, S5_GROUP, S5_STATE), 0.25),
        's5_d': nrm((L, S5_W)),
        's5_w_glu': nrm((L, S5_W, S5_W), S5_W ** -0.5),
        's5_b_glu': nrm((L, S5_W), 0.02),
        'gla_w_a2': nrm((L, GLA_RANK, GLA_HEADS * GLA_DK), GLA_RANK ** -0.5),
        'gla_b_a': nrm((L, GLA_HEADS * GLA_DK), 0.1),
        'gla_gain': gain((L, GLA_DV)),
        'g_fox_out': gain((L, FOX_W)),
        'g_s5_out': gain((L, S5_W)),
        'w_out': nrm((L, MIX_W, D_MODEL), MIX_W ** -0.5),
        'g_mem_pre': gain((L, D_MODEL)),
        'g_mem_tok': gain((L, D_MODEL)),
        'mem_w_q': nrm((L, D_MODEL, MEM_W), D_MODEL ** -0.5),
        'mem_w_k': nrm((L, D_MODEL, MEM_W), D_MODEL ** -0.5),
        'mem_w_v': nrm((L, D_MODEL, MEM_W), D_MODEL ** -0.5),
        'mem_q_gain': gain((L, MEM_HD)),
        'mem_k_gain': gain((L, MEM_HD)),
        'mem_w_o': nrm((L, MEM_W, D_MODEL), MEM_W ** -0.5),
        'g_moe': gain((L, D_MODEL)),
        'moe_w_group': nrm((L, D_MODEL, N_GROUPS), D_MODEL ** -0.5),
        'moe_b_group': nrm((L, N_GROUPS), 0.01),
        'moe_w_expert': nrm((L, D_MODEL, N_EXPERTS), D_MODEL ** -0.5),
        'moe_b_expert': nrm((L, N_EXPERTS), 0.01),
        'moe_w_gate': nrm((L, N_EXPERTS, D_MODEL, D_EXPERT), D_MODEL ** -0.5),
        'moe_w_up': nrm((L, N_EXPERTS, D_MODEL, D_EXPERT), D_MODEL ** -0.5),
        'moe_w_down': nrm((L, N_EXPERTS, D_EXPERT, D_MODEL), D_EXPERT ** -0.5),
    }


def reference(x_prompt, x_sample, cache_fox_k, cache_fox_v, cache_fox_logf, state_s5_re, state_s5_im, state_gla,
              cache_mem_k, cache_mem_v, page_table, mem_prompt, g_mix, w_in, fox_q_gain, fox_k_gain, fox_f_bias,
              s5_a_re, s5_a_im, s5_log_dt, s5_b_re, s5_b_im, s5_c_re, s5_c_im, s5_d, s5_w_glu, s5_b_glu,
              gla_w_a2, gla_b_a, gla_gain, g_fox_out, g_s5_out, w_out, g_mem_pre, g_mem_tok, mem_w_q, mem_w_k,
              mem_w_v, mem_q_gain, mem_k_gain, mem_w_o, g_moe, moe_w_group, moe_b_group, moe_w_expert,
              moe_b_expert, moe_w_gate, moe_w_up, moe_w_down):
    xp, xs = x_prompt, x_sample
    B, DB = xp.shape[0], xs.shape[0]
    pk, pv, pf, sk, sv, sf = [], [], [], [], [], []
    p5r, p5i, s5r, s5i, pg, sg, pmk, pmv = [], [], [], [], [], [], [], []
    for l in range(DEPTH):
        s5p = (s5_a_re[l], s5_a_im[l], s5_log_dt[l], s5_b_re[l], s5_b_im[l], s5_c_re[l], s5_c_im[l],
               s5_d[l], s5_w_glu[l], s5_b_glu[l])
        glap = (gla_w_a2[l], gla_b_a[l], gla_gain[l])
        moep = (moe_w_group[l], moe_b_group[l], moe_w_expert[l], moe_b_expert[l], moe_w_gate[l], moe_w_up[l], moe_w_down[l])

        h = rmsnorm(xp, g_mix[l])
        q, k, v, logf, su, gin = project_mix(h, w_in[l], fox_q_gain[l], fox_k_gain[l], fox_f_bias[l])
        fo = fox_prompt(q, k, v, logf)
        so, hre, him = s5_mix(su, jnp.zeros((B, S5_GROUPS, S5_STATE), jnp.float32),
                              jnp.zeros((B, S5_GROUPS, S5_STATE), jnp.float32), *s5p)
        go, S = gla_mix(*gin, jnp.zeros((B, GLA_HEADS, GLA_DK, GLA_DV), jnp.float32), *glap)
        xp = xp + mix_merge(fo, so, go, g_fox_out[l], g_s5_out[l], w_out[l])
        pk.append(k); pv.append(v); pf.append(logf); p5r.append(hre); p5i.append(him); pg.append(S)
        mk, mv = memory_kv(mem_prompt, g_mem_tok[l], mem_w_k[l], mem_w_v[l], mem_k_gain[l])
        pmk.append(mk); pmv.append(mv)
        xp = xp + memory_attend(rmsnorm(xp, g_mem_pre[l]), mk, mv, mem_w_q[l], mem_q_gain[l], mem_w_o[l])
        xp = xp + hier_moe(rmsnorm(xp, g_moe[l]), *moep)

        h = rmsnorm(xs, g_mix[l])
        q, k, v, logf, su, gin = project_mix(h, w_in[l], fox_q_gain[l], fox_k_gain[l], fox_f_bias[l])
        k_past = gather_pages(cache_fox_k[l], page_table)
        v_past = gather_pages(cache_fox_v[l], page_table)
        f_past = gather_pages(cache_fox_logf[l], page_table)
        fo = fox_sample(q, k, v, logf, k_past, v_past, f_past)
        so, hre, him = s5_mix(su, state_s5_re[l], state_s5_im[l], *s5p)
        go, S = gla_mix(*gin, state_gla[l], *glap)
        xs = xs + mix_merge(fo, so, go, g_fox_out[l], g_s5_out[l], w_out[l])
        sk.append(k); sv.append(v); sf.append(logf); s5r.append(hre); s5i.append(him); sg.append(S)
        xs = xs + memory_attend(rmsnorm(xs, g_mem_pre[l]), cache_mem_k[l], cache_mem_v[l], mem_w_q[l], mem_q_gain[l], mem_w_o[l])
        xs = xs + hier_moe(rmsnorm(xs, g_moe[l]), *moep)

    return (xp, xs,
            jnp.stack(pk), jnp.stack(pv), jnp.stack(pf),
            jnp.stack(sk), jnp.stack(sv), jnp.stack(sf),
            jnp.stack(p5r), jnp.stack(p5i), jnp.stack(s5r), jnp.stack(s5i),
            jnp.stack(pg), jnp.stack(sg),
            jnp.stack(pmk), jnp.stack(pmv))
```

```python
import functools
import math

import numpy as np
import jax
import jax.numpy as jnp
from jax import lax
from jax.experimental import pallas as pl
from jax.experimental.pallas import tpu as pltpu

F32, BF16 = jnp.float32, jnp.bfloat16

D_MODEL = 2048
FOX_HEADS, FOX_HD = 8, 128
FOX_W = FOX_HEADS * FOX_HD
S5_W, S5_GROUP, S5_GROUPS, S5_STATE = 512, 16, 32, 64
S5_NS = S5_GROUPS * S5_STATE
S5_BLOCKS = 4
S5_BC = S5_W // S5_BLOCKS
S5_BS = S5_NS // S5_BLOCKS
GLA_HEADS, GLA_DK, GLA_DV = 4, 64, 128
GLA_W = GLA_HEADS * GLA_DV
GLA_QK = GLA_HEADS * GLA_DK
GLA_RANK = 16
GLA_TAU = 16.0
GLA_CHUNK = 16
MEM_TOKENS, MEM_HEADS, MEM_HD = 256, 4, 128
MEM_W = MEM_HEADS * MEM_HD
N_GROUPS, EXP_PER_GROUP, N_EXPERTS, D_EXPERT = 4, 4, 16, 256
PAGE_SIZE = 128
EPS = 1e-6
NEG = -1e30

LANES = 128
VMEM_LIMIT_BYTES = 48 * 1024 * 1024

DECODE_GROUP = 16
DECODE_PAGES_PER_STEP = 8
TAIL_W = LANES
SAMPLE_ROWS = 16


def _params(*sem):
    return pltpu.CompilerParams(dimension_semantics=sem, vmem_limit_bytes=VMEM_LIMIT_BYTES)


def _sds(shape, dtype):
    return jax.ShapeDtypeStruct(shape, dtype)


_NN = (((1,), (0,)), ((), ()))
_NT = (((1,), (1,)), ((), ()))
_TN = (((0,), (0,)), ((), ()))


def _mxu(a, b, precise, dims=_NN):
    if precise:
        return lax.dot_general(a.astype(F32), b.astype(F32), dims, preferred_element_type=F32,
                               precision=lax.Precision.HIGHEST)
    return lax.dot_general(a.astype(BF16), b.astype(BF16), dims, preferred_element_type=F32)


def _rms_kernel(x_ref, g_ref, o_ref):
    x = x_ref[...]
    y = x * lax.rsqrt(jnp.mean(x * x, axis=-1, keepdims=True) + EPS)
    o_ref[...] = (y * g_ref[...]).astype(o_ref.dtype)


def rmsnorm_cast(x, g, tm, out_dtype=BF16):
    n, d = x.shape
    return pl.pallas_call(
        _rms_kernel, grid=(n // tm,),
        in_specs=[pl.BlockSpec((tm, d), lambda i: (i, 0)), pl.BlockSpec((1, d), lambda i: (0, 0))],
        out_specs=pl.BlockSpec((tm, d), lambda i: (i, 0)),
        out_shape=_sds((n, d), out_dtype), compiler_params=_params("parallel"), name="rmsnorm_cast",
    )(x, g.reshape(1, d))


def _mm_kernel(*refs, headnorm, post_scale, logsig_lanes, has_res, n_out):
    x_ref, w_ref = refs[0], refs[1]
    pos = 2
    gain_ref = bias_ref = res_ref = None
    if headnorm:
        gain_ref, pos = refs[pos], pos + 1
    if logsig_lanes:
        bias_ref, pos = refs[pos], pos + 1
    if has_res:
        res_ref, pos = refs[pos], pos + 1
    outs = refs[pos:pos + n_out]
    acc = jnp.dot(x_ref[...], w_ref[...], preferred_element_type=F32)
    if logsig_lanes:
        lane = lax.broadcasted_iota(jnp.int32, acc.shape, 1)
        acc = jnp.where(lane < logsig_lanes, jax.nn.log_sigmoid(acc + bias_ref[...]), acc)
    if has_res:
        acc = res_ref[...] + acc
    if headnorm:
        for c in range(acc.shape[1] // LANES):
            sl = slice(c * LANES, (c + 1) * LANES)
            blk = acc[:, sl]
            y = blk * lax.rsqrt(jnp.mean(blk * blk, axis=-1, keepdims=True) + EPS)
            y = y * gain_ref[:, sl]
            if post_scale != 1.0:
                y = y * post_scale
            for o in outs:
                o[:, sl] = y.astype(o.dtype)
    else:
        for o in outs:
            o[...] = acc.astype(o.dtype)


def matmul(x, w, *, tm, tn, out_dtypes=(F32,), gain=None, post_scale=1.0, logsig_bias=None,
           logsig_lanes=0, res=None, name="matmul"):
    n, k = x.shape
    c = w.shape[1]
    args = [x, w]
    in_specs = [pl.BlockSpec((tm, k), lambda i, j: (i, 0)), pl.BlockSpec((k, tn), lambda i, j: (0, j))]
    if gain is not None:
        args.append(gain.reshape(1, c).astype(F32))
        in_specs.append(pl.BlockSpec((1, tn), lambda i, j: (0, j)))
    if logsig_lanes:
        args.append(logsig_bias.reshape(1, c).astype(F32))
        in_specs.append(pl.BlockSpec((1, tn), lambda i, j: (0, j)))
    if res is not None:
        args.append(res)
        in_specs.append(pl.BlockSpec((tm, tn), lambda i, j: (i, j)))
    outs = pl.pallas_call(
        functools.partial(_mm_kernel, headnorm=gain is not None, post_scale=post_scale,
                          logsig_lanes=logsig_lanes, has_res=res is not None, n_out=len(out_dtypes)),
        grid=(n // tm, c // tn), in_specs=in_specs,
        out_specs=[pl.BlockSpec((tm, tn), lambda i, j: (i, j)) for _ in out_dtypes],
        out_shape=[_sds((n, c), dt) for dt in out_dtypes],
        compiler_params=_params("parallel", "parallel"), name=name,
    )(*args)
    return outs


def _head_rmsnorm(blk, gain):
    return blk * lax.rsqrt(jnp.mean(blk * blk, axis=-1, keepdims=True) + EPS) * gain


def _pmm_kernel(*refs, epi, n_aux, post_scale):
    x_ref, w_ref = refs[0], refs[1]
    aux = refs[2:2 + n_aux]
    o_ref, acc_s = refs[2 + n_aux], refs[3 + n_aux]
    k = pl.program_id(0)

    @pl.when(k == 0)
    def _():
        acc_s[...] = jnp.zeros(acc_s.shape, F32)

    acc_s[...] += _mxu(x_ref[...], w_ref[...], True)

    @pl.when(k == pl.num_programs(0) - 1)
    def _():
        acc = acc_s[...]
        if epi == "none":
            o_ref[...] = acc
        elif epi == "res":
            o_ref[...] = aux[0][...] + acc
        elif epi == "headnorm":
            for c in range(acc.shape[1] // LANES):
                sl = slice(c * LANES, (c + 1) * LANES)
                o_ref[:, sl] = _head_rmsnorm(acc[:, sl], aux[0][:, sl]) * post_scale
        elif epi == "inproj":
            gain_ref, bias_ref = aux
            for c in range(2 * FOX_W // LANES):
                sl = slice(c * LANES, (c + 1) * LANES)
                o_ref[:, sl] = _head_rmsnorm(acc[:, sl], gain_ref[:, sl])
            o_ref[:, 2 * FOX_W:3 * FOX_W] = acc[:, 2 * FOX_W:3 * FOX_W]
            t = acc[:, 3 * FOX_W:]
            lane = lax.broadcasted_iota(jnp.int32, t.shape, 1)
            o_ref[:, 3 * FOX_W:] = jnp.where(lane < FOX_HEADS, jax.nn.log_sigmoid(t + bias_ref[...]), t)
        elif epi == "router":
            o_ref[...] = _router_gate(acc + aux[0][...])


def pmatmul(x, w, layer, *, tk, epi="none", aux=(), post_scale=1.0, name="pmatmul"):
    rows, kdim = x.shape
    c = w.shape[2]
    in_specs = [pl.BlockSpec((rows, tk), lambda k: (0, k)), pl.BlockSpec((None, tk, c), lambda k: (layer, k, 0))]
    in_specs += [pl.BlockSpec(a.shape, lambda k: (0, 0)) for a in aux]
    return pl.pallas_call(
        functools.partial(_pmm_kernel, epi=epi, n_aux=len(aux), post_scale=post_scale), grid=(kdim // tk,),
        in_specs=in_specs, out_specs=pl.BlockSpec((rows, c), lambda k: (0, 0)),
        out_shape=_sds((rows, c), F32), scratch_shapes=[pltpu.VMEM((rows, c), F32)],
        compiler_params=_params("arbitrary"), name=name,
    )(x, w, *aux)


def _fox_gate_kernel(t_ref, crow_ref, ccol_ref):
    x = t_ref[0]
    c = x.T[0:FOX_HEADS, :]
    t_len = c.shape[1]
    lane = lax.broadcasted_iota(jnp.int32, c.shape, 1)
    s = 1
    while s < t_len:
        c = c + jnp.where(lane >= s, pltpu.roll(c, s, axis=1), 0.0)
        s *= 2
    crow_ref[0] = c
    cpad = jnp.concatenate([c, jnp.zeros((LANES - FOX_HEADS, t_len), F32)], axis=0)
    ccol_ref[0] = cpad.T


def fox_gate_cumsum(tail, b, t):
    return pl.pallas_call(
        _fox_gate_kernel, grid=(b,),
        in_specs=[pl.BlockSpec((1, t, TAIL_W), lambda i: (i, 0, 0))],
        out_specs=[pl.BlockSpec((1, FOX_HEADS, t), lambda i: (i, 0, 0)),
                   pl.BlockSpec((1, t, TAIL_W), lambda i: (i, 0, 0))],
        out_shape=[_sds((b, FOX_HEADS, t), F32), _sds((b, t, TAIL_W), F32)],
        compiler_params=_params("parallel"), name="fox_gate_cumsum",
    )(tail.reshape(b, t, TAIL_W))


def _fox_attn_kernel(qi_tab, ki_tab, q_ref, k_ref, v_ref, cq_ref, ck_ref, g_ref, o_ref, m_s, l_s, acc_s, *, tq, tk):
    p_id = pl.program_id(1)
    qi, ki = qi_tab[p_id], ki_tab[p_id]

    @pl.when(ki == 0)
    def _():
        m_s[...] = jnp.full(m_s.shape, NEG, F32)
        l_s[...] = jnp.zeros(l_s.shape, F32)
        acc_s[...] = jnp.zeros(acc_s.shape, F32)

    def step(masked):
        if masked:
            row = lax.broadcasted_iota(jnp.int32, (tq, tk), 0)
            col = lax.broadcasted_iota(jnp.int32, (tq, tk), 1)
            causal = col <= row
        for h in range(FOX_HEADS):
            sl = slice(h * FOX_HD, (h + 1) * FOX_HD)
            s = lax.dot_general(q_ref[0, :, sl], k_ref[0, :, sl], (((1,), (1,)), ((), ())),
                                preferred_element_type=F32)
            s = s + (cq_ref[0, :, h:h + 1] - ck_ref[0, h:h + 1, :])
            if masked:
                s = jnp.where(causal, s, NEG)
            m_old = m_s[h]
            m_new = jnp.maximum(m_old, jnp.max(s, axis=-1, keepdims=True))
            alpha = jnp.exp(m_old - m_new)
            p = jnp.exp(s - m_new)
            l_s[h] = l_s[h] * alpha + jnp.sum(p, axis=-1, keepdims=True)
            m_s[h] = m_new
            acc_s[:, sl] = acc_s[:, sl] * alpha + jnp.dot(p.astype(BF16), v_ref[0, :, sl],
                                                          preferred_element_type=F32)

    @pl.when(ki < qi)
    def _():
        step(False)

    @pl.when(ki == qi)
    def _():
        step(True)
        ssq = jnp.zeros((tq, 1), F32)
        for h in range(FOX_HEADS):
            sl = slice(h * FOX_HD, (h + 1) * FOX_HD)
            o = acc_s[:, sl] / l_s[h]
            acc_s[:, sl] = o
            ssq = ssq + jnp.sum(o * o, axis=-1, keepdims=True)
        inv = lax.rsqrt(ssq / FOX_W + EPS)
        o_ref[0] = (acc_s[...] * inv * g_ref[...]).astype(o_ref.dtype)


def fox_prompt_attention(q, k, v, ccol, crow, g_out, b, t, tq):
    nq = t // tq
    pairs = [(i, j) for i in range(nq) for j in range(i + 1)]
    qi_tab = jnp.asarray(np.array([p[0] for p in pairs], np.int32))
    ki_tab = jnp.asarray(np.array([p[1] for p in pairs], np.int32))
    q3, k3, v3 = (a.reshape(b, t, FOX_W) for a in (q, k, v))
    grid_spec = pltpu.PrefetchScalarGridSpec(
        num_scalar_prefetch=2, grid=(b, len(pairs)),
        in_specs=[
            pl.BlockSpec((1, tq, FOX_W), lambda bi, p, qt, kt: (bi, qt[p], 0)),
            pl.BlockSpec((1, tq, FOX_W), lambda bi, p, qt, kt: (bi, kt[p], 0)),
            pl.BlockSpec((1, tq, FOX_W), lambda bi, p, qt, kt: (bi, kt[p], 0)),
            pl.BlockSpec((1, tq, TAIL_W), lambda bi, p, qt, kt: (bi, qt[p], 0)),
            pl.BlockSpec((1, FOX_HEADS, tq), lambda bi, p, qt, kt: (bi, 0, kt[p])),
            pl.BlockSpec((1, FOX_W), lambda bi, p, qt, kt: (0, 0)),
        ],
        out_specs=pl.BlockSpec((1, tq, FOX_W), lambda bi, p, qt, kt: (bi, qt[p], 0)),
        scratch_shapes=[pltpu.VMEM((FOX_HEADS, tq, 1), F32), pltpu.VMEM((FOX_HEADS, tq, 1), F32),
                        pltpu.VMEM((tq, FOX_W), F32)],
    )
    out = pl.pallas_call(
        functools.partial(_fox_attn_kernel, tq=tq, tk=tq), grid_spec=grid_spec,
        out_shape=_sds((b, t, FOX_W), BF16),
        compiler_params=_params("parallel", "arbitrary"), name="fox_prompt_attention",
    )(qi_tab, ki_tab, q3, k3, v3, ccol, crow, g_out.reshape(1, FOX_W))
    return out.reshape(b * t, FOX_W)


def _fox_decode_kernel(pt_ref, q_ref, kn_ref, vn_ref, fn_ref, g_ref, *refs, pages_per_step, scale):
    pp = pages_per_step
    k_refs, v_refs, f_refs = refs[0:pp], refs[pp:2 * pp], refs[2 * pp:3 * pp]
    o_ref = refs[3 * pp]
    m_s, l_s, acc_s, r_s = refs[3 * pp + 1:]
    step_id = pl.program_id(1)
    q = q_ref[0] * scale

    @pl.when(step_id == 0)
    def _():
        m_s[...] = jnp.sum(q * kn_ref[0], axis=-1, keepdims=True)
        l_s[...] = jnp.ones(l_s.shape, F32)
        acc_s[...] = vn_ref[0]
        r_s[...] = fn_ref[0]

    tok_r = lax.broadcasted_iota(jnp.int32, (PAGE_SIZE, PAGE_SIZE), 0)
    tok_c = lax.broadcasted_iota(jnp.int32, (PAGE_SIZE, PAGE_SIZE), 1)
    later = (tok_r > tok_c).astype(F32)
    r_run = r_s[...]
    parts = []
    grp = DECODE_GROUP
    tok = lax.broadcasted_iota(jnp.int32, (grp, 1, FOX_HD), 0)
    lane = lax.broadcasted_iota(jnp.int32, (grp, 1, FOX_HD), 2)
    for j in range(pp):
        f = f_refs[j][...]
        r = jnp.dot(f, later, preferred_element_type=F32, precision=lax.Precision.HIGHEST) + r_run
        r_run = r_run + jnp.sum(f, axis=-1, keepdims=True)
        for g in range(PAGE_SIZE // grp):
            rows = pl.ds(g * grp, grp)
            prod = k_refs[j][rows] * q[None] + jnp.where(lane == tok + g * grp, r[None], 0.0)
            s = jnp.sum(prod, axis=-1, keepdims=True)
            m_g = jnp.max(s, axis=0)
            p = jnp.exp(s - m_g[None])
            parts.append((m_g, jnp.sum(p, axis=0), jnp.sum(p * v_refs[j][rows], axis=0)))
    m_old = m_s[...]
    m_new = m_old
    for m_j, _, _ in parts:
        m_new = jnp.maximum(m_new, m_j)
    alpha = jnp.exp(m_old - m_new)
    l_run, acc = l_s[...] * alpha, acc_s[...] * alpha
    for m_j, l_j, pv_j in parts:
        a_j = jnp.exp(m_j - m_new)
        l_run, acc = l_run + l_j * a_j, acc + pv_j * a_j
    m_s[...], l_s[...], acc_s[...], r_s[...] = m_new, l_run, acc, r_run

    @pl.when(step_id == pl.num_programs(1) - 1)
    def _():
        o = acc_s[...] / l_s[...]
        ssq = jnp.sum(jnp.sum(o * o, axis=-1, keepdims=True), axis=0, keepdims=True)
        o_ref[0] = (o * lax.rsqrt(ssq / FOX_W + EPS) * g_ref[...]).astype(o_ref.dtype)


def fox_sample_attention(layer, q, k_new, v_new, logf_new, cache_k, cache_v, cache_ft, page_table, g_out,
                         pages_per_step):
    db, n_pages = page_table.shape
    n_steps = n_pages // pages_per_step

    def page_map(j):
        def im(bi, si, pt):
            return (layer, pt[bi, n_pages - 1 - (si * pages_per_step + j)], 0, 0, 0)
        return im

    def page_map_f(j):
        def im(bi, si, pt):
            return (layer, pt[bi, n_pages - 1 - (si * pages_per_step + j)], 0, 0)
        return im

    vec = pl.BlockSpec((1, FOX_HEADS, FOX_HD), lambda bi, si, pt: (bi, 0, 0))
    in_specs = [vec, vec, vec,
                pl.BlockSpec((1, FOX_HEADS, 1), lambda bi, si, pt: (bi, 0, 0)),
                pl.BlockSpec((FOX_HEADS, FOX_HD), lambda bi, si, pt: (0, 0))]
    in_specs += [pl.BlockSpec((None, None, PAGE_SIZE, FOX_HEADS, FOX_HD), page_map(j)) for j in range(pages_per_step)]
    in_specs += [pl.BlockSpec((None, None, PAGE_SIZE, FOX_HEADS, FOX_HD), page_map(j)) for j in range(pages_per_step)]
    in_specs += [pl.BlockSpec((None, None, FOX_HEADS, PAGE_SIZE), page_map_f(j)) for j in range(pages_per_step)]
    grid_spec = pltpu.PrefetchScalarGridSpec(
        num_scalar_prefetch=1, grid=(db, n_steps), in_specs=in_specs,
        out_specs=pl.BlockSpec((1, FOX_HEADS, FOX_HD), lambda bi, si, pt: (bi, 0, 0)),
        scratch_shapes=[pltpu.VMEM((FOX_HEADS, 1), F32), pltpu.VMEM((FOX_HEADS, 1), F32),
                        pltpu.VMEM((FOX_HEADS, FOX_HD), F32), pltpu.VMEM((FOX_HEADS, 1), F32)],
    )
    return pl.pallas_call(
        functools.partial(_fox_decode_kernel, pages_per_step=pages_per_step, scale=FOX_HD ** -0.5),
        grid_spec=grid_spec, out_shape=_sds((db, FOX_HEADS, FOX_HD), F32),
        compiler_params=_params("parallel", "arbitrary"), name="fox_sample_attention",
    )(page_table, q, k_new, v_new, logf_new, g_out.reshape(FOX_HEADS, FOX_HD),
      *([cache_k] * pages_per_step), *([cache_v] * pages_per_step), *([cache_ft] * pages_per_step))


def _s5_param_kernel(are_ref, aim_ref, ldt_ref, bre_ref, bim_ref, lre_ref, lim_ref, bbre_ref, bbim_ref):
    a_re, a_im = are_ref[...], aim_ref[...]
    dt = jnp.exp(ldt_ref[...])
    mag = jnp.exp(a_re * dt)
    lam_re, lam_im = mag * jnp.cos(a_im * dt), mag * jnp.sin(a_im * dt)
    den = a_re * a_re + a_im * a_im
    z_re = ((lam_re - 1.0) * a_re + lam_im * a_im) / den
    z_im = (lam_im * a_re - (lam_re - 1.0) * a_im) / den
    lre_ref[...] = lam_re
    lim_ref[...] = lam_im
    b_re, b_im = bre_ref[...], bim_ref[...]
    bbre_ref[...] = z_re[:, None, :] * b_re - z_im[:, None, :] * b_im
    bbim_ref[...] = z_re[:, None, :] * b_im + z_im[:, None, :] * b_re


def s5_discretise(a_re, a_im, log_dt, b_re_t, b_im_t):
    g, p = a_re.shape
    c = b_re_t.shape[1]
    return pl.pallas_call(
        _s5_param_kernel,
        out_shape=[_sds((g, p), F32), _sds((g, p), F32), _sds((g, c, p), F32), _sds((g, c, p), F32)],
        name="s5_discretise",
    )(a_re, a_im, log_dt.reshape(g, 1), b_re_t, b_im_t)


def _block_diag(w, groups_per_block):
    g, a, b = w.shape
    nb = g // groups_per_block
    eye = jnp.eye(groups_per_block, dtype=w.dtype)
    w5 = w.reshape(nb, groups_per_block, a, b)
    return jnp.einsum("jgab,gh->jgahb", w5, eye).reshape(nb, groups_per_block * a, groups_per_block * b)


def _s5_kernel(u_ref, h0re_ref, h0im_ref, lre_ref, lim_ref, wb_ref, wc_ref, d_ref, wglu_ref, bglu_ref, gout_ref,
               y_ref, hre_out, him_out, hre_s, him_s, cre_s, cim_s, y_s, *, rows, scan, precise):
    t_id = pl.program_id(1)
    lam_re, lam_im = lre_ref[...], lim_ref[...]

    @pl.when(t_id == 0)
    def _():
        cre_s[...] = h0re_ref[0]
        cim_s[...] = h0im_ref[0]

    u = u_ref[0]
    for j in range(S5_BLOCKS):
        bu = _mxu(u[:, j * S5_BC:(j + 1) * S5_BC], wb_ref[j], precise)
        hre_s[:, j * S5_BS:(j + 1) * S5_BS] = bu[:, :S5_BS]
        him_s[:, j * S5_BS:(j + 1) * S5_BS] = bu[:, S5_BS:]
    c_re, c_im = cre_s[...], cim_s[...]
    if scan:
        hre_s[0:1, :] = hre_s[0:1, :] + (lam_re * c_re - lam_im * c_im)
        him_s[0:1, :] = him_s[0:1, :] + (lam_re * c_im + lam_im * c_re)
        p_re, p_im = lam_re, lam_im
        s = 1
        while s < rows:
            a_re, a_im = hre_s[pl.ds(s, rows - s), :], him_s[pl.ds(s, rows - s), :]
            b_re, b_im = hre_s[pl.ds(0, rows - s), :], him_s[pl.ds(0, rows - s), :]
            hre_s[pl.ds(s, rows - s), :] = a_re + (p_re * b_re - p_im * b_im)
            him_s[pl.ds(s, rows - s), :] = a_im + (p_re * b_im + p_im * b_re)
            p_re, p_im = p_re * p_re - p_im * p_im, 2.0 * p_re * p_im
            s *= 2
        cre_s[...] = hre_s[rows - 1:rows, :]
        cim_s[...] = him_s[rows - 1:rows, :]
        hre_out[0] = hre_s[rows - 1:rows, :]
        him_out[0] = him_s[rows - 1:rows, :]
    else:
        hre_s[...] = hre_s[...] + (lam_re * c_re - lam_im * c_im)
        him_s[...] = him_s[...] + (lam_re * c_im + lam_im * c_re)
        hre_out[0] = hre_s[...]
        him_out[0] = him_s[...]

    for j in range(S5_BLOCKS):
        sl = slice(j * S5_BS, (j + 1) * S5_BS)
        yj = _mxu(hre_s[:, sl], wc_ref[j, 0:S5_BS, :], precise)
        yj = yj + _mxu(him_s[:, sl], wc_ref[j, S5_BS:2 * S5_BS, :], precise)
        y_s[:, j * S5_BC:(j + 1) * S5_BC] = yj
    y = y_s[...] + d_ref[...] * u
    y = jax.nn.gelu(y)
    z = _mxu(y, wglu_ref[...], precise) + bglu_ref[...]
    y = y * jax.nn.sigmoid(z)
    y = y * lax.rsqrt(jnp.mean(y * y, axis=-1, keepdims=True) + EPS) * gout_ref[...]
    y_ref[0] = y.astype(y_ref.dtype)


def s5_mixer(u, h0_re, h0_im, lam_re, lam_im, wb, wc, d_skip, w_glu, b_glu, g_out, *, nseq, rows, nsteps, scan,
             col_block, precise=False):
    st_rows = 1 if scan else rows
    full = lambda shape: pl.BlockSpec(shape, lambda b, t: tuple(0 for _ in shape))
    state_spec = pl.BlockSpec((1, st_rows, S5_NS), lambda b, t: (b, 0, 0))
    y, hre, him = pl.pallas_call(
        functools.partial(_s5_kernel, rows=rows, scan=scan, precise=precise), grid=(nseq, nsteps),
        in_specs=[pl.BlockSpec((1, rows, S5_W), lambda b, t: (b, t, col_block)),
                  state_spec, state_spec, full((1, S5_NS)), full((1, S5_NS)),
                  full((S5_BLOCKS, S5_BC, 2 * S5_BS)), full((S5_BLOCKS, 2 * S5_BS, S5_BC)),
                  full((1, S5_W)), full((S5_W, S5_W)), full((1, S5_W)), full((1, S5_W))],
        out_specs=[pl.BlockSpec((1, rows, S5_W), lambda b, t: (b, t, 0)), state_spec, state_spec],
        out_shape=[_sds((nseq, nsteps * rows, S5_W), F32 if precise else BF16),
                   _sds((nseq, st_rows, S5_NS), F32), _sds((nseq, st_rows, S5_NS), F32)],
        scratch_shapes=[pltpu.VMEM((rows, S5_NS), F32), pltpu.VMEM((rows, S5_NS), F32),
                        pltpu.VMEM((st_rows, S5_NS), F32), pltpu.VMEM((st_rows, S5_NS), F32),
                        pltpu.VMEM((rows, S5_W), F32)],
        compiler_params=_params("parallel", "arbitrary"), name="s5_mixer",
    )(u, h0_re, h0_im, lam_re, lam_im, wb, wc, d_skip.reshape(1, S5_W), w_glu, b_glu.reshape(1, S5_W),
      g_out.reshape(1, S5_W))
    return y, hre, him


def _gla_kernel(qk_ref, v_ref, go_ref, ga_ref, wa_ref, ba_ref, gain_ref, s0_ref, o_ref, st_out,
                st_s, qe_s, kd_s, bl_s, v_s, o_s, *, rows, t_valid, precise):
    t_id = pl.program_id(1)

    @pl.when(t_id == 0)
    def _():
        st_s[...] = s0_ref[0]

    x = _mxu(ga_ref[0], wa_ref[...], precise) + ba_ref[...]
    la = jax.nn.log_sigmoid(x) / GLA_TAU
    r_i = lax.broadcasted_iota(jnp.int32, (rows, rows), 0)
    c_i = lax.broadcasted_iota(jnp.int32, (rows, rows), 1)
    if t_valid < rows:
        la = jnp.where(lax.broadcasted_iota(jnp.int32, la.shape, 0) < t_valid, la, 0.0)
    chunk_shift = GLA_CHUNK.bit_length() - 1
    same = (r_i >> chunk_shift) == (c_i >> chunk_shift)
    causal = same & (c_i <= r_i)
    hi = lax.Precision.HIGHEST
    b = jnp.dot(causal.astype(F32), la, preferred_element_type=F32, precision=hi)
    b_last = jnp.dot(same.astype(F32), la, preferred_element_type=F32, precision=hi)
    qk = qk_ref[0]
    q, k = qk[:, 0:GLA_QK] * (GLA_DK ** -0.5), qk[:, GLA_QK:2 * GLA_QK]
    qe = q * jnp.exp(b)
    ke = k * jnp.exp(-b)
    qe_s[...] = qe
    kd_s[...] = k * jnp.exp(b_last - b)
    bl_s[...] = b_last
    v = v_ref[0]
    v_s[...] = v
    if not precise:
        qe, ke, v = qe.astype(BF16), ke.astype(BF16), v.astype(BF16)
    for h in range(GLA_HEADS):
        ks = slice(h * GLA_DK, (h + 1) * GLA_DK)
        vs = slice(h * GLA_DV, (h + 1) * GLA_DV)
        att = _mxu(qe[:, ks], ke[:, ks], precise, _NT)
        att = jnp.where(causal, att, 0.0)
        o_s[:, vs] = _mxu(att, v[:, vs], precise)

    def chunk_step(j, carry):
        r0 = pl.multiple_of(j * GLA_CHUNK, GLA_CHUNK)
        rs = pl.ds(r0, GLA_CHUNK)
        for h in range(GLA_HEADS):
            ks = slice(h * GLA_DK, (h + 1) * GLA_DK)
            vs = slice(h * GLA_DV, (h + 1) * GLA_DV)
            st = st_s[h]
            o_s[rs, vs] = o_s[rs, vs] + _mxu(qe_s[rs, ks], st, precise, _NT)
            upd = _mxu(v_s[rs, vs], kd_s[rs, ks], precise, _TN)
            st_s[h] = st * jnp.exp(bl_s[pl.ds(r0, 1), ks]) + upd
        return carry

    lax.fori_loop(0, rows // GLA_CHUNK, chunk_step, 0)
    st_out[0] = st_s[...]

    go = go_ref[0]
    for h in range(GLA_HEADS):
        vs = slice(h * GLA_DV, (h + 1) * GLA_DV)
        o = o_s[:, vs]
        y = o * lax.rsqrt(jnp.mean(o * o, axis=-1, keepdims=True) + EPS) * gain_ref[...]
        o_ref[0, :, vs] = (y * jax.nn.silu(go[:, vs])).astype(o_ref.dtype)


def gla_mixer(rest, tail, wa_pad, b_a, gain, s0_t, *, nseq, rows, nsteps, t_valid, precise=False):
    full = lambda shape: pl.BlockSpec(shape, lambda b, t: tuple(0 for _ in shape))
    st_spec = pl.BlockSpec((1, GLA_HEADS, GLA_DV, GLA_DK), lambda b, t: (b, 0, 0, 0))
    o, st = pl.pallas_call(
        functools.partial(_gla_kernel, rows=rows, t_valid=t_valid, precise=precise), grid=(nseq, nsteps),
        in_specs=[pl.BlockSpec((1, rows, 2 * GLA_QK), lambda b, t: (b, t, 1)),
                  pl.BlockSpec((1, rows, GLA_W), lambda b, t: (b, t, 2)),
                  pl.BlockSpec((1, rows, GLA_W), lambda b, t: (b, t, 3)),
                  pl.BlockSpec((1, rows, TAIL_W), lambda b, t: (b, t, 0)),
                  full((TAIL_W, GLA_QK)), full((1, GLA_QK)), full((1, GLA_DV)), st_spec],
        out_specs=[pl.BlockSpec((1, rows, GLA_W), lambda b, t: (b, t, 0)), st_spec],
        out_shape=[_sds((nseq, nsteps * rows, GLA_W), F32 if precise else BF16),
                   _sds((nseq, GLA_HEADS, GLA_DV, GLA_DK), F32)],
        scratch_shapes=[pltpu.VMEM((GLA_HEADS, GLA_DV, GLA_DK), F32), pltpu.VMEM((rows, GLA_QK), F32),
                        pltpu.VMEM((rows, GLA_QK), F32), pltpu.VMEM((rows, GLA_QK), F32),
                        pltpu.VMEM((rows, GLA_W), F32), pltpu.VMEM((rows, GLA_W), F32)],
        compiler_params=_params("parallel", "arbitrary"), name="gla_mixer",
    )(rest, rest, rest, tail, wa_pad, b_a.reshape(1, GLA_QK), gain.reshape(1, GLA_DV), s0_t)
    return o, st


def _mem_attn_kernel(q_ref, k_ref, v_ref, o_ref, *, precise):
    for h in range(MEM_HEADS):
        sl = slice(h * MEM_HD, (h + 1) * MEM_HD)
        s = _mxu(q_ref[0, :, sl], k_ref[0, :, sl], precise, _NT)
        p = jnp.exp(s - jnp.max(s, axis=-1, keepdims=True))
        o = _mxu(p, v_ref[0, :, sl], precise)
        o_ref[0, :, sl] = (o / jnp.sum(p, axis=-1, keepdims=True)).astype(o_ref.dtype)


def memory_attention(q, mk, mv, tq, precise=False):
    b, t, _ = q.shape
    return pl.pallas_call(
        functools.partial(_mem_attn_kernel, precise=precise), grid=(b, t // tq),
        in_specs=[pl.BlockSpec((1, tq, MEM_W), lambda bi, i: (bi, i, 0)),
                  pl.BlockSpec((1, MEM_TOKENS, MEM_W), lambda bi, i: (bi, 0, 0)),
                  pl.BlockSpec((1, MEM_TOKENS, MEM_W), lambda bi, i: (bi, 0, 0))],
        out_specs=pl.BlockSpec((1, tq, MEM_W), lambda bi, i: (bi, i, 0)),
        out_shape=_sds((b, t, MEM_W), F32 if precise else BF16), compiler_params=_params("parallel", "parallel"),
        name="memory_attention",
    )(q, mk, mv)


def _router_gate(logits):
    lane = lax.broadcasted_iota(jnp.int32, logits.shape, 1)
    is_group = (lane >= N_EXPERTS) & (lane < N_EXPERTS + N_GROUPS)
    lg = jnp.where(is_group, logits, NEG)
    eg = jnp.where(is_group, jnp.exp(lg - jnp.max(lg, axis=-1, keepdims=True)), 0.0)
    pg = eg / jnp.sum(eg, axis=-1, keepdims=True)
    g_w = jnp.max(pg, axis=-1, keepdims=True)
    lane_f = lane.astype(F32)
    far = float(4 * LANES)
    g_lane = jnp.min(jnp.where(is_group & (pg == g_w), lane_f, far), axis=-1, keepdims=True)
    group_of_lane = (lane >> (EXP_PER_GROUP.bit_length() - 1)).astype(F32)
    sel = (lane < N_EXPERTS) & (group_of_lane == g_lane - float(N_EXPERTS))
    le = jnp.where(sel, logits, NEG)
    ee = jnp.where(sel, jnp.exp(le - jnp.max(le, axis=-1, keepdims=True)), 0.0)
    pe = ee / jnp.sum(ee, axis=-1, keepdims=True)
    m1 = jnp.max(jnp.where(sel, pe, -1.0), axis=-1, keepdims=True)
    i1 = jnp.min(jnp.where(sel & (pe == m1), lane_f, far), axis=-1, keepdims=True)
    rest = sel & (lane_f != i1)
    m2 = jnp.max(jnp.where(rest, pe, -1.0), axis=-1, keepdims=True)
    i2 = jnp.min(jnp.where(rest & (pe == m2), lane_f, far), axis=-1, keepdims=True)
    tot = m1 + m2
    w = jnp.where(lane_f == i1, m1 / tot, jnp.where(lane_f == i2, m2 / tot, 0.0))
    return w * g_w


def _router_kernel(x_ref, g_ref, w_ref, b_ref, gate_ref):
    x = x_ref[...]
    xn = x * lax.rsqrt(jnp.mean(x * x, axis=-1, keepdims=True) + EPS) * g_ref[...]
    gate_ref[...] = _router_gate(_mxu(xn, w_ref[...], True) + b_ref[...])


def moe_router(x, g, w_r, b_r, tm):
    n, d = x.shape
    return pl.pallas_call(
        _router_kernel, grid=(n // tm,),
        in_specs=[pl.BlockSpec((tm, d), lambda i: (i, 0)), pl.BlockSpec((1, d), lambda i: (0, 0)),
                  pl.BlockSpec((d, LANES), lambda i: (0, 0)), pl.BlockSpec((1, LANES), lambda i: (0, 0))],
        out_specs=pl.BlockSpec((tm, LANES), lambda i: (i, 0)),
        out_shape=_sds((n, LANES), F32), compiler_params=_params("parallel"), name="moe_router",
    )(x, g.reshape(1, d), w_r, b_r)


def _expert_kernel(x_ref, gate_ref, wg_ref, wu_ref, wd_ref, res_ref, o_ref, acc_s, *, precise):
    e = pl.program_id(1)

    @pl.when(e == 0)
    def _():
        acc_s[...] = jnp.zeros(acc_s.shape, F32)

    x = x_ref[...]
    a = jax.nn.silu(_mxu(x, wg_ref[...], precise)) * _mxu(x, wu_ref[...], precise)
    gate = gate_ref[...]
    lane = lax.broadcasted_iota(jnp.int32, gate.shape, 1)
    g_col = jnp.sum(jnp.where(lane == e, gate, 0.0), axis=-1, keepdims=True)
    acc_s[...] += _mxu(a * g_col, wd_ref[...], precise)

    @pl.when(e == pl.num_programs(1) - 1)
    def _():
        o_ref[...] = res_ref[...] + acc_s[...]


def moe_experts(x, gate, w_gate, w_up, w_down, layer, res, tm, precise=False):
    n, d = x.shape
    return pl.pallas_call(
        functools.partial(_expert_kernel, precise=precise), grid=(n // tm, N_EXPERTS),
        in_specs=[pl.BlockSpec((tm, d), lambda i, e: (i, 0)), pl.BlockSpec((tm, LANES), lambda i, e: (i, 0)),
                  pl.BlockSpec((None, None, d, D_EXPERT), lambda i, e: (layer, e, 0, 0)),
                  pl.BlockSpec((None, None, d, D_EXPERT), lambda i, e: (layer, e, 0, 0)),
                  pl.BlockSpec((None, None, D_EXPERT, d), lambda i, e: (layer, e, 0, 0)),
                  pl.BlockSpec((tm, d), lambda i, e: (i, 0))],
        out_specs=pl.BlockSpec((tm, d), lambda i, e: (i, 0)),
        out_shape=_sds((n, d), F32), scratch_shapes=[pltpu.VMEM((tm, d), F32)],
        compiler_params=_params("parallel", "arbitrary"), name="moe_experts",
    )(x, gate, w_gate, w_up, w_down, res)


def _layer_weights(l, p):
    w_in = p["w_in"][l]
    c = np.cumsum([0, FOX_W, FOX_W, FOX_W, FOX_HEADS, S5_W, GLA_QK, GLA_QK, GLA_W, GLA_RANK, GLA_W]).tolist()
    col = lambda i: w_in[:, c[i]:c[i + 1]]
    w = {}
    w["wq"], w["wk"], w["wv"] = (col(i).astype(BF16) for i in (0, 1, 2))
    w["wrest"] = jnp.concatenate([col(4), col(5), col(6), col(7), col(9)], axis=1).astype(BF16)
    w["wtail"] = jnp.concatenate(
        [col(3), col(8), jnp.zeros((D_MODEL, TAIL_W - FOX_HEADS - GLA_RANK), F32)], axis=1).astype(BF16)
    w["tail_bias"] = jnp.concatenate([p["fox_f_bias"][l], jnp.zeros((TAIL_W - FOX_HEADS,), F32)])
    w["q_gain"] = jnp.tile(p["fox_q_gain"][l], FOX_HEADS)
    w["k_gain"] = jnp.tile(p["fox_k_gain"][l], FOX_HEADS)
    lam_re, lam_im, bb_re, bb_im = s5_discretise(
        p["s5_a_re"][l], p["s5_a_im"][l], p["s5_log_dt"][l],
        jnp.swapaxes(p["s5_b_re"][l], 1, 2), jnp.swapaxes(p["s5_b_im"][l], 1, 2))
    gpb = S5_GROUPS // S5_BLOCKS
    w["s5_lam_re"], w["s5_lam_im"] = lam_re.reshape(1, S5_NS), lam_im.reshape(1, S5_NS)
    w["s5_wb_f32"] = jnp.concatenate([_block_diag(bb_re, gpb), _block_diag(bb_im, gpb)], axis=2)
    w["s5_wb"] = w["s5_wb_f32"].astype(BF16)
    c_re_t = jnp.swapaxes(p["s5_c_re"][l], 1, 2)
    c_im_t = jnp.swapaxes(p["s5_c_im"][l], 1, 2)
    w["s5_wc_f32"] = jnp.concatenate([_block_diag(c_re_t, gpb), -_block_diag(c_im_t, gpb)], axis=1)
    w["s5_wc"] = w["s5_wc_f32"].astype(BF16)
    w["s5_wglu"] = p["s5_w_glu"][l].astype(BF16)
    w["gla_wa_f32"] = jnp.concatenate(
        [jnp.zeros((FOX_HEADS, GLA_QK), F32), p["gla_w_a2"][l],
         jnp.zeros((TAIL_W - FOX_HEADS - GLA_RANK, GLA_QK), F32)], axis=0)
    w["gla_wa"] = w["gla_wa_f32"].astype(BF16)
    w["w_out"] = p["w_out"][l].astype(BF16)
    w["mem_wq"], w["mem_wk"], w["mem_wv"], w["mem_wo"] = (
        p[n][l].astype(BF16) for n in ("mem_w_q", "mem_w_k", "mem_w_v", "mem_w_o"))
    w["mem_q_gain"] = jnp.tile(p["mem_q_gain"][l], MEM_HEADS)
    w["mem_k_gain"] = jnp.tile(p["mem_k_gain"][l], MEM_HEADS)
    w["w_router"] = jnp.concatenate(
        [p["moe_w_expert"][l], p["moe_w_group"][l],
         jnp.zeros((D_MODEL, LANES - N_EXPERTS - N_GROUPS), F32)], axis=1)
    w["b_router"] = jnp.concatenate(
        [p["moe_b_expert"][l], p["moe_b_group"][l], jnp.zeros((LANES - N_EXPERTS - N_GROUPS,), F32)]).reshape(1, LANES)
    w["moe_wg"], w["moe_wu"], w["moe_wd"] = (
        p[n][l].astype(BF16)[None] for n in ("moe_w_gate", "moe_w_up", "moe_w_down"))
    return w


def _in_projection(h, w, tm):
    tn = 512
    (q_b,) = matmul(h, w["wq"], tm=tm, tn=tn, out_dtypes=(BF16,), gain=w["q_gain"], post_scale=FOX_HD ** -0.5,
                    name="proj_q")
    k_f, k_b = matmul(h, w["wk"], tm=tm, tn=tn, out_dtypes=(F32, BF16), gain=w["k_gain"], name="proj_k")
    v_f, v_b = matmul(h, w["wv"], tm=tm, tn=tn, out_dtypes=(F32, BF16), name="proj_v")
    (rest,) = matmul(h, w["wrest"], tm=tm, tn=tn, name="proj_rest")
    (tail,) = matmul(h, w["wtail"], tm=tm, tn=TAIL_W, logsig_bias=w["tail_bias"], logsig_lanes=FOX_HEADS,
                     name="proj_tail")
    return q_b, k_f, k_b, v_f, v_b, rest, tail


def _post_mixer(x, cat, mk, mv, w, p, l, b, t, tm, tq_mem):
    n = x.shape[0]
    (x,) = matmul(cat, w["w_out"], tm=tm, tn=512, res=x, name="proj_out")
    hq = rmsnorm_cast(x, p["g_mem_pre"][l], tm)
    (q,) = matmul(hq, w["mem_wq"], tm=tm, tn=512, out_dtypes=(BF16,), gain=w["mem_q_gain"],
                  post_scale=MEM_HD ** -0.5, name="mem_q")
    o = memory_attention(q.reshape(b, t, MEM_W), mk, mv, tq_mem).reshape(n, MEM_W)
    (x,) = matmul(o, w["mem_wo"], tm=tm, tn=512, res=x, name="mem_o")
    hm = rmsnorm_cast(x, p["g_moe"][l], tm)
    gate = moe_router(x, p["g_moe"][l], w["w_router"], w["b_router"], tm)
    return moe_experts(hm, gate, w["moe_wg"], w["moe_wu"], w["moe_wd"], 0, x, tm)


def kernel(x_prompt, x_sample, cache_fox_k, cache_fox_v, cache_fox_logf, state_s5_re, state_s5_im, state_gla,
           cache_mem_k, cache_mem_v, page_table, mem_prompt, g_mix, w_in, fox_q_gain, fox_k_gain, fox_f_bias,
           s5_a_re, s5_a_im, s5_log_dt, s5_b_re, s5_b_im, s5_c_re, s5_c_im, s5_d, s5_w_glu, s5_b_glu,
           gla_w_a2, gla_b_a, gla_gain, g_fox_out, g_s5_out, w_out, g_mem_pre, g_mem_tok, mem_w_q, mem_w_k,
           mem_w_v, mem_q_gain, mem_k_gain, mem_w_o, g_moe, moe_w_group, moe_b_group, moe_w_expert,
           moe_b_expert, moe_w_gate, moe_w_up, moe_w_down):
    p = dict(g_mix=g_mix, w_in=w_in, fox_q_gain=fox_q_gain, fox_k_gain=fox_k_gain, fox_f_bias=fox_f_bias,
             s5_a_re=s5_a_re, s5_a_im=s5_a_im, s5_log_dt=s5_log_dt, s5_b_re=s5_b_re, s5_b_im=s5_b_im,
             s5_c_re=s5_c_re, s5_c_im=s5_c_im, s5_d=s5_d, s5_w_glu=s5_w_glu, s5_b_glu=s5_b_glu,
             gla_w_a2=gla_w_a2, gla_b_a=gla_b_a, gla_gain=gla_gain, g_fox_out=g_fox_out, g_s5_out=g_s5_out,
             w_out=w_out, g_mem_pre=g_mem_pre, g_mem_tok=g_mem_tok, mem_w_q=mem_w_q, mem_w_k=mem_w_k,
             mem_w_v=mem_w_v, mem_q_gain=mem_q_gain, mem_k_gain=mem_k_gain, mem_w_o=mem_w_o, g_moe=g_moe,
             moe_w_group=moe_w_group, moe_b_group=moe_b_group, moe_w_expert=moe_w_expert,
             moe_b_expert=moe_b_expert, moe_w_gate=moe_w_gate, moe_w_up=moe_w_up, moe_w_down=moe_w_down)
    depth = w_in.shape[0]
    b, t, d = x_prompt.shape
    db = x_sample.shape[0]
    n = b * t
    n_mem = b * MEM_TOKENS
    tm = min(512, n)
    tq_fox = min(256, t)
    s5_rows = min(256, t)
    gla_rows = min(256, t)
    tq_mem = min(512, t)

    xp = x_prompt.reshape(n, d)
    xs = jnp.pad(x_sample.reshape(db, d), ((0, SAMPLE_ROWS - db), (0, 0)))
    mem2d = mem_prompt.reshape(n_mem, d)
    cache_ft = jnp.swapaxes(cache_fox_logf, 2, 3)
    zero_s5 = jnp.zeros((b, 1, S5_NS), F32)
    zero_gla = jnp.zeros((b, GLA_HEADS, GLA_DV, GLA_DK), F32)
    outs = {k: [] for k in ("pk", "pv", "pf", "sk", "sv", "sf", "p5r", "p5i", "s5r", "s5i", "pg", "sg", "pmk", "pmv")}

    for l in range(depth):
        w = _layer_weights(l, p)

        h = rmsnorm_cast(xp, g_mix[l], tm)
        q_b, k_f, k_b, v_f, v_b, rest, tail = _in_projection(h, w, tm)
        crow, ccol = fox_gate_cumsum(tail, b, t)
        fox_o = fox_prompt_attention(q_b, k_b, v_b, ccol, crow, g_fox_out[l], b, t, tq_fox)
        rest3, tail3 = rest.reshape(b, t, 4 * S5_W), tail.reshape(b, t, TAIL_W)
        s5_o, hre, him = s5_mixer(rest3, zero_s5, zero_s5, w["s5_lam_re"], w["s5_lam_im"], w["s5_wb"], w["s5_wc"],
                                  s5_d[l], w["s5_wglu"], s5_b_glu[l], g_s5_out[l],
                                  nseq=b, rows=s5_rows, nsteps=t // s5_rows, scan=True, col_block=0)
        gla_o, st = gla_mixer(rest3, tail3, w["gla_wa"], gla_b_a[l], gla_gain[l], zero_gla,
                              nseq=b, rows=gla_rows, nsteps=t // gla_rows, t_valid=gla_rows)
        cat = jnp.concatenate([fox_o, s5_o.reshape(n, S5_W), gla_o.reshape(n, GLA_W)], axis=1)
        outs["pk"].append(k_f.reshape(b, t, FOX_HEADS, FOX_HD))
        outs["pv"].append(v_f.reshape(b, t, FOX_HEADS, FOX_HD))
        outs["pf"].append(tail3[:, :, :FOX_HEADS])
        outs["p5r"].append(hre.reshape(b, S5_GROUPS, S5_STATE))
        outs["p5i"].append(him.reshape(b, S5_GROUPS, S5_STATE))
        outs["pg"].append(jnp.swapaxes(st, 2, 3))
        hmem = rmsnorm_cast(mem2d, g_mem_tok[l], tm=min(512, n_mem))
        (mk,) = matmul(hmem, w["mem_wk"], tm=min(512, n_mem), tn=512, gain=w["mem_k_gain"], name="mem_k")
        (mv,) = matmul(hmem, w["mem_wv"], tm=min(512, n_mem), tn=512, name="mem_v")
        outs["pmk"].append(mk.reshape(b, MEM_TOKENS, MEM_HEADS, MEM_HD))
        outs["pmv"].append(mv.reshape(b, MEM_TOKENS, MEM_HEADS, MEM_HD))
        xp = _post_mixer(xp, cat, mk.reshape(b, MEM_TOKENS, MEM_W), mv.reshape(b, MEM_TOKENS, MEM_W), w, p, l,
                         b, t, tm, tq_mem)

        ns = SAMPLE_ROWS
        h = rmsnorm_cast(xs, g_mix[l], ns, out_dtype=F32)
        c0 = np.cumsum([0, FOX_W, FOX_W, FOX_W, FOX_HEADS, S5_W, GLA_QK, GLA_QK, GLA_W, GLA_RANK, GLA_W]).tolist()
        in_cols = c0[-1]
        qk_gain = jnp.concatenate([w["q_gain"], w["k_gain"]]).reshape(1, 2 * FOX_W)
        f_bias = jnp.concatenate([fox_f_bias[l], jnp.zeros((in_cols - 3 * FOX_W - FOX_HEADS,), F32)]).reshape(1, -1)
        proj = pmatmul(h, w_in, l, tk=256, epi="inproj", aux=(qk_gain, f_bias), name="proj_in_sample")
        col = lambda i: proj[:, c0[i]:c0[i + 1]]
        hd3 = lambda a: a[:db].reshape(db, FOX_HEADS, FOX_HD)
        q_f, k_f, v_f, logf_new = col(0), col(1), col(2), col(3)[:db]
        rest = jnp.concatenate([col(4), col(5), col(6), col(7), col(9)], axis=1)
        tail = jnp.concatenate([col(3), col(8), jnp.zeros((ns, TAIL_W - FOX_HEADS - GLA_RANK), F32)], axis=1)
        fox_o = fox_sample_attention(l, hd3(q_f), hd3(k_f), hd3(v_f), logf_new.reshape(db, FOX_HEADS, 1),
                                     cache_fox_k, cache_fox_v, cache_ft, page_table, g_fox_out[l],
                                     pages_per_step=DECODE_PAGES_PER_STEP)
        fox_o = jnp.pad(fox_o.reshape(db, FOX_W), ((0, ns - db), (0, 0)))
        pad_state = lambda a: jnp.pad(a.reshape(1, db, S5_NS), ((0, 0), (0, ns - db), (0, 0)))
        s5_o, hre, him = s5_mixer(rest.reshape(1, ns, 4 * S5_W), pad_state(state_s5_re[l]), pad_state(state_s5_im[l]),
                                  w["s5_lam_re"], w["s5_lam_im"], w["s5_wb_f32"], w["s5_wc_f32"], s5_d[l],
                                  s5_w_glu[l], s5_b_glu[l], g_s5_out[l], nseq=1, rows=ns, nsteps=1, scan=False,
                                  col_block=0, precise=True)
        seq_pad = lambda a: jnp.pad(a[:db].reshape(db, 1, a.shape[1]), ((0, 0), (0, GLA_CHUNK - 1), (0, 0)))
        gla_o, st = gla_mixer(seq_pad(rest), seq_pad(tail), w["gla_wa_f32"], gla_b_a[l], gla_gain[l],
                              jnp.swapaxes(state_gla[l], 2, 3), nseq=db, rows=GLA_CHUNK, nsteps=1, t_valid=1,
                              precise=True)
        gla_o = jnp.pad(gla_o[:, 0, :], ((0, ns - db), (0, 0)))
        cat = jnp.concatenate([fox_o, s5_o.reshape(ns, S5_W), gla_o], axis=1)
        outs["sk"].append(hd3(k_f).reshape(db, 1, FOX_HEADS, FOX_HD))
        outs["sv"].append(hd3(v_f).reshape(db, 1, FOX_HEADS, FOX_HD))
        outs["sf"].append(logf_new.reshape(db, 1, FOX_HEADS))
        outs["s5r"].append(hre[0, :db].reshape(db, S5_GROUPS, S5_STATE))
        outs["s5i"].append(him[0, :db].reshape(db, S5_GROUPS, S5_STATE))
        outs["sg"].append(jnp.swapaxes(st, 2, 3))
        xs = _post_mixer_sample(xs, cat, cache_mem_k[l].reshape(db, MEM_TOKENS, MEM_W),
                                cache_mem_v[l].reshape(db, MEM_TOKENS, MEM_W), w, p, l, db)

    stk = lambda k: jnp.stack(outs[k])
    return (xp.reshape(b, t, d), xs[:db].reshape(db, 1, d),
            stk("pk"), stk("pv"), stk("pf"), stk("sk"), stk("sv"), stk("sf"),
            stk("p5r"), stk("p5i"), stk("s5r"), stk("s5i"), stk("pg"), stk("sg"), stk("pmk"), stk("pmv"))


def _post_mixer_sample(x, cat, mk, mv, w, p, l, db):
    ns = x.shape[0]
    x = pmatmul(cat, p["w_out"], l, tk=512, epi="res", aux=(x,), name="proj_out_sample")
    hq = rmsnorm_cast(x, p["g_mem_pre"][l], ns, out_dtype=F32)
    q = pmatmul(hq, p["mem_w_q"], l, tk=512, epi="headnorm", aux=(w["mem_q_gain"].reshape(1, MEM_W),),
                post_scale=MEM_HD ** -0.5, name="mem_q_sample")
    q3 = jnp.pad(q[:db].reshape(db, 1, MEM_W), ((0, 0), (0, SAMPLE_ROWS - 1), (0, 0)))
    o = memory_attention(q3, mk, mv, SAMPLE_ROWS, precise=True)[:, 0, :]
    o = jnp.pad(o, ((0, ns - db), (0, 0)))
    x = pmatmul(o, p["mem_w_o"], l, tk=MEM_W, epi="res", aux=(x,), name="mem_o_sample")
    hm = rmsnorm_cast(x, p["g_moe"][l], ns, out_dtype=F32)
    gate = pmatmul(hm, w["w_router"][None], 0, tk=512, epi="router", aux=(w["b_router"],), name="router_sample")
    return moe_experts(hm, gate, p["moe_w_gate"], p["moe_w_up"], p["moe_w_down"], l, x, ns, precise=True)
```

```python
import functools
import math

import numpy as np
import jax
import jax.numpy as jnp
from jax import lax
from jax.experimental import pallas as pl
from jax.experimental.pallas import tpu as pltpu

F32, BF16 = jnp.float32, jnp.bfloat16

D_MODEL = 2048
FOX_HEADS, FOX_HD = 8, 128
FOX_W = FOX_HEADS * FOX_HD
S5_W, S5_GROUP, S5_GROUPS, S5_STATE = 512, 16, 32, 64
S5_NS = S5_GROUPS * S5_STATE
S5_BLOCKS = 4
S5_BC = S5_W // S5_BLOCKS
S5_BS = S5_NS // S5_BLOCKS
GLA_HEADS, GLA_DK, GLA_DV = 4, 64, 128
GLA_W = GLA_HEADS * GLA_DV
GLA_QK = GLA_HEADS * GLA_DK
GLA_RANK = 16
GLA_TAU = 16.0
GLA_CHUNK = 16
MEM_TOKENS, MEM_HEADS, MEM_HD = 256, 4, 128
MEM_W = MEM_HEADS * MEM_HD
N_GROUPS, EXP_PER_GROUP, N_EXPERTS, D_EXPERT = 4, 4, 16, 256
PAGE_SIZE = 128
EPS = 1e-6
NEG = -1e30
LOG2E = math.log2(math.e)

LANES = 128
VMEM_LIMIT_BYTES = 48 * 1024 * 1024

DECODE_GROUP = 16
DECODE_PAGES_PER_STEP = 8
TAIL_W = LANES
SAMPLE_ROWS = 16


def _params(*sem):
    return pltpu.CompilerParams(dimension_semantics=sem, vmem_limit_bytes=VMEM_LIMIT_BYTES)


def _sds(shape, dtype):
    return jax.ShapeDtypeStruct(shape, dtype)


_NN = (((1,), (0,)), ((), ()))
_NT = (((1,), (1,)), ((), ()))
_TN = (((0,), (0,)), ((), ()))


def _mxu(a, b, precise, dims=_NN):
    if precise:
        return lax.dot_general(a.astype(F32), b.astype(F32), dims, preferred_element_type=F32,
                               precision=lax.Precision.HIGHEST)
    return lax.dot_general(a.astype(BF16), b.astype(BF16), dims, preferred_element_type=F32)


def _rms_kernel(x_ref, g_ref, o_ref):
    x = x_ref[...]
    y = x * lax.rsqrt(jnp.mean(x * x, axis=-1, keepdims=True) + EPS)
    o_ref[...] = (y * g_ref[...]).astype(o_ref.dtype)


def rmsnorm_cast(x, g, tm, out_dtype=BF16):
    n, d = x.shape
    return pl.pallas_call(
        _rms_kernel, grid=(n // tm,),
        in_specs=[pl.BlockSpec((tm, d), lambda i: (i, 0)), pl.BlockSpec((1, d), lambda i: (0, 0))],
        out_specs=pl.BlockSpec((tm, d), lambda i: (i, 0)),
        out_shape=_sds((n, d), out_dtype), compiler_params=_params("parallel"), name="rmsnorm_cast",
    )(x, g.reshape(1, d))


def _mm_kernel(*refs, headnorm, post_scale, logsig_lanes, has_res, n_out):
    x_ref, w_ref = refs[0], refs[1]
    pos = 2
    gain_ref = bias_ref = res_ref = None
    if headnorm:
        gain_ref, pos = refs[pos], pos + 1
    if logsig_lanes:
        bias_ref, pos = refs[pos], pos + 1
    if has_res:
        res_ref, pos = refs[pos], pos + 1
    outs = refs[pos:pos + n_out]
    acc = jnp.dot(x_ref[...], w_ref[...], preferred_element_type=F32)
    if logsig_lanes:
        lane = lax.broadcasted_iota(jnp.int32, acc.shape, 1)
        acc = jnp.where(lane < logsig_lanes, jax.nn.log_sigmoid(acc + bias_ref[...]), acc)
    if has_res:
        acc = res_ref[...] + acc
    if headnorm:
        for c in range(acc.shape[1] // LANES):
            sl = slice(c * LANES, (c + 1) * LANES)
            blk = acc[:, sl]
            y = blk * lax.rsqrt(jnp.mean(blk * blk, axis=-1, keepdims=True) + EPS)
            y = y * gain_ref[:, sl]
            if post_scale != 1.0:
                y = y * post_scale
            for o in outs:
                o[:, sl] = y.astype(o.dtype)
    else:
        for o in outs:
            o[...] = acc.astype(o.dtype)


def matmul(x, w, *, tm, tn, out_dtypes=(F32,), gain=None, post_scale=1.0, logsig_bias=None,
           logsig_lanes=0, res=None, layer=None, cols=None, name="matmul"):
    n, k = x.shape
    col0, c = cols if cols is not None else (0, w.shape[-1])
    jb = col0 // tn
    args = [x, w]
    if layer is None:
        w_spec = pl.BlockSpec((k, tn), lambda i, j: (0, jb + j))
    else:
        w_spec = pl.BlockSpec((None, k, tn), lambda i, j: (layer, 0, jb + j))
    in_specs = [pl.BlockSpec((tm, k), lambda i, j: (i, 0)), w_spec]
    if gain is not None:
        args.append(gain.reshape(1, c).astype(F32))
        in_specs.append(pl.BlockSpec((1, tn), lambda i, j: (0, j)))
    if logsig_lanes:
        args.append(logsig_bias.reshape(1, c).astype(F32))
        in_specs.append(pl.BlockSpec((1, tn), lambda i, j: (0, j)))
    if res is not None:
        args.append(res)
        in_specs.append(pl.BlockSpec((tm, tn), lambda i, j: (i, j)))
    outs = pl.pallas_call(
        functools.partial(_mm_kernel, headnorm=gain is not None, post_scale=post_scale,
                          logsig_lanes=logsig_lanes, has_res=res is not None, n_out=len(out_dtypes)),
        grid=(n // tm, c // tn), in_specs=in_specs,
        out_specs=[pl.BlockSpec((tm, tn), lambda i, j: (i, j)) for _ in out_dtypes],
        out_shape=[_sds((n, c), dt) for dt in out_dtypes],
        compiler_params=_params("parallel", "parallel"), name=name,
    )(*args)
    return outs


def _proj_norm_kernel(*refs, ks):
    nx = len(ks)
    x_refs, w_ref, res_ref, g_ref, o_ref, h_ref = refs[:nx], refs[nx], refs[nx + 1], refs[nx + 2], refs[nx + 3], refs[nx + 4]
    acc, off = None, 0
    for x_ref, k in zip(x_refs, ks):
        part = jnp.dot(x_ref[...], w_ref[off:off + k, :], preferred_element_type=F32)
        acc = part if acc is None else acc + part
        off += k
    x_new = res_ref[...] + acc
    o_ref[...] = x_new
    y = x_new * lax.rsqrt(jnp.mean(x_new * x_new, axis=-1, keepdims=True) + EPS)
    h_ref[...] = (y * g_ref[...]).astype(h_ref.dtype)


def proj_res_norm(xs, w, layer, res, g, tm, name):
    n, d = res.shape
    ks = tuple(x.shape[1] for x in xs)
    ktot = sum(ks)
    in_specs = [pl.BlockSpec((tm, k), lambda i: (i, 0)) for k in ks]
    in_specs += [pl.BlockSpec((None, ktot, d), lambda i: (layer, 0, 0)), pl.BlockSpec((tm, d), lambda i: (i, 0)),
                 pl.BlockSpec((1, d), lambda i: (0, 0))]
    return pl.pallas_call(
        functools.partial(_proj_norm_kernel, ks=ks), grid=(n // tm,), in_specs=in_specs,
        out_specs=[pl.BlockSpec((tm, d), lambda i: (i, 0)), pl.BlockSpec((tm, d), lambda i: (i, 0))],
        out_shape=[_sds((n, d), F32), _sds((n, d), BF16)], compiler_params=_params("parallel"), name=name,
    )(*xs, w, res, g.reshape(1, d))


def _head_rmsnorm(blk, gain):
    return blk * lax.rsqrt(jnp.mean(blk * blk, axis=-1, keepdims=True) + EPS) * gain


def _pmm_kernel(*refs, epi, n_aux, post_scale):
    x_ref, w_ref = refs[0], refs[1]
    aux = refs[2:2 + n_aux]
    o_ref, acc_s = refs[2 + n_aux], refs[3 + n_aux]
    k = pl.program_id(0)

    @pl.when(k == 0)
    def _():
        acc_s[...] = jnp.zeros(acc_s.shape, F32)

    acc_s[...] += _mxu(x_ref[...], w_ref[...], True)

    @pl.when(k == pl.num_programs(0) - 1)
    def _():
        acc = acc_s[...]
        if epi == "none":
            o_ref[...] = acc
        elif epi == "res":
            o_ref[...] = aux[0][...] + acc
        elif epi == "headnorm":
            for c in range(acc.shape[1] // LANES):
                sl = slice(c * LANES, (c + 1) * LANES)
                o_ref[:, sl] = _head_rmsnorm(acc[:, sl], aux[0][:, sl]) * post_scale
        elif epi == "inproj":
            gain_ref, bias_ref = aux
            for c in range(2 * FOX_W // LANES):
                sl = slice(c * LANES, (c + 1) * LANES)
                o_ref[:, sl] = _head_rmsnorm(acc[:, sl], gain_ref[:, sl])
            o_ref[:, 2 * FOX_W:3 * FOX_W] = acc[:, 2 * FOX_W:3 * FOX_W]
            t = acc[:, 3 * FOX_W:]
            lane = lax.broadcasted_iota(jnp.int32, t.shape, 1)
            o_ref[:, 3 * FOX_W:] = jnp.where(lane < FOX_HEADS, jax.nn.log_sigmoid(t + bias_ref[...]), t)
        elif epi == "router":
            o_ref[...] = _router_gate(acc + aux[0][...])


def pmatmul(x, w, layer, *, tk, epi="none", aux=(), post_scale=1.0, name="pmatmul"):
    rows, kdim = x.shape
    c = w.shape[2]
    in_specs = [pl.BlockSpec((rows, tk), lambda k: (0, k)), pl.BlockSpec((None, tk, c), lambda k: (layer, k, 0))]
    in_specs += [pl.BlockSpec(a.shape, lambda k: (0, 0)) for a in aux]
    return pl.pallas_call(
        functools.partial(_pmm_kernel, epi=epi, n_aux=len(aux), post_scale=post_scale), grid=(kdim // tk,),
        in_specs=in_specs, out_specs=pl.BlockSpec((rows, c), lambda k: (0, 0)),
        out_shape=_sds((rows, c), F32), scratch_shapes=[pltpu.VMEM((rows, c), F32)],
        compiler_params=_params("arbitrary"), name=name,
    )(x, w, *aux)


def _fox_gate_kernel(t_ref, crow_ref, ccol_ref):
    x = t_ref[0]
    c = x.T[0:FOX_HEADS, :]
    t_len = c.shape[1]
    lane = lax.broadcasted_iota(jnp.int32, c.shape, 1)
    s = 1
    while s < t_len:
        c = c + jnp.where(lane >= s, pltpu.roll(c, s, axis=1), 0.0)
        s *= 2
    c = c * LOG2E
    crow_ref[0] = c
    cpad = jnp.concatenate([c, jnp.zeros((LANES - FOX_HEADS, t_len), F32)], axis=0)
    ccol_ref[0] = cpad.T


def fox_gate_cumsum(tail, b, t):
    return pl.pallas_call(
        _fox_gate_kernel, grid=(b,),
        in_specs=[pl.BlockSpec((1, t, TAIL_W), lambda i: (i, 0, 0))],
        out_specs=[pl.BlockSpec((1, FOX_HEADS, t), lambda i: (i, 0, 0)),
                   pl.BlockSpec((1, t, TAIL_W), lambda i: (i, 0, 0))],
        out_shape=[_sds((b, FOX_HEADS, t), F32), _sds((b, t, TAIL_W), F32)],
        compiler_params=_params("parallel"), name="fox_gate_cumsum",
    )(tail.reshape(b, t, TAIL_W))


def _lane_tile(x, n):
    return x if n == 1 else jnp.concatenate([x] * n, axis=1)


def _fox_attn_kernel(qi_tab, ki_tab, q_ref, k_ref, v_ref, cq_ref, ck_ref, g_ref, o_ref, m_s, l_s, acc_s, cq_s,
                     *, tq, tk):
    p_id = pl.program_id(1)
    qi, ki = qi_tab[p_id], ki_tab[p_id]
    nrep = tk // LANES

    @pl.when(ki == 0)
    def _():
        m_s[...] = jnp.full(m_s.shape, NEG, F32)
        l_s[...] = jnp.zeros(l_s.shape, F32)
        acc_s[...] = jnp.zeros(acc_s.shape, F32)
        for h in range(FOX_HEADS):
            cq_s[h] = jnp.broadcast_to(cq_ref[0, :, h:h + 1], (tq, LANES))

    def step(masked):
        if masked:
            row = lax.broadcasted_iota(jnp.int32, (tq, tk), 0)
            col = lax.broadcasted_iota(jnp.int32, (tq, tk), 1)
            causal = col <= row
        for h in range(FOX_HEADS):
            sl = slice(h * FOX_HD, (h + 1) * FOX_HD)
            s = lax.dot_general(q_ref[0, :, sl], k_ref[0, :, sl], _NT, preferred_element_type=F32)
            s = (s + _lane_tile(cq_s[h], nrep)) - ck_ref[0, h:h + 1, :]
            if masked:
                s = jnp.where(causal, s, NEG)
            m_old = m_s[h]
            m_new = jnp.maximum(m_old, jnp.broadcast_to(jnp.max(s, axis=-1, keepdims=True), (tq, LANES)))
            alpha = jnp.exp2(m_old - m_new)
            p = jnp.exp2(s - _lane_tile(m_new, nrep))
            l_s[h] = l_s[h] * alpha + jnp.broadcast_to(jnp.sum(p, axis=-1, keepdims=True), (tq, LANES))
            m_s[h] = m_new
            acc_s[:, sl] = acc_s[:, sl] * alpha + jnp.dot(p.astype(BF16), v_ref[0, :, sl],
                                                          preferred_element_type=F32)

    @pl.when(ki < qi)
    def _():
        step(False)

    @pl.when(ki == qi)
    def _():
        step(True)
        ssq = jnp.zeros((tq, 1), F32)
        for h in range(FOX_HEADS):
            sl = slice(h * FOX_HD, (h + 1) * FOX_HD)
            o = acc_s[:, sl] / l_s[h]
            acc_s[:, sl] = o
            ssq = ssq + jnp.sum(o * o, axis=-1, keepdims=True)
        inv = lax.rsqrt(ssq / FOX_W + EPS)
        o_ref[0] = (acc_s[...] * inv * g_ref[...]).astype(o_ref.dtype)


def fox_prompt_attention(q, k, v, ccol, crow, g_out, b, t, tq):
    nq = t // tq
    pairs = [(i, j) for i in range(nq) for j in range(i + 1)]
    qi_tab = jnp.asarray(np.array([p[0] for p in pairs], np.int32))
    ki_tab = jnp.asarray(np.array([p[1] for p in pairs], np.int32))
    q3, k3, v3 = (a.reshape(b, t, FOX_W) for a in (q, k, v))
    grid_spec = pltpu.PrefetchScalarGridSpec(
        num_scalar_prefetch=2, grid=(b, len(pairs)),
        in_specs=[
            pl.BlockSpec((1, tq, FOX_W), lambda bi, p, qt, kt: (bi, qt[p], 0)),
            pl.BlockSpec((1, tq, FOX_W), lambda bi, p, qt, kt: (bi, kt[p], 0)),
            pl.BlockSpec((1, tq, FOX_W), lambda bi, p, qt, kt: (bi, kt[p], 0)),
            pl.BlockSpec((1, tq, TAIL_W), lambda bi, p, qt, kt: (bi, qt[p], 0)),
            pl.BlockSpec((1, FOX_HEADS, tq), lambda bi, p, qt, kt: (bi, 0, kt[p])),
            pl.BlockSpec((1, FOX_W), lambda bi, p, qt, kt: (0, 0)),
        ],
        out_specs=pl.BlockSpec((1, tq, FOX_W), lambda bi, p, qt, kt: (bi, qt[p], 0)),
        scratch_shapes=[pltpu.VMEM((FOX_HEADS, tq, LANES), F32), pltpu.VMEM((FOX_HEADS, tq, LANES), F32),
                        pltpu.VMEM((tq, FOX_W), F32), pltpu.VMEM((FOX_HEADS, tq, LANES), F32)],
    )
    out = pl.pallas_call(
        functools.partial(_fox_attn_kernel, tq=tq, tk=tq), grid_spec=grid_spec,
        out_shape=_sds((b, t, FOX_W), BF16),
        compiler_params=_params("parallel", "arbitrary"), name="fox_prompt_attention",
    )(qi_tab, ki_tab, q3, k3, v3, ccol, crow, g_out.reshape(1, FOX_W))
    return out.reshape(b * t, FOX_W)


def _fox_decode_kernel(pt_ref, q_ref, kn_ref, vn_ref, fn_ref, g_ref, *refs, pages_per_step, scale):
    pp = pages_per_step
    k_refs, v_refs, f_refs = refs[0:pp], refs[pp:2 * pp], refs[2 * pp:3 * pp]
    o_ref = refs[3 * pp]
    m_s, l_s, acc_s, r_s = refs[3 * pp + 1:]
    step_id = pl.program_id(1)
    q = q_ref[0] * scale

    @pl.when(step_id == 0)
    def _():
        m_s[...] = jnp.sum(q * kn_ref[0], axis=-1, keepdims=True)
        l_s[...] = jnp.ones(l_s.shape, F32)
        acc_s[...] = vn_ref[0]
        r_s[...] = fn_ref[0]

    tok_r = lax.broadcasted_iota(jnp.int32, (PAGE_SIZE, PAGE_SIZE), 0)
    tok_c = lax.broadcasted_iota(jnp.int32, (PAGE_SIZE, PAGE_SIZE), 1)
    later = (tok_r > tok_c).astype(F32)
    r_run = r_s[...]
    parts = []
    grp = DECODE_GROUP
    tok = lax.broadcasted_iota(jnp.int32, (grp, 1, FOX_HD), 0)
    lane = lax.broadcasted_iota(jnp.int32, (grp, 1, FOX_HD), 2)
    for j in range(pp):
        f = f_refs[j][...]
        r = jnp.dot(f, later, preferred_element_type=F32, precision=lax.Precision.HIGHEST) + r_run
        r_run = r_run + jnp.sum(f, axis=-1, keepdims=True)
        for g in range(PAGE_SIZE // grp):
            rows = pl.ds(g * grp, grp)
            prod = k_refs[j][rows] * q[None] + jnp.where(lane == tok + g * grp, r[None], 0.0)
            s = jnp.sum(prod, axis=-1, keepdims=True)
            m_g = jnp.max(s, axis=0)
            p = jnp.exp(s - m_g[None])
            parts.append((m_g, jnp.sum(p, axis=0), jnp.sum(p * v_refs[j][rows], axis=0)))
    m_old = m_s[...]
    m_new = m_old
    for m_j, _, _ in parts:
        m_new = jnp.maximum(m_new, m_j)
    alpha = jnp.exp(m_old - m_new)
    l_run, acc = l_s[...] * alpha, acc_s[...] * alpha
    for m_j, l_j, pv_j in parts:
        a_j = jnp.exp(m_j - m_new)
        l_run, acc = l_run + l_j * a_j, acc + pv_j * a_j
    m_s[...], l_s[...], acc_s[...], r_s[...] = m_new, l_run, acc, r_run

    @pl.when(step_id == pl.num_programs(1) - 1)
    def _():
        o = acc_s[...] / l_s[...]
        ssq = jnp.sum(jnp.sum(o * o, axis=-1, keepdims=True), axis=0, keepdims=True)
        o_ref[0] = (o * lax.rsqrt(ssq / FOX_W + EPS) * g_ref[...]).astype(o_ref.dtype)


def fox_sample_attention(layer, q, k_new, v_new, logf_new, cache_k, cache_v, cache_ft, page_table, g_out,
                         pages_per_step):
    db, n_pages = page_table.shape
    n_steps = n_pages // pages_per_step

    def page_map(j):
        def im(bi, si, pt):
            return (layer, pt[bi, n_pages - 1 - (si * pages_per_step + j)], 0, 0, 0)
        return im

    def page_map_f(j):
        def im(bi, si, pt):
            return (layer, pt[bi, n_pages - 1 - (si * pages_per_step + j)], 0, 0)
        return im

    vec = pl.BlockSpec((1, FOX_HEADS, FOX_HD), lambda bi, si, pt: (bi, 0, 0))
    in_specs = [vec, vec, vec,
                pl.BlockSpec((1, FOX_HEADS, 1), lambda bi, si, pt: (bi, 0, 0)),
                pl.BlockSpec((FOX_HEADS, FOX_HD), lambda bi, si, pt: (0, 0))]
    in_specs += [pl.BlockSpec((None, None, PAGE_SIZE, FOX_HEADS, FOX_HD), page_map(j)) for j in range(pages_per_step)]
    in_specs += [pl.BlockSpec((None, None, PAGE_SIZE, FOX_HEADS, FOX_HD), page_map(j)) for j in range(pages_per_step)]
    in_specs += [pl.BlockSpec((None, None, FOX_HEADS, PAGE_SIZE), page_map_f(j)) for j in range(pages_per_step)]
    grid_spec = pltpu.PrefetchScalarGridSpec(
        num_scalar_prefetch=1, grid=(db, n_steps), in_specs=in_specs,
        out_specs=pl.BlockSpec((1, FOX_HEADS, FOX_HD), lambda bi, si, pt: (bi, 0, 0)),
        scratch_shapes=[pltpu.VMEM((FOX_HEADS, 1), F32), pltpu.VMEM((FOX_HEADS, 1), F32),
                        pltpu.VMEM((FOX_HEADS, FOX_HD), F32), pltpu.VMEM((FOX_HEADS, 1), F32)],
    )
    return pl.pallas_call(
        functools.partial(_fox_decode_kernel, pages_per_step=pages_per_step, scale=FOX_HD ** -0.5),
        grid_spec=grid_spec, out_shape=_sds((db, FOX_HEADS, FOX_HD), F32),
        compiler_params=_params("parallel", "arbitrary"), name="fox_sample_attention",
    )(page_table, q, k_new, v_new, logf_new, g_out.reshape(FOX_HEADS, FOX_HD),
      *([cache_k] * pages_per_step), *([cache_v] * pages_per_step), *([cache_ft] * pages_per_step))


def _s5_param_kernel(are_ref, aim_ref, ldt_ref, bre_ref, bim_ref, lre_ref, lim_ref, bbre_ref, bbim_ref):
    a_re, a_im = are_ref[...], aim_ref[...]
    dt = jnp.exp(ldt_ref[...])
    mag = jnp.exp(a_re * dt)
    lam_re, lam_im = mag * jnp.cos(a_im * dt), mag * jnp.sin(a_im * dt)
    den = a_re * a_re + a_im * a_im
    z_re = ((lam_re - 1.0) * a_re + lam_im * a_im) / den
    z_im = (lam_im * a_re - (lam_re - 1.0) * a_im) / den
    lre_ref[...] = lam_re
    lim_ref[...] = lam_im
    b_re, b_im = bre_ref[...], bim_ref[...]
    bbre_ref[...] = z_re[:, None, :] * b_re - z_im[:, None, :] * b_im
    bbim_ref[...] = z_re[:, None, :] * b_im + z_im[:, None, :] * b_re


def s5_discretise(a_re, a_im, log_dt, b_re_t, b_im_t):
    g, p = a_re.shape
    c = b_re_t.shape[1]
    return pl.pallas_call(
        _s5_param_kernel,
        out_shape=[_sds((g, p), F32), _sds((g, p), F32), _sds((g, c, p), F32), _sds((g, c, p), F32)],
        name="s5_discretise",
    )(a_re, a_im, log_dt.reshape(g, 1), b_re_t, b_im_t)


def _block_diag(w, groups_per_block):
    g, a, b = w.shape
    nb = g // groups_per_block
    eye = jnp.eye(groups_per_block, dtype=w.dtype)
    w5 = w.reshape(nb, groups_per_block, a, b)
    return jnp.einsum("jgab,gh->jgahb", w5, eye).reshape(nb, groups_per_block * a, groups_per_block * b)


def _s5_kernel(u_ref, h0re_ref, h0im_ref, lre_ref, lim_ref, wb_ref, wc_ref, d_ref, wglu_ref, bglu_ref, gout_ref,
               y_ref, hre_out, him_out, hre_s, him_s, cre_s, cim_s, y_s, *, rows, scan, precise):
    t_id = pl.program_id(1)
    lam_re, lam_im = lre_ref[...], lim_ref[...]

    @pl.when(t_id == 0)
    def _():
        cre_s[...] = h0re_ref[0]
        cim_s[...] = h0im_ref[0]

    u = u_ref[0]
    for j in range(S5_BLOCKS):
        bu = _mxu(u[:, j * S5_BC:(j + 1) * S5_BC], wb_ref[j], precise)
        hre_s[:, j * S5_BS:(j + 1) * S5_BS] = bu[:, :S5_BS]
        him_s[:, j * S5_BS:(j + 1) * S5_BS] = bu[:, S5_BS:]
    c_re, c_im = cre_s[...], cim_s[...]
    if scan:
        hre_s[0:1, :] = hre_s[0:1, :] + (lam_re * c_re - lam_im * c_im)
        him_s[0:1, :] = him_s[0:1, :] + (lam_re * c_im + lam_im * c_re)
        p_re, p_im = lam_re, lam_im
        s = 1
        while s < rows:
            a_re, a_im = hre_s[pl.ds(s, rows - s), :], him_s[pl.ds(s, rows - s), :]
            b_re, b_im = hre_s[pl.ds(0, rows - s), :], him_s[pl.ds(0, rows - s), :]
            hre_s[pl.ds(s, rows - s), :] = a_re + (p_re * b_re - p_im * b_im)
            him_s[pl.ds(s, rows - s), :] = a_im + (p_re * b_im + p_im * b_re)
            p_re, p_im = p_re * p_re - p_im * p_im, 2.0 * p_re * p_im
            s *= 2
        cre_s[...] = hre_s[rows - 1:rows, :]
        cim_s[...] = him_s[rows - 1:rows, :]
        hre_out[0] = hre_s[rows - 1:rows, :]
        him_out[0] = him_s[rows - 1:rows, :]
    else:
        hre_s[...] = hre_s[...] + (lam_re * c_re - lam_im * c_im)
        him_s[...] = him_s[...] + (lam_re * c_im + lam_im * c_re)
        hre_out[0] = hre_s[...]
        him_out[0] = him_s[...]

    for j in range(S5_BLOCKS):
        sl = slice(j * S5_BS, (j + 1) * S5_BS)
        yj = _mxu(hre_s[:, sl], wc_ref[j, 0:S5_BS, :], precise)
        yj = yj + _mxu(him_s[:, sl], wc_ref[j, S5_BS:2 * S5_BS, :], precise)
        y_s[:, j * S5_BC:(j + 1) * S5_BC] = yj
    y = y_s[...] + d_ref[...] * u
    y = jax.nn.gelu(y)
    z = _mxu(y, wglu_ref[...], precise) + bglu_ref[...]
    y = y * jax.nn.sigmoid(z)
    y = y * lax.rsqrt(jnp.mean(y * y, axis=-1, keepdims=True) + EPS) * gout_ref[...]
    y_ref[0] = y.astype(y_ref.dtype)


def s5_mixer(u, h0_re, h0_im, lam_re, lam_im, wb, wc, d_skip, w_glu, b_glu, g_out, *, nseq, rows, nsteps, scan,
             col_block, precise=False):
    st_rows = 1 if scan else rows
    full = lambda shape: pl.BlockSpec(shape, lambda b, t: tuple(0 for _ in shape))
    state_spec = pl.BlockSpec((1, st_rows, S5_NS), lambda b, t: (b, 0, 0))
    y, hre, him = pl.pallas_call(
        functools.partial(_s5_kernel, rows=rows, scan=scan, precise=precise), grid=(nseq, nsteps),
        in_specs=[pl.BlockSpec((1, rows, S5_W), lambda b, t: (b, t, col_block)),
                  state_spec, state_spec, full((1, S5_NS)), full((1, S5_NS)),
                  full((S5_BLOCKS, S5_BC, 2 * S5_BS)), full((S5_BLOCKS, 2 * S5_BS, S5_BC)),
                  full((1, S5_W)), full((S5_W, S5_W)), full((1, S5_W)), full((1, S5_W))],
        out_specs=[pl.BlockSpec((1, rows, S5_W), lambda b, t: (b, t, 0)), state_spec, state_spec],
        out_shape=[_sds((nseq, nsteps * rows, S5_W), F32 if precise else BF16),
                   _sds((nseq, st_rows, S5_NS), F32), _sds((nseq, st_rows, S5_NS), F32)],
        scratch_shapes=[pltpu.VMEM((rows, S5_NS), F32), pltpu.VMEM((rows, S5_NS), F32),
                        pltpu.VMEM((st_rows, S5_NS), F32), pltpu.VMEM((st_rows, S5_NS), F32),
                        pltpu.VMEM((rows, S5_W), F32)],
        compiler_params=_params("parallel", "arbitrary"), name="s5_mixer",
    )(u, h0_re, h0_im, lam_re, lam_im, wb, wc, d_skip.reshape(1, S5_W), w_glu, b_glu.reshape(1, S5_W),
      g_out.reshape(1, S5_W))
    return y, hre, him


def _gla_kernel(qk_ref, v_ref, go_ref, ga_ref, wa_ref, ba_ref, gain_ref, s0_ref, o_ref, st_out,
                st_s, qe_s, kd_s, bl_s, v_s, o_s, *, rows, t_valid, precise):
    t_id = pl.program_id(1)

    @pl.when(t_id == 0)
    def _():
        st_s[...] = s0_ref[0]

    x = _mxu(ga_ref[0], wa_ref[...], precise) + ba_ref[...]
    la = jax.nn.log_sigmoid(x) / GLA_TAU
    r_i = lax.broadcasted_iota(jnp.int32, (rows, rows), 0)
    c_i = lax.broadcasted_iota(jnp.int32, (rows, rows), 1)
    if t_valid < rows:
        la = jnp.where(lax.broadcasted_iota(jnp.int32, la.shape, 0) < t_valid, la, 0.0)
    chunk_shift = GLA_CHUNK.bit_length() - 1
    same = (r_i >> chunk_shift) == (c_i >> chunk_shift)
    causal = same & (c_i <= r_i)
    hi = lax.Precision.HIGHEST
    b = jnp.dot(causal.astype(F32), la, preferred_element_type=F32, precision=hi)
    b_last = jnp.dot(same.astype(F32), la, preferred_element_type=F32, precision=hi)
    qk = qk_ref[0]
    q, k = qk[:, 0:GLA_QK] * (GLA_DK ** -0.5), qk[:, GLA_QK:2 * GLA_QK]
    qe = q * jnp.exp(b)
    ke = k * jnp.exp(-b)
    qe_s[...] = qe
    kd_s[...] = k * jnp.exp(b_last - b)
    bl_s[...] = b_last
    v = v_ref[0]
    v_s[...] = v
    if not precise:
        qe, ke, v = qe.astype(BF16), ke.astype(BF16), v.astype(BF16)
    for h in range(GLA_HEADS):
        ks = slice(h * GLA_DK, (h + 1) * GLA_DK)
        vs = slice(h * GLA_DV, (h + 1) * GLA_DV)
        att = _mxu(qe[:, ks], ke[:, ks], precise, _NT)
        att = jnp.where(causal, att, 0.0)
        o_s[:, vs] = _mxu(att, v[:, vs], precise)

    def chunk_step(j, carry):
        r0 = pl.multiple_of(j * GLA_CHUNK, GLA_CHUNK)
        rs = pl.ds(r0, GLA_CHUNK)
        for h in range(GLA_HEADS):
            ks = slice(h * GLA_DK, (h + 1) * GLA_DK)
            vs = slice(h * GLA_DV, (h + 1) * GLA_DV)
            st = st_s[h]
            o_s[rs, vs] = o_s[rs, vs] + _mxu(qe_s[rs, ks], st, precise, _NT)
            upd = _mxu(v_s[rs, vs], kd_s[rs, ks], precise, _TN)
            st_s[h] = st * jnp.exp(bl_s[pl.ds(r0, 1), ks]) + upd
        return carry

    lax.fori_loop(0, rows // GLA_CHUNK, chunk_step, 0)
    st_out[0] = st_s[...]

    go = go_ref[0]
    for h in range(GLA_HEADS):
        vs = slice(h * GLA_DV, (h + 1) * GLA_DV)
        o = o_s[:, vs]
        y = o * lax.rsqrt(jnp.mean(o * o, axis=-1, keepdims=True) + EPS) * gain_ref[...]
        o_ref[0, :, vs] = (y * jax.nn.silu(go[:, vs])).astype(o_ref.dtype)


def gla_mixer(rest, tail, wa_pad, b_a, gain, s0_t, *, nseq, rows, nsteps, t_valid, precise=False):
    full = lambda shape: pl.BlockSpec(shape, lambda b, t: tuple(0 for _ in shape))
    st_spec = pl.BlockSpec((1, GLA_HEADS, GLA_DV, GLA_DK), lambda b, t: (b, 0, 0, 0))
    o, st = pl.pallas_call(
        functools.partial(_gla_kernel, rows=rows, t_valid=t_valid, precise=precise), grid=(nseq, nsteps),
        in_specs=[pl.BlockSpec((1, rows, 2 * GLA_QK), lambda b, t: (b, t, 1)),
                  pl.BlockSpec((1, rows, GLA_W), lambda b, t: (b, t, 2)),
                  pl.BlockSpec((1, rows, GLA_W), lambda b, t: (b, t, 3)),
                  pl.BlockSpec((1, rows, TAIL_W), lambda b, t: (b, t, 0)),
                  full((TAIL_W, GLA_QK)), full((1, GLA_QK)), full((1, GLA_DV)), st_spec],
        out_specs=[pl.BlockSpec((1, rows, GLA_W), lambda b, t: (b, t, 0)), st_spec],
        out_shape=[_sds((nseq, nsteps * rows, GLA_W), F32 if precise else BF16),
                   _sds((nseq, GLA_HEADS, GLA_DV, GLA_DK), F32)],
        scratch_shapes=[pltpu.VMEM((GLA_HEADS, GLA_DV, GLA_DK), F32), pltpu.VMEM((rows, GLA_QK), F32),
                        pltpu.VMEM((rows, GLA_QK), F32), pltpu.VMEM((rows, GLA_QK), F32),
                        pltpu.VMEM((rows, GLA_W), F32), pltpu.VMEM((rows, GLA_W), F32)],
        compiler_params=_params("parallel", "arbitrary"), name="gla_mixer",
    )(rest, rest, rest, tail, wa_pad, b_a.reshape(1, GLA_QK), gain.reshape(1, GLA_DV), s0_t)
    return o, st


def _mem_attn_kernel(q_ref, k_ref, v_ref, o_ref, *, precise):
    for h in range(MEM_HEADS):
        sl = slice(h * MEM_HD, (h + 1) * MEM_HD)
        s = _mxu(q_ref[0, :, sl], k_ref[0, :, sl], precise, _NT)
        p = jnp.exp(s - jnp.max(s, axis=-1, keepdims=True))
        o = _mxu(p, v_ref[0, :, sl], precise)
        o_ref[0, :, sl] = (o / jnp.sum(p, axis=-1, keepdims=True)).astype(o_ref.dtype)


def memory_attention(q, mk, mv, tq, precise=False):
    b, t, _ = q.shape
    return pl.pallas_call(
        functools.partial(_mem_attn_kernel, precise=precise), grid=(b, t // tq),
        in_specs=[pl.BlockSpec((1, tq, MEM_W), lambda bi, i: (bi, i, 0)),
                  pl.BlockSpec((1, MEM_TOKENS, MEM_W), lambda bi, i: (bi, 0, 0)),
                  pl.BlockSpec((1, MEM_TOKENS, MEM_W), lambda bi, i: (bi, 0, 0))],
        out_specs=pl.BlockSpec((1, tq, MEM_W), lambda bi, i: (bi, i, 0)),
        out_shape=_sds((b, t, MEM_W), F32 if precise else BF16), compiler_params=_params("parallel", "parallel"),
        name="memory_attention",
    )(q, mk, mv)


def _router_gate(logits):
    lane = lax.broadcasted_iota(jnp.int32, logits.shape, 1)
    is_group = (lane >= N_EXPERTS) & (lane < N_EXPERTS + N_GROUPS)
    lg = jnp.where(is_group, logits, NEG)
    eg = jnp.where(is_group, jnp.exp(lg - jnp.max(lg, axis=-1, keepdims=True)), 0.0)
    pg = eg / jnp.sum(eg, axis=-1, keepdims=True)
    g_w = jnp.max(pg, axis=-1, keepdims=True)
    lane_f = lane.astype(F32)
    far = float(4 * LANES)
    g_lane = jnp.min(jnp.where(is_group & (pg == g_w), lane_f, far), axis=-1, keepdims=True)
    group_of_lane = (lane >> (EXP_PER_GROUP.bit_length() - 1)).astype(F32)
    sel = (lane < N_EXPERTS) & (group_of_lane == g_lane - float(N_EXPERTS))
    le = jnp.where(sel, logits, NEG)
    ee = jnp.where(sel, jnp.exp(le - jnp.max(le, axis=-1, keepdims=True)), 0.0)
    pe = ee / jnp.sum(ee, axis=-1, keepdims=True)
    m1 = jnp.max(jnp.where(sel, pe, -1.0), axis=-1, keepdims=True)
    i1 = jnp.min(jnp.where(sel & (pe == m1), lane_f, far), axis=-1, keepdims=True)
    rest = sel & (lane_f != i1)
    m2 = jnp.max(jnp.where(rest, pe, -1.0), axis=-1, keepdims=True)
    i2 = jnp.min(jnp.where(rest & (pe == m2), lane_f, far), axis=-1, keepdims=True)
    tot = m1 + m2
    w = jnp.where(lane_f == i1, m1 / tot, jnp.where(lane_f == i2, m2 / tot, 0.0))
    return w * g_w


def _router_kernel(x_ref, g_ref, w_ref, b_ref, gate_ref):
    x = x_ref[...]
    xn = x * lax.rsqrt(jnp.mean(x * x, axis=-1, keepdims=True) + EPS) * g_ref[...]
    gate_ref[...] = _router_gate(_mxu(xn, w_ref[...], True) + b_ref[...])


def moe_router(x, g, w_r, b_r, tm):
    n, d = x.shape
    return pl.pallas_call(
        _router_kernel, grid=(n // tm,),
        in_specs=[pl.BlockSpec((tm, d), lambda i: (i, 0)), pl.BlockSpec((1, d), lambda i: (0, 0)),
                  pl.BlockSpec((d, LANES), lambda i: (0, 0)), pl.BlockSpec((1, LANES), lambda i: (0, 0))],
        out_specs=pl.BlockSpec((tm, LANES), lambda i: (i, 0)),
        out_shape=_sds((n, LANES), F32), compiler_params=_params("parallel"), name="moe_router",
    )(x, g.reshape(1, d), w_r, b_r)


def _expert_kernel(x_ref, gate_ref, wg_ref, wu_ref, wd_ref, res_ref, *rest, precise, next_norm):
    if next_norm:
        g_ref, o_ref, h_ref, acc_s = rest
    else:
        o_ref, acc_s = rest
    e = pl.program_id(1)

    @pl.when(e == 0)
    def _():
        acc_s[...] = jnp.zeros(acc_s.shape, F32)

    x = x_ref[...]
    a = jax.nn.silu(_mxu(x, wg_ref[...], precise)) * _mxu(x, wu_ref[...], precise)
    gate = gate_ref[...]
    lane = lax.broadcasted_iota(jnp.int32, gate.shape, 1)
    g_col = jnp.sum(jnp.where(lane == e, gate, 0.0), axis=-1, keepdims=True)
    acc_s[...] += _mxu(a * g_col, wd_ref[...], precise)

    @pl.when(e == pl.num_programs(1) - 1)
    def _():
        x_new = res_ref[...] + acc_s[...]
        o_ref[...] = x_new
        if next_norm:
            y = x_new * lax.rsqrt(jnp.mean(x_new * x_new, axis=-1, keepdims=True) + EPS)
            h_ref[...] = (y * g_ref[...]).astype(h_ref.dtype)


def moe_experts(x, gate, w_gate, w_up, w_down, layer, res, tm, precise=False, g_next=None):
    n, d = x.shape
    row = pl.BlockSpec((tm, d), lambda i, e: (i, 0))
    in_specs = [row, pl.BlockSpec((tm, LANES), lambda i, e: (i, 0)),
                pl.BlockSpec((None, None, d, D_EXPERT), lambda i, e: (layer, e, 0, 0)),
                pl.BlockSpec((None, None, d, D_EXPERT), lambda i, e: (layer, e, 0, 0)),
                pl.BlockSpec((None, None, D_EXPERT, d), lambda i, e: (layer, e, 0, 0)), row]
    args = [x, gate, w_gate, w_up, w_down, res]
    out_specs, out_shape = row, _sds((n, d), F32)
    if g_next is not None:
        in_specs.append(pl.BlockSpec((1, d), lambda i, e: (0, 0)))
        args.append(g_next.reshape(1, d))
        out_specs, out_shape = [row, row], [_sds((n, d), F32), _sds((n, d), BF16)]
    return pl.pallas_call(
        functools.partial(_expert_kernel, precise=precise, next_norm=g_next is not None), grid=(n // tm, N_EXPERTS),
        in_specs=in_specs, out_specs=out_specs, out_shape=out_shape, scratch_shapes=[pltpu.VMEM((tm, d), F32)],
        compiler_params=_params("parallel", "arbitrary"), name="moe_experts",
    )(*args)


_BF16_STACKS = ("w_in", "w_out", "mem_w_q", "mem_w_k", "mem_w_v", "mem_w_o", "moe_w_gate", "moe_w_up", "moe_w_down")


def _layer_weights(l, p, pb):
    w_in = pb["w_in"][l]
    c = np.cumsum([0, FOX_W, FOX_W, FOX_W, FOX_HEADS, S5_W, GLA_QK, GLA_QK, GLA_W, GLA_RANK, GLA_W]).tolist()
    col = lambda i: w_in[:, c[i]:c[i + 1]]
    w = {}
    w["wrest"] = jnp.concatenate([col(4), col(5), col(6), col(7), col(9)], axis=1)
    w["wtail"] = jnp.concatenate(
        [col(3), col(8), jnp.zeros((D_MODEL, TAIL_W - FOX_HEADS - GLA_RANK), BF16)], axis=1)
    w["tail_bias"] = jnp.concatenate([p["fox_f_bias"][l], jnp.zeros((TAIL_W - FOX_HEADS,), F32)])
    w["q_gain"] = jnp.tile(p["fox_q_gain"][l], FOX_HEADS)
    w["k_gain"] = jnp.tile(p["fox_k_gain"][l], FOX_HEADS)
    lam_re, lam_im, bb_re, bb_im = s5_discretise(
        p["s5_a_re"][l], p["s5_a_im"][l], p["s5_log_dt"][l],
        jnp.swapaxes(p["s5_b_re"][l], 1, 2), jnp.swapaxes(p["s5_b_im"][l], 1, 2))
    gpb = S5_GROUPS // S5_BLOCKS
    w["s5_lam_re"], w["s5_lam_im"] = lam_re.reshape(1, S5_NS), lam_im.reshape(1, S5_NS)
    w["s5_wb_f32"] = jnp.concatenate([_block_diag(bb_re, gpb), _block_diag(bb_im, gpb)], axis=2)
    w["s5_wb"] = w["s5_wb_f32"].astype(BF16)
    c_re_t = jnp.swapaxes(p["s5_c_re"][l], 1, 2)
    c_im_t = jnp.swapaxes(p["s5_c_im"][l], 1, 2)
    w["s5_wc_f32"] = jnp.concatenate([_block_diag(c_re_t, gpb), -_block_diag(c_im_t, gpb)], axis=1)
    w["s5_wc"] = w["s5_wc_f32"].astype(BF16)
    w["s5_wglu"] = p["s5_w_glu"][l].astype(BF16)
    w["gla_wa_f32"] = jnp.concatenate(
        [jnp.zeros((FOX_HEADS, GLA_QK), F32), p["gla_w_a2"][l],
         jnp.zeros((TAIL_W - FOX_HEADS - GLA_RANK, GLA_QK), F32)], axis=0)
    w["gla_wa"] = w["gla_wa_f32"].astype(BF16)
    w["mem_q_gain"] = jnp.tile(p["mem_q_gain"][l], MEM_HEADS)
    w["mem_k_gain"] = jnp.tile(p["mem_k_gain"][l], MEM_HEADS)
    w["w_router"] = jnp.concatenate(
        [p["moe_w_expert"][l], p["moe_w_group"][l],
         jnp.zeros((D_MODEL, LANES - N_EXPERTS - N_GROUPS), F32)], axis=1)
    w["b_router"] = jnp.concatenate(
        [p["moe_b_expert"][l], p["moe_b_group"][l], jnp.zeros((LANES - N_EXPERTS - N_GROUPS,), F32)]).reshape(1, LANES)
    return w


def _in_projection(h, w, pb, l, tm):
    tn = 512
    (q_b,) = matmul(h, pb["w_in"], tm=tm, tn=tn, out_dtypes=(BF16,), gain=w["q_gain"],
                    post_scale=FOX_HD ** -0.5 * LOG2E, layer=l, cols=(0, FOX_W), name="proj_q")
    k_f, k_b = matmul(h, pb["w_in"], tm=tm, tn=tn, out_dtypes=(F32, BF16), gain=w["k_gain"], layer=l,
                      cols=(FOX_W, FOX_W), name="proj_k")
    v_f, v_b = matmul(h, pb["w_in"], tm=tm, tn=tn, out_dtypes=(F32, BF16), layer=l, cols=(2 * FOX_W, FOX_W),
                      name="proj_v")
    (rest,) = matmul(h, w["wrest"], tm=tm, tn=tn, name="proj_rest")
    (tail,) = matmul(h, w["wtail"], tm=tm, tn=TAIL_W, logsig_bias=w["tail_bias"], logsig_lanes=FOX_HEADS,
                     name="proj_tail")
    return q_b, k_f, k_b, v_f, v_b, rest, tail


def _post_mixer(x, branches, mk, mv, w, p, pb, l, b, t, tm, tq_mem, g_next):
    n = x.shape[0]
    tm_full = min(256, n)
    x, hq = proj_res_norm(branches, pb["w_out"], l, x, p["g_mem_pre"][l], tm_full, name="proj_out")
    (q,) = matmul(hq, pb["mem_w_q"], tm=tm, tn=512, out_dtypes=(BF16,), gain=w["mem_q_gain"],
                  post_scale=MEM_HD ** -0.5, layer=l, name="mem_q")
    o = memory_attention(q.reshape(b, t, MEM_W), mk, mv, tq_mem).reshape(n, MEM_W)
    x, hm = proj_res_norm([o], pb["mem_w_o"], l, x, p["g_moe"][l], tm_full, name="mem_o")
    gate = moe_router(x, p["g_moe"][l], w["w_router"], w["b_router"], tm)
    out = moe_experts(hm, gate, pb["moe_w_gate"], pb["moe_w_up"], pb["moe_w_down"], l, x, tm, g_next=g_next)
    return out if g_next is not None else (out, None)


def kernel(x_prompt, x_sample, cache_fox_k, cache_fox_v, cache_fox_logf, state_s5_re, state_s5_im, state_gla,
           cache_mem_k, cache_mem_v, page_table, mem_prompt, g_mix, w_in, fox_q_gain, fox_k_gain, fox_f_bias,
           s5_a_re, s5_a_im, s5_log_dt, s5_b_re, s5_b_im, s5_c_re, s5_c_im, s5_d, s5_w_glu, s5_b_glu,
           gla_w_a2, gla_b_a, gla_gain, g_fox_out, g_s5_out, w_out, g_mem_pre, g_mem_tok, mem_w_q, mem_w_k,
           mem_w_v, mem_q_gain, mem_k_gain, mem_w_o, g_moe, moe_w_group, moe_b_group, moe_w_expert,
           moe_b_expert, moe_w_gate, moe_w_up, moe_w_down):
    p = dict(g_mix=g_mix, w_in=w_in, fox_q_gain=fox_q_gain, fox_k_gain=fox_k_gain, fox_f_bias=fox_f_bias,
             s5_a_re=s5_a_re, s5_a_im=s5_a_im, s5_log_dt=s5_log_dt, s5_b_re=s5_b_re, s5_b_im=s5_b_im,
             s5_c_re=s5_c_re, s5_c_im=s5_c_im, s5_d=s5_d, s5_w_glu=s5_w_glu, s5_b_glu=s5_b_glu,
             gla_w_a2=gla_w_a2, gla_b_a=gla_b_a, gla_gain=gla_gain, g_fox_out=g_fox_out, g_s5_out=g_s5_out,
             w_out=w_out, g_mem_pre=g_mem_pre, g_mem_tok=g_mem_tok, mem_w_q=mem_w_q, mem_w_k=mem_w_k,
             mem_w_v=mem_w_v, mem_q_gain=mem_q_gain, mem_k_gain=mem_k_gain, mem_w_o=mem_w_o, g_moe=g_moe,
             moe_w_group=moe_w_group, moe_b_group=moe_b_group, moe_w_expert=moe_w_expert,
             moe_b_expert=moe_b_expert, moe_w_gate=moe_w_gate, moe_w_up=moe_w_up, moe_w_down=moe_w_down)
    depth = w_in.shape[0]
    b, t, d = x_prompt.shape
    db = x_sample.shape[0]
    n = b * t
    n_mem = b * MEM_TOKENS
    pb = {name: p[name].astype(BF16) for name in _BF16_STACKS}
    tm = min(512, n)
    tq_fox = min(512, t)
    s5_rows = min(256, t)
    gla_rows = min(256, t)
    tq_mem = min(512, t)

    xp = x_prompt.reshape(n, d)
    xs = jnp.pad(x_sample.reshape(db, d), ((0, SAMPLE_ROWS - db), (0, 0)))
    mem2d = mem_prompt.reshape(n_mem, d)
    cache_ft = jnp.swapaxes(cache_fox_logf, 2, 3)
    zero_s5 = jnp.zeros((b, 1, S5_NS), F32)
    zero_gla = jnp.zeros((b, GLA_HEADS, GLA_DV, GLA_DK), F32)
    outs = {k: [] for k in ("pk", "pv", "pf", "sk", "sv", "sf", "p5r", "p5i", "s5r", "s5i", "pg", "sg", "pmk", "pmv")}

    h_prompt = rmsnorm_cast(xp, g_mix[0], tm)
    for l in range(depth):
        w = _layer_weights(l, p, pb)

        q_b, k_f, k_b, v_f, v_b, rest, tail = _in_projection(h_prompt, w, pb, l, tm)
        crow, ccol = fox_gate_cumsum(tail, b, t)
        fox_o = fox_prompt_attention(q_b, k_b, v_b, ccol, crow, g_fox_out[l], b, t, tq_fox)
        rest3, tail3 = rest.reshape(b, t, 4 * S5_W), tail.reshape(b, t, TAIL_W)
        s5_o, hre, him = s5_mixer(rest3, zero_s5, zero_s5, w["s5_lam_re"], w["s5_lam_im"], w["s5_wb"], w["s5_wc"],
                                  s5_d[l], w["s5_wglu"], s5_b_glu[l], g_s5_out[l],
                                  nseq=b, rows=s5_rows, nsteps=t // s5_rows, scan=True, col_block=0)
        gla_o, st = gla_mixer(rest3, tail3, w["gla_wa"], gla_b_a[l], gla_gain[l], zero_gla,
                              nseq=b, rows=gla_rows, nsteps=t // gla_rows, t_valid=gla_rows)
        branches = [fox_o, s5_o.reshape(n, S5_W), gla_o.reshape(n, GLA_W)]
        outs["pk"].append(k_f.reshape(b, t, FOX_HEADS, FOX_HD))
        outs["pv"].append(v_f.reshape(b, t, FOX_HEADS, FOX_HD))
        outs["pf"].append(tail3[:, :, :FOX_HEADS])
        outs["p5r"].append(hre.reshape(b, S5_GROUPS, S5_STATE))
        outs["p5i"].append(him.reshape(b, S5_GROUPS, S5_STATE))
        outs["pg"].append(jnp.swapaxes(st, 2, 3))
        hmem = rmsnorm_cast(mem2d, g_mem_tok[l], tm=min(512, n_mem))
        (mk,) = matmul(hmem, pb["mem_w_k"], tm=min(512, n_mem), tn=512, gain=w["mem_k_gain"], layer=l, name="mem_k")
        (mv,) = matmul(hmem, pb["mem_w_v"], tm=min(512, n_mem), tn=512, layer=l, name="mem_v")
        outs["pmk"].append(mk.reshape(b, MEM_TOKENS, MEM_HEADS, MEM_HD))
        outs["pmv"].append(mv.reshape(b, MEM_TOKENS, MEM_HEADS, MEM_HD))
        xp, h_prompt = _post_mixer(xp, branches, mk.reshape(b, MEM_TOKENS, MEM_W), mv.reshape(b, MEM_TOKENS, MEM_W),
                                   w, p, pb, l, b, t, tm, tq_mem, g_mix[l + 1] if l + 1 < depth else None)

        ns = SAMPLE_ROWS
        h = rmsnorm_cast(xs, g_mix[l], ns, out_dtype=F32)
        c0 = np.cumsum([0, FOX_W, FOX_W, FOX_W, FOX_HEADS, S5_W, GLA_QK, GLA_QK, GLA_W, GLA_RANK, GLA_W]).tolist()
        in_cols = c0[-1]
        qk_gain = jnp.concatenate([w["q_gain"], w["k_gain"]]).reshape(1, 2 * FOX_W)
        f_bias = jnp.concatenate([fox_f_bias[l], jnp.zeros((in_cols - 3 * FOX_W - FOX_HEADS,), F32)]).reshape(1, -1)
        proj = pmatmul(h, w_in, l, tk=256, epi="inproj", aux=(qk_gain, f_bias), name="proj_in_sample")
        col = lambda i: proj[:, c0[i]:c0[i + 1]]
        hd3 = lambda a: a[:db].reshape(db, FOX_HEADS, FOX_HD)
        q_f, k_f, v_f, logf_new = col(0), col(1), col(2), col(3)[:db]
        rest = jnp.concatenate([col(4), col(5), col(6), col(7), col(9)], axis=1)
        tail = jnp.concatenate([col(3), col(8), jnp.zeros((ns, TAIL_W - FOX_HEADS - GLA_RANK), F32)], axis=1)
        fox_o = fox_sample_attention(l, hd3(q_f), hd3(k_f), hd3(v_f), logf_new.reshape(db, FOX_HEADS, 1),
                                     cache_fox_k, cache_fox_v, cache_ft, page_table, g_fox_out[l],
                                     pages_per_step=DECODE_PAGES_PER_STEP)
        fox_o = jnp.pad(fox_o.reshape(db, FOX_W), ((0, ns - db), (0, 0)))
        pad_state = lambda a: jnp.pad(a.reshape(1, db, S5_NS), ((0, 0), (0, ns - db), (0, 0)))
        s5_o, hre, him = s5_mixer(rest.reshape(1, ns, 4 * S5_W), pad_state(state_s5_re[l]), pad_state(state_s5_im[l]),
                                  w["s5_lam_re"], w["s5_lam_im"], w["s5_wb_f32"], w["s5_wc_f32"], s5_d[l],
                                  s5_w_glu[l], s5_b_glu[l], g_s5_out[l], nseq=1, rows=ns, nsteps=1, scan=False,
                                  col_block=0, precise=True)
        seq_pad = lambda a: jnp.pad(a[:db].reshape(db, 1, a.shape[1]), ((0, 0), (0, GLA_CHUNK - 1), (0, 0)))
        gla_o, st = gla_mixer(seq_pad(rest), seq_pad(tail), w["gla_wa_f32"], gla_b_a[l], gla_gain[l],
                              jnp.swapaxes(state_gla[l], 2, 3), nseq=db, rows=GLA_CHUNK, nsteps=1, t_valid=1,
                              precise=True)
        gla_o = jnp.pad(gla_o[:, 0, :], ((0, ns - db), (0, 0)))
        cat = jnp.concatenate([fox_o, s5_o.reshape(ns, S5_W), gla_o], axis=1)
        outs["sk"].append(hd3(k_f).reshape(db, 1, FOX_HEADS, FOX_HD))
        outs["sv"].append(hd3(v_f).reshape(db, 1, FOX_HEADS, FOX_HD))
        outs["sf"].append(logf_new.reshape(db, 1, FOX_HEADS))
        outs["s5r"].append(hre[0, :db].reshape(db, S5_GROUPS, S5_STATE))
        outs["s5i"].append(him[0, :db].reshape(db, S5_GROUPS, S5_STATE))
        outs["sg"].append(jnp.swapaxes(st, 2, 3))
        xs = _post_mixer_sample(xs, cat, cache_mem_k[l].reshape(db, MEM_TOKENS, MEM_W),
                                cache_mem_v[l].reshape(db, MEM_TOKENS, MEM_W), w, p, l, db)

    stk = lambda k: jnp.stack(outs[k])
    return (xp.reshape(b, t, d), xs[:db].reshape(db, 1, d),
            stk("pk"), stk("pv"), stk("pf"), stk("sk"), stk("sv"), stk("sf"),
            stk("p5r"), stk("p5i"), stk("s5r"), stk("s5i"), stk("pg"), stk("sg"), stk("pmk"), stk("pmv"))


def _post_mixer_sample(x, cat, mk, mv, w, p, l, db):
    ns = x.shape[0]
    x = pmatmul(cat, p["w_out"], l, tk=512, epi="res", aux=(x,), name="proj_out_sample")
    hq = rmsnorm_cast(x, p["g_mem_pre"][l], ns, out_dtype=F32)
    q = pmatmul(hq, p["mem_w_q"], l, tk=512, epi="headnorm", aux=(w["mem_q_gain"].reshape(1, MEM_W),),
                post_scale=MEM_HD ** -0.5, name="mem_q_sample")
    q3 = jnp.pad(q[:db].reshape(db, 1, MEM_W), ((0, 0), (0, SAMPLE_ROWS - 1), (0, 0)))
    o = memory_attention(q3, mk, mv, SAMPLE_ROWS, precise=True)[:, 0, :]
    o = jnp.pad(o, ((0, ns - db), (0, 0)))
    x = pmatmul(o, p["mem_w_o"], l, tk=MEM_W, epi="res", aux=(x,), name="mem_o_sample")
    hm = rmsnorm_cast(x, p["g_moe"][l], ns, out_dtype=F32)
    gate = pmatmul(hm, w["w_router"][None], 0, tk=512, epi="router", aux=(w["b_router"],), name="router_sample")
    return moe_experts(hm, gate, p["moe_w_gate"], p["moe_w_up"], p["moe_w_down"], l, x, ns, precise=True)
```

```python
import functools
import math

import numpy as np
import jax
import jax.numpy as jnp
from jax import lax
from jax.experimental import pallas as pl
from jax.experimental.pallas import tpu as pltpu

F32, BF16 = jnp.float32, jnp.bfloat16

D_MODEL = 2048
FOX_HEADS, FOX_HD = 8, 128
FOX_W = FOX_HEADS * FOX_HD
S5_W, S5_GROUP, S5_GROUPS, S5_STATE = 512, 16, 32, 64
S5_NS = S5_GROUPS * S5_STATE
S5_BLOCKS = 4
S5_BC = S5_W // S5_BLOCKS
S5_BS = S5_NS // S5_BLOCKS
S5_SCAN_BLOCK = 8
GLA_HEADS, GLA_DK, GLA_DV = 4, 64, 128
GLA_W = GLA_HEADS * GLA_DV
GLA_QK = GLA_HEADS * GLA_DK
GLA_RANK = 16
GLA_TAU = 16.0
GLA_CHUNK = 16
MEM_TOKENS, MEM_HEADS, MEM_HD = 256, 4, 128
MEM_W = MEM_HEADS * MEM_HD
N_GROUPS, EXP_PER_GROUP, N_EXPERTS, D_EXPERT = 4, 4, 16, 256
PAGE_SIZE = 128
EPS = 1e-6
NEG = -1e30
LOG2E = math.log2(math.e)

LANES = 128
VMEM_LIMIT_BYTES = 48 * 1024 * 1024

DECODE_GROUP = 16
DECODE_PAGES_PER_STEP = 8
TAIL_W = LANES
SAMPLE_ROWS = 16


def _params(*sem):
    return pltpu.CompilerParams(dimension_semantics=sem, vmem_limit_bytes=VMEM_LIMIT_BYTES)


def _sds(shape, dtype):
    return jax.ShapeDtypeStruct(shape, dtype)


_NN = (((1,), (0,)), ((), ()))
_NT = (((1,), (1,)), ((), ()))
_TN = (((0,), (0,)), ((), ()))


def _mxu(a, b, precise, dims=_NN):
    if precise:
        return lax.dot_general(a.astype(F32), b.astype(F32), dims, preferred_element_type=F32,
                               precision=lax.Precision.HIGHEST)
    return lax.dot_general(a.astype(BF16), b.astype(BF16), dims, preferred_element_type=F32)


def _rms_kernel(x_ref, g_ref, o_ref):
    x = x_ref[...]
    y = x * lax.rsqrt(jnp.mean(x * x, axis=-1, keepdims=True) + EPS)
    o_ref[...] = (y * g_ref[...]).astype(o_ref.dtype)


def rmsnorm_cast(x, g, tm, out_dtype=BF16):
    n, d = x.shape
    return pl.pallas_call(
        _rms_kernel, grid=(n // tm,),
        in_specs=[pl.BlockSpec((tm, d), lambda i: (i, 0)), pl.BlockSpec((1, d), lambda i: (0, 0))],
        out_specs=pl.BlockSpec((tm, d), lambda i: (i, 0)),
        out_shape=_sds((n, d), out_dtype), compiler_params=_params("parallel"), name="rmsnorm_cast",
    )(x, g.reshape(1, d))


def _mm_kernel(*refs, headnorm, post_scale, logsig_lanes, has_res, n_out):
    x_ref, w_ref = refs[0], refs[1]
    pos = 2
    gain_ref = bias_ref = res_ref = None
    if headnorm:
        gain_ref, pos = refs[pos], pos + 1
    if logsig_lanes:
        bias_ref, pos = refs[pos], pos + 1
    if has_res:
        res_ref, pos = refs[pos], pos + 1
    outs = refs[pos:pos + n_out]
    acc = jnp.dot(x_ref[...], w_ref[...], preferred_element_type=F32)
    if logsig_lanes:
        lane = lax.broadcasted_iota(jnp.int32, acc.shape, 1)
        acc = jnp.where(lane < logsig_lanes, jax.nn.log_sigmoid(acc + bias_ref[...]), acc)
    if has_res:
        acc = res_ref[...] + acc
    if headnorm:
        for c in range(acc.shape[1] // LANES):
            sl = slice(c * LANES, (c + 1) * LANES)
            blk = acc[:, sl]
            y = blk * lax.rsqrt(jnp.mean(blk * blk, axis=-1, keepdims=True) + EPS)
            y = y * gain_ref[:, sl]
            if post_scale != 1.0:
                y = y * post_scale
            for o in outs:
                o[:, sl] = y.astype(o.dtype)
    else:
        for o in outs:
            o[...] = acc.astype(o.dtype)


def matmul(x, w, *, tm, tn, out_dtypes=(F32,), gain=None, post_scale=1.0, logsig_bias=None,
           logsig_lanes=0, res=None, layer=None, cols=None, name="matmul"):
    n, k = x.shape
    col0, c = cols if cols is not None else (0, w.shape[-1])
    jb = col0 // tn
    args = [x, w]
    if layer is None:
        w_spec = pl.BlockSpec((k, tn), lambda i, j: (0, jb + j))
    else:
        w_spec = pl.BlockSpec((None, k, tn), lambda i, j: (layer, 0, jb + j))
    in_specs = [pl.BlockSpec((tm, k), lambda i, j: (i, 0)), w_spec]
    if gain is not None:
        args.append(gain.reshape(1, c).astype(F32))
        in_specs.append(pl.BlockSpec((1, tn), lambda i, j: (0, j)))
    if logsig_lanes:
        args.append(logsig_bias.reshape(1, c).astype(F32))
        in_specs.append(pl.BlockSpec((1, tn), lambda i, j: (0, j)))
    if res is not None:
        args.append(res)
        in_specs.append(pl.BlockSpec((tm, tn), lambda i, j: (i, j)))
    outs = pl.pallas_call(
        functools.partial(_mm_kernel, headnorm=gain is not None, post_scale=post_scale,
                          logsig_lanes=logsig_lanes, has_res=res is not None, n_out=len(out_dtypes)),
        grid=(n // tm, c // tn), in_specs=in_specs,
        out_specs=[pl.BlockSpec((tm, tn), lambda i, j: (i, j)) for _ in out_dtypes],
        out_shape=[_sds((n, c), dt) for dt in out_dtypes],
        compiler_params=_params("parallel", "parallel"), name=name,
    )(*args)
    return outs


def _proj_norm_kernel(*refs, ks):
    nx = len(ks)
    x_refs, w_ref, res_ref, g_ref, o_ref, h_ref = refs[:nx], refs[nx], refs[nx + 1], refs[nx + 2], refs[nx + 3], refs[nx + 4]
    acc, off = None, 0
    for x_ref, k in zip(x_refs, ks):
        part = jnp.dot(x_ref[...], w_ref[off:off + k, :], preferred_element_type=F32)
        acc = part if acc is None else acc + part
        off += k
    x_new = res_ref[...] + acc
    o_ref[...] = x_new
    y = x_new * lax.rsqrt(jnp.mean(x_new * x_new, axis=-1, keepdims=True) + EPS)
    h_ref[...] = (y * g_ref[...]).astype(h_ref.dtype)


def proj_res_norm(xs, w, layer, res, g, tm, name):
    n, d = res.shape
    ks = tuple(x.shape[1] for x in xs)
    ktot = sum(ks)
    in_specs = [pl.BlockSpec((tm, k), lambda i: (i, 0)) for k in ks]
    in_specs += [pl.BlockSpec((None, ktot, d), lambda i: (layer, 0, 0)), pl.BlockSpec((tm, d), lambda i: (i, 0)),
                 pl.BlockSpec((1, d), lambda i: (0, 0))]
    return pl.pallas_call(
        functools.partial(_proj_norm_kernel, ks=ks), grid=(n // tm,), in_specs=in_specs,
        out_specs=[pl.BlockSpec((tm, d), lambda i: (i, 0)), pl.BlockSpec((tm, d), lambda i: (i, 0))],
        out_shape=[_sds((n, d), F32), _sds((n, d), BF16)], compiler_params=_params("parallel"), name=name,
    )(*xs, w, res, g.reshape(1, d))


def _head_rmsnorm(blk, gain):
    return blk * lax.rsqrt(jnp.mean(blk * blk, axis=-1, keepdims=True) + EPS) * gain


def _pmm_kernel(*refs, epi, n_aux, post_scale):
    x_ref, w_ref = refs[0], refs[1]
    aux = refs[2:2 + n_aux]
    o_ref, acc_s = refs[2 + n_aux], refs[3 + n_aux]
    k = pl.program_id(0)

    @pl.when(k == 0)
    def _():
        acc_s[...] = jnp.zeros(acc_s.shape, F32)

    acc_s[...] += _mxu(x_ref[...], w_ref[...], True)

    @pl.when(k == pl.num_programs(0) - 1)
    def _():
        acc = acc_s[...]
        if epi == "none":
            o_ref[...] = acc
        elif epi == "res":
            o_ref[...] = aux[0][...] + acc
        elif epi == "headnorm":
            for c in range(acc.shape[1] // LANES):
                sl = slice(c * LANES, (c + 1) * LANES)
                o_ref[:, sl] = _head_rmsnorm(acc[:, sl], aux[0][:, sl]) * post_scale
        elif epi == "inproj":
            gain_ref, bias_ref = aux
            for c in range(2 * FOX_W // LANES):
                sl = slice(c * LANES, (c + 1) * LANES)
                o_ref[:, sl] = _head_rmsnorm(acc[:, sl], gain_ref[:, sl])
            o_ref[:, 2 * FOX_W:3 * FOX_W] = acc[:, 2 * FOX_W:3 * FOX_W]
            t = acc[:, 3 * FOX_W:]
            lane = lax.broadcasted_iota(jnp.int32, t.shape, 1)
            o_ref[:, 3 * FOX_W:] = jnp.where(lane < FOX_HEADS, jax.nn.log_sigmoid(t + bias_ref[...]), t)
        elif epi == "router":
            o_ref[...] = _router_gate(acc + aux[0][...])


def pmatmul(x, w, layer, *, tk, epi="none", aux=(), post_scale=1.0, name="pmatmul"):
    rows, kdim = x.shape
    c = w.shape[2]
    in_specs = [pl.BlockSpec((rows, tk), lambda k: (0, k)), pl.BlockSpec((None, tk, c), lambda k: (layer, k, 0))]
    in_specs += [pl.BlockSpec(a.shape, lambda k: (0, 0)) for a in aux]
    return pl.pallas_call(
        functools.partial(_pmm_kernel, epi=epi, n_aux=len(aux), post_scale=post_scale), grid=(kdim // tk,),
        in_specs=in_specs, out_specs=pl.BlockSpec((rows, c), lambda k: (0, 0)),
        out_shape=_sds((rows, c), F32), scratch_shapes=[pltpu.VMEM((rows, c), F32)],
        compiler_params=_params("arbitrary"), name=name,
    )(x, w, *aux)


def _fox_gate_kernel(t_ref, crow_ref, ccol_ref):
    x = t_ref[0]
    c = x.T[0:FOX_HEADS, :]
    t_len = c.shape[1]
    lane = lax.broadcasted_iota(jnp.int32, c.shape, 1)
    s = 1
    while s < t_len:
        c = c + jnp.where(lane >= s, pltpu.roll(c, s, axis=1), 0.0)
        s *= 2
    c = c * LOG2E
    crow_ref[0] = c
    cpad = jnp.concatenate([c, jnp.zeros((LANES - FOX_HEADS, t_len), F32)], axis=0)
    ccol_ref[0] = cpad.T


def fox_gate_cumsum(tail, b, t):
    return pl.pallas_call(
        _fox_gate_kernel, grid=(b,),
        in_specs=[pl.BlockSpec((1, t, TAIL_W), lambda i: (i, 0, 0))],
        out_specs=[pl.BlockSpec((1, FOX_HEADS, t), lambda i: (i, 0, 0)),
                   pl.BlockSpec((1, t, TAIL_W), lambda i: (i, 0, 0))],
        out_shape=[_sds((b, FOX_HEADS, t), F32), _sds((b, t, TAIL_W), F32)],
        compiler_params=_params("parallel"), name="fox_gate_cumsum",
    )(tail.reshape(b, t, TAIL_W))


def _lane_tile(x, n):
    return x if n == 1 else jnp.concatenate([x] * n, axis=1)


def _fox_attn_kernel(qi_tab, ki_tab, q_ref, k_ref, v_ref, cq_ref, ck_ref, g_ref, o_ref, m_s, l_s, acc_s, cq_s,
                     *, tq, tk):
    p_id = pl.program_id(1)
    qi, ki = qi_tab[p_id], ki_tab[p_id]
    nrep = tk // LANES

    @pl.when(ki == 0)
    def _():
        m_s[...] = jnp.full(m_s.shape, NEG, F32)
        l_s[...] = jnp.zeros(l_s.shape, F32)
        acc_s[...] = jnp.zeros(acc_s.shape, F32)
        for h in range(FOX_HEADS):
            cq_s[h] = jnp.broadcast_to(cq_ref[0, :, h:h + 1], (tq, LANES))

    def step(masked):
        if masked:
            row = lax.broadcasted_iota(jnp.int32, (tq, tk), 0)
            col = lax.broadcasted_iota(jnp.int32, (tq, tk), 1)
            causal = col <= row
        for h in range(FOX_HEADS):
            sl = slice(h * FOX_HD, (h + 1) * FOX_HD)
            s = lax.dot_general(q_ref[0, :, sl], k_ref[0, :, sl], _NT, preferred_element_type=F32)
            s = (s + _lane_tile(cq_s[h], nrep)) - ck_ref[0, h:h + 1, :]
            if masked:
                s = jnp.where(causal, s, NEG)
            m_old = m_s[h]
            m_new = jnp.maximum(m_old, jnp.broadcast_to(jnp.max(s, axis=-1, keepdims=True), (tq, LANES)))
            alpha = jnp.exp2(m_old - m_new)
            p = jnp.exp2(s - _lane_tile(m_new, nrep))
            l_s[h] = l_s[h] * alpha + jnp.broadcast_to(jnp.sum(p, axis=-1, keepdims=True), (tq, LANES))
            m_s[h] = m_new
            acc_s[:, sl] = acc_s[:, sl] * alpha + jnp.dot(p.astype(BF16), v_ref[0, :, sl],
                                                          preferred_element_type=F32)

    @pl.when(ki < qi)
    def _():
        step(False)

    @pl.when(ki == qi)
    def _():
        step(True)
        ssq = jnp.zeros((tq, 1), F32)
        for h in range(FOX_HEADS):
            sl = slice(h * FOX_HD, (h + 1) * FOX_HD)
            o = acc_s[:, sl] / l_s[h]
            acc_s[:, sl] = o
            ssq = ssq + jnp.sum(o * o, axis=-1, keepdims=True)
        inv = lax.rsqrt(ssq / FOX_W + EPS)
        o_ref[0] = (acc_s[...] * inv * g_ref[...]).astype(o_ref.dtype)


def fox_prompt_attention(q, k, v, ccol, crow, g_out, b, t, tq):
    nq = t // tq
    pairs = [(i, j) for i in range(nq) for j in range(i + 1)]
    qi_tab = jnp.asarray(np.array([p[0] for p in pairs], np.int32))
    ki_tab = jnp.asarray(np.array([p[1] for p in pairs], np.int32))
    q3, k3, v3 = (a.reshape(b, t, FOX_W) for a in (q, k, v))
    grid_spec = pltpu.PrefetchScalarGridSpec(
        num_scalar_prefetch=2, grid=(b, len(pairs)),
        in_specs=[
            pl.BlockSpec((1, tq, FOX_W), lambda bi, p, qt, kt: (bi, qt[p], 0)),
            pl.BlockSpec((1, tq, FOX_W), lambda bi, p, qt, kt: (bi, kt[p], 0)),
            pl.BlockSpec((1, tq, FOX_W), lambda bi, p, qt, kt: (bi, kt[p], 0)),
            pl.BlockSpec((1, tq, TAIL_W), lambda bi, p, qt, kt: (bi, qt[p], 0)),
            pl.BlockSpec((1, FOX_HEADS, tq), lambda bi, p, qt, kt: (bi, 0, kt[p])),
            pl.BlockSpec((1, FOX_W), lambda bi, p, qt, kt: (0, 0)),
        ],
        out_specs=pl.BlockSpec((1, tq, FOX_W), lambda bi, p, qt, kt: (bi, qt[p], 0)),
        scratch_shapes=[pltpu.VMEM((FOX_HEADS, tq, LANES), F32), pltpu.VMEM((FOX_HEADS, tq, LANES), F32),
                        pltpu.VMEM((tq, FOX_W), F32), pltpu.VMEM((FOX_HEADS, tq, LANES), F32)],
    )
    out = pl.pallas_call(
        functools.partial(_fox_attn_kernel, tq=tq, tk=tq), grid_spec=grid_spec,
        out_shape=_sds((b, t, FOX_W), BF16),
        compiler_params=_params("parallel", "arbitrary"), name="fox_prompt_attention",
    )(qi_tab, ki_tab, q3, k3, v3, ccol, crow, g_out.reshape(1, FOX_W))
    return out.reshape(b * t, FOX_W)


def _fox_decode_kernel(pt_ref, q_ref, kn_ref, vn_ref, fn_ref, g_ref, *refs, pages_per_step, scale):
    pp = pages_per_step
    k_refs, v_refs, f_refs = refs[0:pp], refs[pp:2 * pp], refs[2 * pp:3 * pp]
    o_ref = refs[3 * pp]
    m_s, l_s, acc_s, r_s = refs[3 * pp + 1:]
    step_id = pl.program_id(1)
    q = q_ref[0] * scale

    @pl.when(step_id == 0)
    def _():
        m_s[...] = jnp.sum(q * kn_ref[0], axis=-1, keepdims=True)
        l_s[...] = jnp.ones(l_s.shape, F32)
        acc_s[...] = vn_ref[0]
        r_s[...] = fn_ref[0]

    tok_r = lax.broadcasted_iota(jnp.int32, (PAGE_SIZE, PAGE_SIZE), 0)
    tok_c = lax.broadcasted_iota(jnp.int32, (PAGE_SIZE, PAGE_SIZE), 1)
    later = (tok_r > tok_c).astype(F32)
    r_run = r_s[...]
    parts = []
    grp = DECODE_GROUP
    tok = lax.broadcasted_iota(jnp.int32, (grp, 1, FOX_HD), 0)
    lane = lax.broadcasted_iota(jnp.int32, (grp, 1, FOX_HD), 2)
    for j in range(pp):
        f = f_refs[j][...]
        r = jnp.dot(f, later, preferred_element_type=F32, precision=lax.Precision.HIGHEST) + r_run
        r_run = r_run + jnp.sum(f, axis=-1, keepdims=True)
        for g in range(PAGE_SIZE // grp):
            rows = pl.ds(g * grp, grp)
            prod = k_refs[j][rows] * q[None] + jnp.where(lane == tok + g * grp, r[None], 0.0)
            s = jnp.sum(prod, axis=-1, keepdims=True)
            m_g = jnp.max(s, axis=0)
            p = jnp.exp(s - m_g[None])
            parts.append((m_g, jnp.sum(p, axis=0), jnp.sum(p * v_refs[j][rows], axis=0)))
    m_old = m_s[...]
    m_new = m_old
    for m_j, _, _ in parts:
        m_new = jnp.maximum(m_new, m_j)
    alpha = jnp.exp(m_old - m_new)
    l_run, acc = l_s[...] * alpha, acc_s[...] * alpha
    for m_j, l_j, pv_j in parts:
        a_j = jnp.exp(m_j - m_new)
        l_run, acc = l_run + l_j * a_j, acc + pv_j * a_j
    m_s[...], l_s[...], acc_s[...], r_s[...] = m_new, l_run, acc, r_run

    @pl.when(step_id == pl.num_programs(1) - 1)
    def _():
        o = acc_s[...] / l_s[...]
        ssq = jnp.sum(jnp.sum(o * o, axis=-1, keepdims=True), axis=0, keepdims=True)
        o_ref[0] = (o * lax.rsqrt(ssq / FOX_W + EPS) * g_ref[...]).astype(o_ref.dtype)


def fox_sample_attention(layer, q, k_new, v_new, logf_new, cache_k, cache_v, cache_ft, page_table, g_out,
                         pages_per_step):
    db, n_pages = page_table.shape
    n_steps = n_pages // pages_per_step

    def page_map(j):
        def im(bi, si, pt):
            return (layer, pt[bi, n_pages - 1 - (si * pages_per_step + j)], 0, 0, 0)
        return im

    def page_map_f(j):
        def im(bi, si, pt):
            return (layer, pt[bi, n_pages - 1 - (si * pages_per_step + j)], 0, 0)
        return im

    vec = pl.BlockSpec((1, FOX_HEADS, FOX_HD), lambda bi, si, pt: (bi, 0, 0))
    in_specs = [vec, vec, vec,
                pl.BlockSpec((1, FOX_HEADS, 1), lambda bi, si, pt: (bi, 0, 0)),
                pl.BlockSpec((FOX_HEADS, FOX_HD), lambda bi, si, pt: (0, 0))]
    in_specs += [pl.BlockSpec((None, None, PAGE_SIZE, FOX_HEADS, FOX_HD), page_map(j)) for j in range(pages_per_step)]
    in_specs += [pl.BlockSpec((None, None, PAGE_SIZE, FOX_HEADS, FOX_HD), page_map(j)) for j in range(pages_per_step)]
    in_specs += [pl.BlockSpec((None, None, FOX_HEADS, PAGE_SIZE), page_map_f(j)) for j in range(pages_per_step)]
    grid_spec = pltpu.PrefetchScalarGridSpec(
        num_scalar_prefetch=1, grid=(db, n_steps), in_specs=in_specs,
        out_specs=pl.BlockSpec((1, FOX_HEADS, FOX_HD), lambda bi, si, pt: (bi, 0, 0)),
        scratch_shapes=[pltpu.VMEM((FOX_HEADS, 1), F32), pltpu.VMEM((FOX_HEADS, 1), F32),
                        pltpu.VMEM((FOX_HEADS, FOX_HD), F32), pltpu.VMEM((FOX_HEADS, 1), F32)],
    )
    return pl.pallas_call(
        functools.partial(_fox_decode_kernel, pages_per_step=pages_per_step, scale=FOX_HD ** -0.5),
        grid_spec=grid_spec, out_shape=_sds((db, FOX_HEADS, FOX_HD), F32),
        compiler_params=_params("parallel", "arbitrary"), name="fox_sample_attention",
    )(page_table, q, k_new, v_new, logf_new, g_out.reshape(FOX_HEADS, FOX_HD),
      *([cache_k] * pages_per_step), *([cache_v] * pages_per_step), *([cache_ft] * pages_per_step))


def _s5_param_kernel(are_ref, aim_ref, ldt_ref, bre_ref, bim_ref, lre_ref, lim_ref, bbre_ref, bbim_ref):
    a_re, a_im = are_ref[...], aim_ref[...]
    dt = jnp.exp(ldt_ref[...])
    mag = jnp.exp(a_re * dt)
    lam_re, lam_im = mag * jnp.cos(a_im * dt), mag * jnp.sin(a_im * dt)
    den = a_re * a_re + a_im * a_im
    z_re = ((lam_re - 1.0) * a_re + lam_im * a_im) / den
    z_im = (lam_im * a_re - (lam_re - 1.0) * a_im) / den
    lre_ref[...] = lam_re
    lim_ref[...] = lam_im
    b_re, b_im = bre_ref[...], bim_ref[...]
    bbre_ref[...] = z_re[:, None, :] * b_re - z_im[:, None, :] * b_im
    bbim_ref[...] = z_re[:, None, :] * b_im + z_im[:, None, :] * b_re


def s5_discretise(a_re, a_im, log_dt, b_re_t, b_im_t):
    g, p = a_re.shape
    c = b_re_t.shape[1]
    return pl.pallas_call(
        _s5_param_kernel,
        out_shape=[_sds((g, p), F32), _sds((g, p), F32), _sds((g, c, p), F32), _sds((g, c, p), F32)],
        name="s5_discretise",
    )(a_re, a_im, log_dt.reshape(g, 1), b_re_t, b_im_t)


def _block_diag(w, groups_per_block):
    g, a, b = w.shape
    nb = g // groups_per_block
    eye = jnp.eye(groups_per_block, dtype=w.dtype)
    w5 = w.reshape(nb, groups_per_block, a, b)
    return jnp.einsum("jgab,gh->jgahb", w5, eye).reshape(nb, groups_per_block * a, groups_per_block * b)


def _row_tile(x, n):
    tile = jnp.broadcast_to(x, (S5_SCAN_BLOCK, x.shape[1]))
    return tile if n == S5_SCAN_BLOCK else jnp.concatenate([tile] * (n // S5_SCAN_BLOCK), axis=0)


def _s5_kernel(u_ref, h0re_ref, h0im_ref, lre_ref, lim_ref, wb_ref, wc_ref, d_ref, wglu_ref, bglu_ref, gout_ref,
               y_ref, hre_out, him_out, hre_s, him_s, cre_s, cim_s, y_s, *, rows, scan, precise):
    t_id = pl.program_id(1)
    lam_re, lam_im = lre_ref[...], lim_ref[...]

    @pl.when(t_id == 0)
    def _():
        cre_s[...] = h0re_ref[0]
        cim_s[...] = h0im_ref[0]

    u = u_ref[0]
    for j in range(S5_BLOCKS):
        bu = _mxu(u[:, j * S5_BC:(j + 1) * S5_BC], wb_ref[j], precise)
        hre_s[:, j * S5_BS:(j + 1) * S5_BS] = bu[:, :S5_BS]
        him_s[:, j * S5_BS:(j + 1) * S5_BS] = bu[:, S5_BS:]
    c_re, c_im = cre_s[...], cim_s[...]
    if scan:
        hre_s[0:1, :] = hre_s[0:1, :] + (lam_re * c_re - lam_im * c_im)
        him_s[0:1, :] = him_s[0:1, :] + (lam_re * c_im + lam_im * c_re)
        seg = S5_SCAN_BLOCK
        steps = rows // seg
        l_re, l_im = _row_tile(lam_re, seg), _row_tile(lam_im, seg)
        p_re, p_im = hre_s[0:seg, :], him_s[0:seg, :]
        for v in range(1, steps):
            sl = slice(v * seg, (v + 1) * seg)
            p_re, p_im = (hre_s[sl, :] + (l_re * p_re - l_im * p_im), him_s[sl, :] + (l_re * p_im + l_im * p_re))
            hre_s[sl, :] = p_re
            him_s[sl, :] = p_im
        f_re, f_im = lam_re, lam_im
        n = 1
        while n < steps:
            f_re, f_im = f_re * f_re - f_im * f_im, 2.0 * f_re * f_im
            n *= 2
        f_re, f_im = _row_tile(f_re, seg), _row_tile(f_im, seg)
        pos = lax.broadcasted_iota(jnp.int32, (seg, S5_NS), 0)
        e_re, e_im = p_re, p_im
        s = 1
        while s < seg:
            m_re, m_im = jnp.where(pos >= s, f_re, 0.0), jnp.where(pos >= s, f_im, 0.0)
            b_re, b_im = pltpu.roll(e_re, s, axis=0), pltpu.roll(e_im, s, axis=0)
            e_re, e_im = e_re + (m_re * b_re - m_im * b_im), e_im + (m_re * b_im + m_im * b_re)
            f_re, f_im = f_re * f_re - f_im * f_im, 2.0 * f_re * f_im
            s *= 2
        c_re = jnp.where(pos >= 1, pltpu.roll(e_re, 1, axis=0), 0.0)
        c_im = jnp.where(pos >= 1, pltpu.roll(e_im, 1, axis=0), 0.0)
        for v in range(steps):
            c_re, c_im = l_re * c_re - l_im * c_im, l_re * c_im + l_im * c_re
            sl = slice(v * seg, (v + 1) * seg)
            hre_s[sl, :] = hre_s[sl, :] + c_re
            him_s[sl, :] = him_s[sl, :] + c_im
        cre_s[...] = hre_s[rows - 1:rows, :]
        cim_s[...] = him_s[rows - 1:rows, :]
        hre_out[0] = hre_s[rows - 1:rows, :]
        him_out[0] = him_s[rows - 1:rows, :]
    else:
        hre_s[...] = hre_s[...] + (lam_re * c_re - lam_im * c_im)
        him_s[...] = him_s[...] + (lam_re * c_im + lam_im * c_re)
        hre_out[0] = hre_s[...]
        him_out[0] = him_s[...]

    for j in range(S5_BLOCKS):
        sl = slice(j * S5_BS, (j + 1) * S5_BS)
        yj = _mxu(hre_s[:, sl], wc_ref[j, 0:S5_BS, :], precise)
        yj = yj + _mxu(him_s[:, sl], wc_ref[j, S5_BS:2 * S5_BS, :], precise)
        y_s[:, j * S5_BC:(j + 1) * S5_BC] = yj
    y = y_s[...] + d_ref[...] * u
    y = jax.nn.gelu(y)
    z = _mxu(y, wglu_ref[...], precise) + bglu_ref[...]
    y = y * jax.nn.sigmoid(z)
    y = y * lax.rsqrt(jnp.mean(y * y, axis=-1, keepdims=True) + EPS) * gout_ref[...]
    y_ref[0] = y.astype(y_ref.dtype)


def s5_mixer(u, h0_re, h0_im, lam_re, lam_im, wb, wc, d_skip, w_glu, b_glu, g_out, *, nseq, rows, nsteps, scan,
             col_block, precise=False):
    st_rows = 1 if scan else rows
    full = lambda shape: pl.BlockSpec(shape, lambda b, t: tuple(0 for _ in shape))
    state_spec = pl.BlockSpec((1, st_rows, S5_NS), lambda b, t: (b, 0, 0))
    y, hre, him = pl.pallas_call(
        functools.partial(_s5_kernel, rows=rows, scan=scan, precise=precise), grid=(nseq, nsteps),
        in_specs=[pl.BlockSpec((1, rows, S5_W), lambda b, t: (b, t, col_block)),
                  state_spec, state_spec, full((1, S5_NS)), full((1, S5_NS)),
                  full((S5_BLOCKS, S5_BC, 2 * S5_BS)), full((S5_BLOCKS, 2 * S5_BS, S5_BC)),
                  full((1, S5_W)), full((S5_W, S5_W)), full((1, S5_W)), full((1, S5_W))],
        out_specs=[pl.BlockSpec((1, rows, S5_W), lambda b, t: (b, t, 0)), state_spec, state_spec],
        out_shape=[_sds((nseq, nsteps * rows, S5_W), F32 if precise else BF16),
                   _sds((nseq, st_rows, S5_NS), F32), _sds((nseq, st_rows, S5_NS), F32)],
        scratch_shapes=[pltpu.VMEM((rows, S5_NS), F32), pltpu.VMEM((rows, S5_NS), F32),
                        pltpu.VMEM((st_rows, S5_NS), F32), pltpu.VMEM((st_rows, S5_NS), F32),
                        pltpu.VMEM((rows, S5_W), F32)],
        compiler_params=_params("parallel", "arbitrary"), name="s5_mixer",
    )(u, h0_re, h0_im, lam_re, lam_im, wb, wc, d_skip.reshape(1, S5_W), w_glu, b_glu.reshape(1, S5_W),
      g_out.reshape(1, S5_W))
    return y, hre, him


def _gla_kernel(qk_ref, v_ref, go_ref, ga_ref, wa_ref, ba_ref, gain_ref, s0_ref, o_ref, st_out,
                st_s, o_s, *, rows, t_valid, precise):
    t_id = pl.program_id(1)

    @pl.when(t_id == 0)
    def _():
        st_s[...] = s0_ref[0]

    x = _mxu(ga_ref[0], wa_ref[...], precise) + ba_ref[...]
    la = jax.nn.log_sigmoid(x) / GLA_TAU
    row = lax.broadcasted_iota(jnp.int32, la.shape, 0)
    if t_valid < rows:
        la = jnp.where(row < t_valid, la, 0.0)
    pos = row & (GLA_CHUNK - 1)
    b, e = la, la
    s = 1
    while s < GLA_CHUNK:
        b = b + jnp.where(pos >= s, pltpu.roll(b, s, axis=0), 0.0)
        e = e + jnp.where(pos < GLA_CHUNK - s, pltpu.roll(e, rows - s, axis=0), 0.0)
        s *= 2
    b_last = b + e - la
    qk = qk_ref[0]
    q, k = qk[:, 0:GLA_QK] * (GLA_DK ** -0.5), qk[:, GLA_QK:2 * GLA_QK]
    qe = q * jnp.exp(b)
    ke = k * jnp.exp(-b)
    kd = k * jnp.exp(b_last - b)
    dec = jnp.exp(b_last)
    v = v_ref[0]
    if not precise:
        qe, ke, kd, v = qe.astype(BF16), ke.astype(BF16), kd.astype(BF16), v.astype(BF16)
    r_i = lax.broadcasted_iota(jnp.int32, (rows, rows), 0)
    c_i = lax.broadcasted_iota(jnp.int32, (rows, rows), 1)
    chunk_shift = GLA_CHUNK.bit_length() - 1
    causal = ((r_i >> chunk_shift) == (c_i >> chunk_shift)) & (c_i <= r_i)
    for h in range(GLA_HEADS):
        ks = slice(h * GLA_DK, (h + 1) * GLA_DK)
        vs = slice(h * GLA_DV, (h + 1) * GLA_DV)
        att = _mxu(qe[:, ks], ke[:, ks], precise, _NT)
        att = jnp.where(causal, att, 0.0)
        o_s[:, vs] = _mxu(att, v[:, vs], precise)

    state = [st_s[h] for h in range(GLA_HEADS)]
    for j in range(rows // GLA_CHUNK):
        rs = slice(j * GLA_CHUNK, (j + 1) * GLA_CHUNK)
        for h in range(GLA_HEADS):
            ks = slice(h * GLA_DK, (h + 1) * GLA_DK)
            vs = slice(h * GLA_DV, (h + 1) * GLA_DV)
            o_s[rs, vs] = o_s[rs, vs] + _mxu(qe[rs, ks], state[h], precise, _NT)
            upd = _mxu(v[rs, vs], kd[rs, ks], precise, _TN)
            state[h] = state[h] * dec[j * GLA_CHUNK:j * GLA_CHUNK + 1, ks] + upd
    for h in range(GLA_HEADS):
        st_s[h] = state[h]
    st_out[0] = st_s[...]

    go = go_ref[0]
    for h in range(GLA_HEADS):
        vs = slice(h * GLA_DV, (h + 1) * GLA_DV)
        o = o_s[:, vs]
        y = o * lax.rsqrt(jnp.mean(o * o, axis=-1, keepdims=True) + EPS) * gain_ref[...]
        o_ref[0, :, vs] = (y * jax.nn.silu(go[:, vs])).astype(o_ref.dtype)


def gla_mixer(rest, tail, wa_pad, b_a, gain, s0_t, *, nseq, rows, nsteps, t_valid, precise=False):
    full = lambda shape: pl.BlockSpec(shape, lambda b, t: tuple(0 for _ in shape))
    st_spec = pl.BlockSpec((1, GLA_HEADS, GLA_DV, GLA_DK), lambda b, t: (b, 0, 0, 0))
    o, st = pl.pallas_call(
        functools.partial(_gla_kernel, rows=rows, t_valid=t_valid, precise=precise), grid=(nseq, nsteps),
        in_specs=[pl.BlockSpec((1, rows, 2 * GLA_QK), lambda b, t: (b, t, 1)),
                  pl.BlockSpec((1, rows, GLA_W), lambda b, t: (b, t, 2)),
                  pl.BlockSpec((1, rows, GLA_W), lambda b, t: (b, t, 3)),
                  pl.BlockSpec((1, rows, TAIL_W), lambda b, t: (b, t, 0)),
                  full((TAIL_W, GLA_QK)), full((1, GLA_QK)), full((1, GLA_DV)), st_spec],
        out_specs=[pl.BlockSpec((1, rows, GLA_W), lambda b, t: (b, t, 0)), st_spec],
        out_shape=[_sds((nseq, nsteps * rows, GLA_W), F32 if precise else BF16),
                   _sds((nseq, GLA_HEADS, GLA_DV, GLA_DK), F32)],
        scratch_shapes=[pltpu.VMEM((GLA_HEADS, GLA_DV, GLA_DK), F32), pltpu.VMEM((rows, GLA_W), F32)],
        compiler_params=_params("parallel", "arbitrary"), name="gla_mixer",
    )(rest, rest, rest, tail, wa_pad, b_a.reshape(1, GLA_QK), gain.reshape(1, GLA_DV), s0_t)
    return o, st


def _mem_attn_kernel(q_ref, k_ref, v_ref, o_ref, *, precise):
    for h in range(MEM_HEADS):
        sl = slice(h * MEM_HD, (h + 1) * MEM_HD)
        s = _mxu(q_ref[0, :, sl], k_ref[0, :, sl], precise, _NT)
        p = jnp.exp(s - jnp.max(s, axis=-1, keepdims=True))
        o = _mxu(p, v_ref[0, :, sl], precise)
        o_ref[0, :, sl] = (o / jnp.sum(p, axis=-1, keepdims=True)).astype(o_ref.dtype)


def memory_attention(q, mk, mv, tq, precise=False):
    b, t, _ = q.shape
    return pl.pallas_call(
        functools.partial(_mem_attn_kernel, precise=precise), grid=(b, t // tq),
        in_specs=[pl.BlockSpec((1, tq, MEM_W), lambda bi, i: (bi, i, 0)),
                  pl.BlockSpec((1, MEM_TOKENS, MEM_W), lambda bi, i: (bi, 0, 0)),
                  pl.BlockSpec((1, MEM_TOKENS, MEM_W), lambda bi, i: (bi, 0, 0))],
        out_specs=pl.BlockSpec((1, tq, MEM_W), lambda bi, i: (bi, i, 0)),
        out_shape=_sds((b, t, MEM_W), F32 if precise else BF16), compiler_params=_params("parallel", "parallel"),
        name="memory_attention",
    )(q, mk, mv)


def _router_gate(logits):
    lane = lax.broadcasted_iota(jnp.int32, logits.shape, 1)
    is_group = (lane >= N_EXPERTS) & (lane < N_EXPERTS + N_GROUPS)
    lg = jnp.where(is_group, logits, NEG)
    eg = jnp.where(is_group, jnp.exp(lg - jnp.max(lg, axis=-1, keepdims=True)), 0.0)
    pg = eg / jnp.sum(eg, axis=-1, keepdims=True)
    g_w = jnp.max(pg, axis=-1, keepdims=True)
    lane_f = lane.astype(F32)
    far = float(4 * LANES)
    g_lane = jnp.min(jnp.where(is_group & (pg == g_w), lane_f, far), axis=-1, keepdims=True)
    group_of_lane = (lane >> (EXP_PER_GROUP.bit_length() - 1)).astype(F32)
    sel = (lane < N_EXPERTS) & (group_of_lane == g_lane - float(N_EXPERTS))
    le = jnp.where(sel, logits, NEG)
    ee = jnp.where(sel, jnp.exp(le - jnp.max(le, axis=-1, keepdims=True)), 0.0)
    pe = ee / jnp.sum(ee, axis=-1, keepdims=True)
    m1 = jnp.max(jnp.where(sel, pe, -1.0), axis=-1, keepdims=True)
    i1 = jnp.min(jnp.where(sel & (pe == m1), lane_f, far), axis=-1, keepdims=True)
    rest = sel & (lane_f != i1)
    m2 = jnp.max(jnp.where(rest, pe, -1.0), axis=-1, keepdims=True)
    i2 = jnp.min(jnp.where(rest & (pe == m2), lane_f, far), axis=-1, keepdims=True)
    tot = m1 + m2
    w = jnp.where(lane_f == i1, m1 / tot, jnp.where(lane_f == i2, m2 / tot, 0.0))
    return w * g_w


def _router_kernel(x_ref, g_ref, w_ref, b_ref, gate_ref):
    x = x_ref[...]
    xn = x * lax.rsqrt(jnp.mean(x * x, axis=-1, keepdims=True) + EPS) * g_ref[...]
    gate_ref[...] = _router_gate(_mxu(xn, w_ref[...], True) + b_ref[...])


def moe_router(x, g, w_r, b_r, tm):
    n, d = x.shape
    return pl.pallas_call(
        _router_kernel, grid=(n // tm,),
        in_specs=[pl.BlockSpec((tm, d), lambda i: (i, 0)), pl.BlockSpec((1, d), lambda i: (0, 0)),
                  pl.BlockSpec((d, LANES), lambda i: (0, 0)), pl.BlockSpec((1, LANES), lambda i: (0, 0))],
        out_specs=pl.BlockSpec((tm, LANES), lambda i: (i, 0)),
        out_shape=_sds((n, LANES), F32), compiler_params=_params("parallel"), name="moe_router",
    )(x, g.reshape(1, d), w_r, b_r)


def _expert_kernel(x_ref, gate_ref, wg_ref, wu_ref, wd_ref, res_ref, *rest, precise, next_norm):
    if next_norm:
        g_ref, o_ref, h_ref, acc_s = rest
    else:
        o_ref, acc_s = rest
    e = pl.program_id(1)

    @pl.when(e == 0)
    def _():
        acc_s[...] = jnp.zeros(acc_s.shape, F32)

    x = x_ref[...]
    a = jax.nn.silu(_mxu(x, wg_ref[...], precise)) * _mxu(x, wu_ref[...], precise)
    gate = gate_ref[...]
    lane = lax.broadcasted_iota(jnp.int32, gate.shape, 1)
    g_col = jnp.sum(jnp.where(lane == e, gate, 0.0), axis=-1, keepdims=True)
    acc_s[...] += _mxu(a * g_col, wd_ref[...], precise)

    @pl.when(e == pl.num_programs(1) - 1)
    def _():
        x_new = res_ref[...] + acc_s[...]
        o_ref[...] = x_new
        if next_norm:
            y = x_new * lax.rsqrt(jnp.mean(x_new * x_new, axis=-1, keepdims=True) + EPS)
            h_ref[...] = (y * g_ref[...]).astype(h_ref.dtype)


def moe_experts(x, gate, w_gate, w_up, w_down, layer, res, tm, precise=False, g_next=None):
    n, d = x.shape
    row = pl.BlockSpec((tm, d), lambda i, e: (i, 0))
    in_specs = [row, pl.BlockSpec((tm, LANES), lambda i, e: (i, 0)),
                pl.BlockSpec((None, None, d, D_EXPERT), lambda i, e: (layer, e, 0, 0)),
                pl.BlockSpec((None, None, d, D_EXPERT), lambda i, e: (layer, e, 0, 0)),
                pl.BlockSpec((None, None, D_EXPERT, d), lambda i, e: (layer, e, 0, 0)), row]
    args = [x, gate, w_gate, w_up, w_down, res]
    out_specs, out_shape = row, _sds((n, d), F32)
    if g_next is not None:
        in_specs.append(pl.BlockSpec((1, d), lambda i, e: (0, 0)))
        args.append(g_next.reshape(1, d))
        out_specs, out_shape = [row, row], [_sds((n, d), F32), _sds((n, d), BF16)]
    return pl.pallas_call(
        functools.partial(_expert_kernel, precise=precise, next_norm=g_next is not None), grid=(n // tm, N_EXPERTS),
        in_specs=in_specs, out_specs=out_specs, out_shape=out_shape, scratch_shapes=[pltpu.VMEM((tm, d), F32)],
        compiler_params=_params("parallel", "arbitrary"), name="moe_experts",
    )(*args)


_BF16_STACKS = ("w_in", "w_out", "mem_w_q", "mem_w_k", "mem_w_v", "mem_w_o", "moe_w_gate", "moe_w_up", "moe_w_down")


def _layer_weights(l, p, pb):
    w_in = pb["w_in"][l]
    c = np.cumsum([0, FOX_W, FOX_W, FOX_W, FOX_HEADS, S5_W, GLA_QK, GLA_QK, GLA_W, GLA_RANK, GLA_W]).tolist()
    col = lambda i: w_in[:, c[i]:c[i + 1]]
    w = {}
    w["wrest"] = jnp.concatenate([col(4), col(5), col(6), col(7), col(9)], axis=1)
    w["wtail"] = jnp.concatenate(
        [col(3), col(8), jnp.zeros((D_MODEL, TAIL_W - FOX_HEADS - GLA_RANK), BF16)], axis=1)
    w["tail_bias"] = jnp.concatenate([p["fox_f_bias"][l], jnp.zeros((TAIL_W - FOX_HEADS,), F32)])
    w["q_gain"] = jnp.tile(p["fox_q_gain"][l], FOX_HEADS)
    w["k_gain"] = jnp.tile(p["fox_k_gain"][l], FOX_HEADS)
    lam_re, lam_im, bb_re, bb_im = s5_discretise(
        p["s5_a_re"][l], p["s5_a_im"][l], p["s5_log_dt"][l],
        jnp.swapaxes(p["s5_b_re"][l], 1, 2), jnp.swapaxes(p["s5_b_im"][l], 1, 2))
    gpb = S5_GROUPS // S5_BLOCKS
    w["s5_lam_re"], w["s5_lam_im"] = lam_re.reshape(1, S5_NS), lam_im.reshape(1, S5_NS)
    w["s5_wb_f32"] = jnp.concatenate([_block_diag(bb_re, gpb), _block_diag(bb_im, gpb)], axis=2)
    w["s5_wb"] = w["s5_wb_f32"].astype(BF16)
    c_re_t = jnp.swapaxes(p["s5_c_re"][l], 1, 2)
    c_im_t = jnp.swapaxes(p["s5_c_im"][l], 1, 2)
    w["s5_wc_f32"] = jnp.concatenate([_block_diag(c_re_t, gpb), -_block_diag(c_im_t, gpb)], axis=1)
    w["s5_wc"] = w["s5_wc_f32"].astype(BF16)
    w["s5_wglu"] = p["s5_w_glu"][l].astype(BF16)
    w["gla_wa_f32"] = jnp.concatenate(
        [jnp.zeros((FOX_HEADS, GLA_QK), F32), p["gla_w_a2"][l],
         jnp.zeros((TAIL_W - FOX_HEADS - GLA_RANK, GLA_QK), F32)], axis=0)
    w["gla_wa"] = w["gla_wa_f32"].astype(BF16)
    w["mem_q_gain"] = jnp.tile(p["mem_q_gain"][l], MEM_HEADS)
    w["mem_k_gain"] = jnp.tile(p["mem_k_gain"][l], MEM_HEADS)
    w["w_router"] = jnp.concatenate(
        [p["moe_w_expert"][l], p["moe_w_group"][l],
         jnp.zeros((D_MODEL, LANES - N_EXPERTS - N_GROUPS), F32)], axis=1)
    w["b_router"] = jnp.concatenate(
        [p["moe_b_expert"][l], p["moe_b_group"][l], jnp.zeros((LANES - N_EXPERTS - N_GROUPS,), F32)]).reshape(1, LANES)
    return w


def _in_projection(h, w, pb, l, tm):
    tn = 512
    (q_b,) = matmul(h, pb["w_in"], tm=tm, tn=tn, out_dtypes=(BF16,), gain=w["q_gain"],
                    post_scale=FOX_HD ** -0.5 * LOG2E, layer=l, cols=(0, FOX_W), name="proj_q")
    k_f, k_b = matmul(h, pb["w_in"], tm=tm, tn=tn, out_dtypes=(F32, BF16), gain=w["k_gain"], layer=l,
                      cols=(FOX_W, FOX_W), name="proj_k")
    v_f, v_b = matmul(h, pb["w_in"], tm=tm, tn=tn, out_dtypes=(F32, BF16), layer=l, cols=(2 * FOX_W, FOX_W),
                      name="proj_v")
    (rest,) = matmul(h, w["wrest"], tm=tm, tn=tn, name="proj_rest")
    (tail,) = matmul(h, w["wtail"], tm=tm, tn=TAIL_W, logsig_bias=w["tail_bias"], logsig_lanes=FOX_HEADS,
                     name="proj_tail")
    return q_b, k_f, k_b, v_f, v_b, rest, tail


def _post_mixer(x, branches, mk, mv, w, p, pb, l, b, t, tm, tq_mem, g_next):
    n = x.shape[0]
    tm_full = min(256, n)
    x, hq = proj_res_norm(branches, pb["w_out"], l, x, p["g_mem_pre"][l], tm_full, name="proj_out")
    (q,) = matmul(hq, pb["mem_w_q"], tm=tm, tn=512, out_dtypes=(BF16,), gain=w["mem_q_gain"],
                  post_scale=MEM_HD ** -0.5, layer=l, name="mem_q")
    o = memory_attention(q.reshape(b, t, MEM_W), mk, mv, tq_mem).reshape(n, MEM_W)
    x, hm = proj_res_norm([o], pb["mem_w_o"], l, x, p["g_moe"][l], tm_full, name="mem_o")
    gate = moe_router(x, p["g_moe"][l], w["w_router"], w["b_router"], tm)
    out = moe_experts(hm, gate, pb["moe_w_gate"], pb["moe_w_up"], pb["moe_w_down"], l, x, tm, g_next=g_next)
    return out if g_next is not None else (out, None)


def kernel(x_prompt, x_sample, cache_fox_k, cache_fox_v, cache_fox_logf, state_s5_re, state_s5_im, state_gla,
           cache_mem_k, cache_mem_v, page_table, mem_prompt, g_mix, w_in, fox_q_gain, fox_k_gain, fox_f_bias,
           s5_a_re, s5_a_im, s5_log_dt, s5_b_re, s5_b_im, s5_c_re, s5_c_im, s5_d, s5_w_glu, s5_b_glu,
           gla_w_a2, gla_b_a, gla_gain, g_fox_out, g_s5_out, w_out, g_mem_pre, g_mem_tok, mem_w_q, mem_w_k,
           mem_w_v, mem_q_gain, mem_k_gain, mem_w_o, g_moe, moe_w_group, moe_b_group, moe_w_expert,
           moe_b_expert, moe_w_gate, moe_w_up, moe_w_down):
    p = dict(g_mix=g_mix, w_in=w_in, fox_q_gain=fox_q_gain, fox_k_gain=fox_k_gain, fox_f_bias=fox_f_bias,
             s5_a_re=s5_a_re, s5_a_im=s5_a_im, s5_log_dt=s5_log_dt, s5_b_re=s5_b_re, s5_b_im=s5_b_im,
             s5_c_re=s5_c_re, s5_c_im=s5_c_im, s5_d=s5_d, s5_w_glu=s5_w_glu, s5_b_glu=s5_b_glu,
             gla_w_a2=gla_w_a2, gla_b_a=gla_b_a, gla_gain=gla_gain, g_fox_out=g_fox_out, g_s5_out=g_s5_out,
             w_out=w_out, g_mem_pre=g_mem_pre, g_mem_tok=g_mem_tok, mem_w_q=mem_w_q, mem_w_k=mem_w_k,
             mem_w_v=mem_w_v, mem_q_gain=mem_q_gain, mem_k_gain=mem_k_gain, mem_w_o=mem_w_o, g_moe=g_moe,
             moe_w_group=moe_w_group, moe_b_group=moe_b_group, moe_w_expert=moe_w_expert,
             moe_b_expert=moe_b_expert, moe_w_gate=moe_w_gate, moe_w_up=moe_w_up, moe_w_down=moe_w_down)
    depth = w_in.shape[0]
    b, t, d = x_prompt.shape
    db = x_sample.shape[0]
    n = b * t
    n_mem = b * MEM_TOKENS
    pb = {name: p[name].astype(BF16) for name in _BF16_STACKS}
    tm = min(512, n)
    tq_fox = min(512, t)
    s5_rows = min(256, t)
    gla_rows = min(256, t)
    tq_mem = min(512, t)

    xp = x_prompt.reshape(n, d)
    xs = jnp.pad(x_sample.reshape(db, d), ((0, SAMPLE_ROWS - db), (0, 0)))
    mem2d = mem_prompt.reshape(n_mem, d)
    cache_ft = jnp.swapaxes(cache_fox_logf, 2, 3)
    zero_s5 = jnp.zeros((b, 1, S5_NS), F32)
    zero_gla = jnp.zeros((b, GLA_HEADS, GLA_DV, GLA_DK), F32)
    outs = {k: [] for k in ("pk", "pv", "pf", "sk", "sv", "sf", "p5r", "p5i", "s5r", "s5i", "pg", "sg", "pmk", "pmv")}

    h_prompt = rmsnorm_cast(xp, g_mix[0], tm)
    for l in range(depth):
        w = _layer_weights(l, p, pb)

        q_b, k_f, k_b, v_f, v_b, rest, tail = _in_projection(h_prompt, w, pb, l, tm)
        crow, ccol = fox_gate_cumsum(tail, b, t)
        fox_o = fox_prompt_attention(q_b, k_b, v_b, ccol, crow, g_fox_out[l], b, t, tq_fox)
        rest3, tail3 = rest.reshape(b, t, 4 * S5_W), tail.reshape(b, t, TAIL_W)
        seg_steps = s5_rows // S5_SCAN_BLOCK
        su = rest3[:, :, :S5_W].reshape(b, t // s5_rows, S5_SCAN_BLOCK, seg_steps, S5_W)
        su = jnp.swapaxes(su, 2, 3).reshape(b, t, S5_W)
        s5_o, hre, him = s5_mixer(su, zero_s5, zero_s5, w["s5_lam_re"], w["s5_lam_im"], w["s5_wb"], w["s5_wc"],
                                  s5_d[l], w["s5_wglu"], s5_b_glu[l], g_s5_out[l],
                                  nseq=b, rows=s5_rows, nsteps=t // s5_rows, scan=True, col_block=0)
        s5_o = s5_o.reshape(b, t // s5_rows, seg_steps, S5_SCAN_BLOCK, S5_W)
        s5_o = jnp.swapaxes(s5_o, 2, 3).reshape(b, t, S5_W)
        gla_o, st = gla_mixer(rest3, tail3, w["gla_wa"], gla_b_a[l], gla_gain[l], zero_gla,
                              nseq=b, rows=gla_rows, nsteps=t // gla_rows, t_valid=gla_rows)
        branches = [fox_o, s5_o.reshape(n, S5_W), gla_o.reshape(n, GLA_W)]
        outs["pk"].append(k_f.reshape(b, t, FOX_HEADS, FOX_HD))
        outs["pv"].append(v_f.reshape(b, t, FOX_HEADS, FOX_HD))
        outs["pf"].append(tail3[:, :, :FOX_HEADS])
        outs["p5r"].append(hre.reshape(b, S5_GROUPS, S5_STATE))
        outs["p5i"].append(him.reshape(b, S5_GROUPS, S5_STATE))
        outs["pg"].append(jnp.swapaxes(st, 2, 3))
        hmem = rmsnorm_cast(mem2d, g_mem_tok[l], tm=min(512, n_mem))
        (mk,) = matmul(hmem, pb["mem_w_k"], tm=min(512, n_mem), tn=512, gain=w["mem_k_gain"], layer=l, name="mem_k")
        (mv,) = matmul(hmem, pb["mem_w_v"], tm=min(512, n_mem), tn=512, layer=l, name="mem_v")
        outs["pmk"].append(mk.reshape(b, MEM_TOKENS, MEM_HEADS, MEM_HD))
        outs["pmv"].append(mv.reshape(b, MEM_TOKENS, MEM_HEADS, MEM_HD))
        xp, h_prompt = _post_mixer(xp, branches, mk.reshape(b, MEM_TOKENS, MEM_W), mv.reshape(b, MEM_TOKENS, MEM_W),
                                   w, p, pb, l, b, t, tm, tq_mem, g_mix[l + 1] if l + 1 < depth else None)

        ns = SAMPLE_ROWS
        h = rmsnorm_cast(xs, g_mix[l], ns, out_dtype=F32)
        c0 = np.cumsum([0, FOX_W, FOX_W, FOX_W, FOX_HEADS, S5_W, GLA_QK, GLA_QK, GLA_W, GLA_RANK, GLA_W]).tolist()
        in_cols = c0[-1]
        qk_gain = jnp.concatenate([w["q_gain"], w["k_gain"]]).reshape(1, 2 * FOX_W)
        f_bias = jnp.concatenate([fox_f_bias[l], jnp.zeros((in_cols - 3 * FOX_W - FOX_HEADS,), F32)]).reshape(1, -1)
        proj = pmatmul(h, w_in, l, tk=256, epi="inproj", aux=(qk_gain, f_bias), name="proj_in_sample")
        col = lambda i: proj[:, c0[i]:c0[i + 1]]
        hd3 = lambda a: a[:db].reshape(db, FOX_HEADS, FOX_HD)
        q_f, k_f, v_f, logf_new = col(0), col(1), col(2), col(3)[:db]
        rest = jnp.concatenate([col(4), col(5), col(6), col(7), col(9)], axis=1)
        tail = jnp.concatenate([col(3), col(8), jnp.zeros((ns, TAIL_W - FOX_HEADS - GLA_RANK), F32)], axis=1)
        fox_o = fox_sample_attention(l, hd3(q_f), hd3(k_f), hd3(v_f), logf_new.reshape(db, FOX_HEADS, 1),
                                     cache_fox_k, cache_fox_v, cache_ft, page_table, g_fox_out[l],
                                     pages_per_step=DECODE_PAGES_PER_STEP)
        fox_o = jnp.pad(fox_o.reshape(db, FOX_W), ((0, ns - db), (0, 0)))
        pad_state = lambda a: jnp.pad(a.reshape(1, db, S5_NS), ((0, 0), (0, ns - db), (0, 0)))
        s5_o, hre, him = s5_mixer(rest.reshape(1, ns, 4 * S5_W), pad_state(state_s5_re[l]), pad_state(state_s5_im[l]),
                                  w["s5_lam_re"], w["s5_lam_im"], w["s5_wb_f32"], w["s5_wc_f32"], s5_d[l],
                                  s5_w_glu[l], s5_b_glu[l], g_s5_out[l], nseq=1, rows=ns, nsteps=1, scan=False,
                                  col_block=0, precise=True)
        seq_pad = lambda a: jnp.pad(a[:db].reshape(db, 1, a.shape[1]), ((0, 0), (0, GLA_CHUNK - 1), (0, 0)))
        gla_o, st = gla_mixer(seq_pad(rest), seq_pad(tail), w["gla_wa_f32"], gla_b_a[l], gla_gain[l],
                              jnp.swapaxes(state_gla[l], 2, 3), nseq=db, rows=GLA_CHUNK, nsteps=1, t_valid=1,
                              precise=True)
        gla_o = jnp.pad(gla_o[:, 0, :], ((0, ns - db), (0, 0)))
        cat = jnp.concatenate([fox_o, s5_o.reshape(ns, S5_W), gla_o], axis=1)
        outs["sk"].append(hd3(k_f).reshape(db, 1, FOX_HEADS, FOX_HD))
        outs["sv"].append(hd3(v_f).reshape(db, 1, FOX_HEADS, FOX_HD))
        outs["sf"].append(logf_new.reshape(db, 1, FOX_HEADS))
        outs["s5r"].append(hre[0, :db].reshape(db, S5_GROUPS, S5_STATE))
        outs["s5i"].append(him[0, :db].reshape(db, S5_GROUPS, S5_STATE))
        outs["sg"].append(jnp.swapaxes(st, 2, 3))
        xs = _post_mixer_sample(xs, cat, cache_mem_k[l].reshape(db, MEM_TOKENS, MEM_W),
                                cache_mem_v[l].reshape(db, MEM_TOKENS, MEM_W), w, p, l, db)

    stk = lambda k: jnp.stack(outs[k])
    return (xp.reshape(b, t, d), xs[:db].reshape(db, 1, d),
            stk("pk"), stk("pv"), stk("pf"), stk("sk"), stk("sv"), stk("sf"),
            stk("p5r"), stk("p5i"), stk("s5r"), stk("s5i"), stk("pg"), stk("sg"), stk("pmk"), stk("pmv"))


def _post_mixer_sample(x, cat, mk, mv, w, p, l, db):
    ns = x.shape[0]
    x = pmatmul(cat, p["w_out"], l, tk=512, epi="res", aux=(x,), name="proj_out_sample")
    hq = rmsnorm_cast(x, p["g_mem_pre"][l], ns, out_dtype=F32)
    q = pmatmul(hq, p["mem_w_q"], l, tk=512, epi="headnorm", aux=(w["mem_q_gain"].reshape(1, MEM_W),),
                post_scale=MEM_HD ** -0.5, name="mem_q_sample")
    q3 = jnp.pad(q[:db].reshape(db, 1, MEM_W), ((0, 0), (0, SAMPLE_ROWS - 1), (0, 0)))
    o = memory_attention(q3, mk, mv, SAMPLE_ROWS, precise=True)[:, 0, :]
    o = jnp.pad(o, ((0, ns - db), (0, 0)))
    x = pmatmul(o, p["mem_w_o"], l, tk=MEM_W, epi="res", aux=(x,), name="mem_o_sample")
    hm = rmsnorm_cast(x, p["g_moe"][l], ns, out_dtype=F32)
    gate = pmatmul(hm, w["w_router"][None], 0, tk=512, epi="router", aux=(w["b_router"],), name="router_sample")
    return moe_experts(hm, gate, p["moe_w_gate"], p["moe_w_up"], p["moe_w_down"], l, x, ns, precise=True)
```

```python
import functools
import math

import numpy as np
import jax
import jax.numpy as jnp
from jax import lax
from jax.experimental import pallas as pl
from jax.experimental.pallas import tpu as pltpu

F32, BF16 = jnp.float32, jnp.bfloat16

D_MODEL = 2048
FOX_HEADS, FOX_HD = 8, 128
FOX_W = FOX_HEADS * FOX_HD
S5_W, S5_GROUP, S5_GROUPS, S5_STATE = 512, 16, 32, 64
S5_NS = S5_GROUPS * S5_STATE
S5_BLOCKS = 4
S5_BC = S5_W // S5_BLOCKS
S5_BS = S5_NS // S5_BLOCKS
S5_SCAN_BLOCK = 8
GLA_HEADS, GLA_DK, GLA_DV = 4, 64, 128
GLA_W = GLA_HEADS * GLA_DV
GLA_QK = GLA_HEADS * GLA_DK
GLA_RANK = 16
GLA_TAU = 16.0
GLA_CHUNK = 16
MEM_TOKENS, MEM_HEADS, MEM_HD = 256, 4, 128
MEM_W = MEM_HEADS * MEM_HD
N_GROUPS, EXP_PER_GROUP, N_EXPERTS, D_EXPERT = 4, 4, 16, 256
PAGE_SIZE = 128
EPS = 1e-6
NEG = -1e30
LOG2E = math.log2(math.e)

LANES = 128
VMEM_LIMIT_BYTES = 48 * 1024 * 1024

PROJ_TN = 512
DECODE_GROUP = 16
DECODE_PAGES_PER_STEP = 8
TAIL_W = LANES
SAMPLE_ROWS = 16


def _params(*sem):
    return pltpu.CompilerParams(dimension_semantics=sem, vmem_limit_bytes=VMEM_LIMIT_BYTES)


def _sds(shape, dtype):
    return jax.ShapeDtypeStruct(shape, dtype)


_NN = (((1,), (0,)), ((), ()))
_NT = (((1,), (1,)), ((), ()))
_TN = (((0,), (0,)), ((), ()))


def _mxu(a, b, precise, dims=_NN):
    if precise:
        return lax.dot_general(a.astype(F32), b.astype(F32), dims, preferred_element_type=F32,
                               precision=lax.Precision.HIGHEST)
    return lax.dot_general(a.astype(BF16), b.astype(BF16), dims, preferred_element_type=F32)


def _rms_kernel(x_ref, g_ref, o_ref):
    x = x_ref[...]
    y = x * lax.rsqrt(jnp.mean(x * x, axis=-1, keepdims=True) + EPS)
    o_ref[...] = (y * g_ref[...]).astype(o_ref.dtype)


def rmsnorm_cast(x, g, tm, out_dtype=BF16):
    n, d = x.shape
    return pl.pallas_call(
        _rms_kernel, grid=(n // tm,),
        in_specs=[pl.BlockSpec((tm, d), lambda i: (i, 0)), pl.BlockSpec((1, d), lambda i: (0, 0))],
        out_specs=pl.BlockSpec((tm, d), lambda i: (i, 0)),
        out_shape=_sds((n, d), out_dtype), compiler_params=_params("parallel"), name="rmsnorm_cast",
    )(x, g.reshape(1, d))


def _mm_kernel(*refs, headnorm, post_scale, logsig_lanes, has_res, n_out):
    x_ref, w_ref = refs[0], refs[1]
    pos = 2
    gain_ref = bias_ref = res_ref = None
    if headnorm:
        gain_ref, pos = refs[pos], pos + 1
    if logsig_lanes:
        bias_ref, pos = refs[pos], pos + 1
    if has_res:
        res_ref, pos = refs[pos], pos + 1
    outs = refs[pos:pos + n_out]
    acc = jnp.dot(x_ref[...], w_ref[...], preferred_element_type=F32)
    if logsig_lanes:
        lane = lax.broadcasted_iota(jnp.int32, acc.shape, 1)
        acc = jnp.where(lane < logsig_lanes, jax.nn.log_sigmoid(acc + bias_ref[...]), acc)
    if has_res:
        acc = res_ref[...] + acc
    if headnorm:
        for c in range(acc.shape[1] // LANES):
            sl = slice(c * LANES, (c + 1) * LANES)
            blk = acc[:, sl]
            y = blk * lax.rsqrt(jnp.mean(blk * blk, axis=-1, keepdims=True) + EPS)
            y = y * gain_ref[:, sl]
            if post_scale != 1.0:
                y = y * post_scale
            for o in outs:
                o[:, sl] = y.astype(o.dtype)
    else:
        for o in outs:
            o[...] = acc.astype(o.dtype)


def matmul(x, w, *, tm, tn, out_dtypes=(F32,), gain=None, post_scale=1.0, logsig_bias=None,
           logsig_lanes=0, res=None, layer=None, cols=None, col_tiled=False, name="matmul"):
    n, k = x.shape
    if col_tiled:
        col0, c = cols if cols is not None else (0, w.shape[-3] * tn)
    else:
        col0, c = cols if cols is not None else (0, w.shape[-1])
    jb = col0 // tn
    args = [x, w]
    if col_tiled and layer is None:
        w_spec = pl.BlockSpec((None, k, tn), lambda i, j: (jb + j, 0, 0))
    elif col_tiled:
        w_spec = pl.BlockSpec((None, None, k, tn), lambda i, j: (layer, jb + j, 0, 0))
    elif layer is None:
        w_spec = pl.BlockSpec((k, tn), lambda i, j: (0, jb + j))
    else:
        w_spec = pl.BlockSpec((None, k, tn), lambda i, j: (layer, 0, jb + j))
    in_specs = [pl.BlockSpec((tm, k), lambda i, j: (i, 0)), w_spec]
    if gain is not None:
        args.append(gain.reshape(1, c).astype(F32))
        in_specs.append(pl.BlockSpec((1, tn), lambda i, j: (0, j)))
    if logsig_lanes:
        args.append(logsig_bias.reshape(1, c).astype(F32))
        in_specs.append(pl.BlockSpec((1, tn), lambda i, j: (0, j)))
    if res is not None:
        args.append(res)
        in_specs.append(pl.BlockSpec((tm, tn), lambda i, j: (i, j)))
    outs = pl.pallas_call(
        functools.partial(_mm_kernel, headnorm=gain is not None, post_scale=post_scale,
                          logsig_lanes=logsig_lanes, has_res=res is not None, n_out=len(out_dtypes)),
        grid=(n // tm, c // tn), in_specs=in_specs,
        out_specs=[pl.BlockSpec((tm, tn), lambda i, j: (i, j)) for _ in out_dtypes],
        out_shape=[_sds((n, c), dt) for dt in out_dtypes],
        compiler_params=_params("parallel", "parallel"), name=name,
    )(*args)
    return outs


def _proj_norm_kernel(*refs, ks):
    nx = len(ks)
    x_refs, w_ref, res_ref, g_ref, o_ref, h_ref = refs[:nx], refs[nx], refs[nx + 1], refs[nx + 2], refs[nx + 3], refs[nx + 4]
    acc, off = None, 0
    for x_ref, k in zip(x_refs, ks):
        part = jnp.dot(x_ref[...], w_ref[off:off + k, :], preferred_element_type=F32)
        acc = part if acc is None else acc + part
        off += k
    x_new = res_ref[...] + acc
    o_ref[...] = x_new
    y = x_new * lax.rsqrt(jnp.mean(x_new * x_new, axis=-1, keepdims=True) + EPS)
    h_ref[...] = (y * g_ref[...]).astype(h_ref.dtype)


def proj_res_norm(xs, w, layer, res, g, tm, name):
    n, d = res.shape
    ks = tuple(x.shape[1] for x in xs)
    ktot = sum(ks)
    in_specs = [pl.BlockSpec((tm, k), lambda i: (i, 0)) for k in ks]
    in_specs += [pl.BlockSpec((None, ktot, d), lambda i: (layer, 0, 0)), pl.BlockSpec((tm, d), lambda i: (i, 0)),
                 pl.BlockSpec((1, d), lambda i: (0, 0))]
    return pl.pallas_call(
        functools.partial(_proj_norm_kernel, ks=ks), grid=(n // tm,), in_specs=in_specs,
        out_specs=[pl.BlockSpec((tm, d), lambda i: (i, 0)), pl.BlockSpec((tm, d), lambda i: (i, 0))],
        out_shape=[_sds((n, d), F32), _sds((n, d), BF16)], compiler_params=_params("parallel"), name=name,
    )(*xs, w, res, g.reshape(1, d))


def _head_rmsnorm(blk, gain):
    return blk * lax.rsqrt(jnp.mean(blk * blk, axis=-1, keepdims=True) + EPS) * gain


def _pmm_kernel(*refs, epi, n_aux, post_scale):
    x_ref, w_ref = refs[0], refs[1]
    aux = refs[2:2 + n_aux]
    o_ref, acc_s = refs[2 + n_aux], refs[3 + n_aux]
    k = pl.program_id(0)

    @pl.when(k == 0)
    def _():
        acc_s[...] = jnp.zeros(acc_s.shape, F32)

    acc_s[...] += _mxu(x_ref[...], w_ref[...], True)

    @pl.when(k == pl.num_programs(0) - 1)
    def _():
        acc = acc_s[...]
        if epi == "none":
            o_ref[...] = acc
        elif epi == "res":
            o_ref[...] = aux[0][...] + acc
        elif epi == "headnorm":
            for c in range(acc.shape[1] // LANES):
                sl = slice(c * LANES, (c + 1) * LANES)
                o_ref[:, sl] = _head_rmsnorm(acc[:, sl], aux[0][:, sl]) * post_scale
        elif epi == "inproj":
            gain_ref, bias_ref = aux
            for c in range(2 * FOX_W // LANES):
                sl = slice(c * LANES, (c + 1) * LANES)
                o_ref[:, sl] = _head_rmsnorm(acc[:, sl], gain_ref[:, sl])
            o_ref[:, 2 * FOX_W:3 * FOX_W] = acc[:, 2 * FOX_W:3 * FOX_W]
            t = acc[:, 3 * FOX_W:]
            lane = lax.broadcasted_iota(jnp.int32, t.shape, 1)
            o_ref[:, 3 * FOX_W:] = jnp.where(lane < FOX_HEADS, jax.nn.log_sigmoid(t + bias_ref[...]), t)
        elif epi == "router":
            o_ref[...] = _router_gate(acc + aux[0][...])


def pmatmul(x, w, layer, *, tk, epi="none", aux=(), post_scale=1.0, name="pmatmul"):
    rows, kdim = x.shape
    c = w.shape[2]
    in_specs = [pl.BlockSpec((rows, tk), lambda k: (0, k)), pl.BlockSpec((None, tk, c), lambda k: (layer, k, 0))]
    in_specs += [pl.BlockSpec(a.shape, lambda k: (0, 0)) for a in aux]
    return pl.pallas_call(
        functools.partial(_pmm_kernel, epi=epi, n_aux=len(aux), post_scale=post_scale), grid=(kdim // tk,),
        in_specs=in_specs, out_specs=pl.BlockSpec((rows, c), lambda k: (0, 0)),
        out_shape=_sds((rows, c), F32), scratch_shapes=[pltpu.VMEM((rows, c), F32)],
        compiler_params=_params("arbitrary"), name=name,
    )(x, w, *aux)


def _fox_gate_kernel(t_ref, crow_ref, ccol_ref):
    x = t_ref[0]
    c = x.T[0:FOX_HEADS, :]
    t_len = c.shape[1]
    lane = lax.broadcasted_iota(jnp.int32, c.shape, 1)
    s = 1
    while s < t_len:
        c = c + jnp.where(lane >= s, pltpu.roll(c, s, axis=1), 0.0)
        s *= 2
    c = c * LOG2E
    crow_ref[0] = c
    cpad = jnp.concatenate([c, jnp.zeros((LANES - FOX_HEADS, t_len), F32)], axis=0)
    ccol_ref[0] = cpad.T


def fox_gate_cumsum(tail, b, t):
    return pl.pallas_call(
        _fox_gate_kernel, grid=(b,),
        in_specs=[pl.BlockSpec((1, t, TAIL_W), lambda i: (i, 0, 0))],
        out_specs=[pl.BlockSpec((1, FOX_HEADS, t), lambda i: (i, 0, 0)),
                   pl.BlockSpec((1, t, TAIL_W), lambda i: (i, 0, 0))],
        out_shape=[_sds((b, FOX_HEADS, t), F32), _sds((b, t, TAIL_W), F32)],
        compiler_params=_params("parallel"), name="fox_gate_cumsum",
    )(tail.reshape(b, t, TAIL_W))


def _lane_tile(x, n):
    return x if n == 1 else jnp.concatenate([x] * n, axis=1)


def _fox_attn_kernel(qi_tab, ki_tab, q_ref, k_ref, v_ref, cq_ref, ck_ref, g_ref, o_ref, m_s, l_s, acc_s, cq_s,
                     *, tq, tk):
    p_id = pl.program_id(1)
    qi, ki = qi_tab[p_id], ki_tab[p_id]
    nrep = tk // LANES

    @pl.when(ki == 0)
    def _():
        m_s[...] = jnp.full(m_s.shape, NEG, F32)
        l_s[...] = jnp.zeros(l_s.shape, F32)
        acc_s[...] = jnp.zeros(acc_s.shape, F32)
        for h in range(FOX_HEADS):
            cq_s[h] = jnp.broadcast_to(cq_ref[0, :, h:h + 1], (tq, LANES))

    def step(masked):
        if masked:
            row = lax.broadcasted_iota(jnp.int32, (tq, tk), 0)
            col = lax.broadcasted_iota(jnp.int32, (tq, tk), 1)
            causal = col <= row
        for h in range(FOX_HEADS):
            sl = slice(h * FOX_HD, (h + 1) * FOX_HD)
            s = lax.dot_general(q_ref[0, :, sl], k_ref[0, :, sl], _NT, preferred_element_type=F32)
            s = (s + _lane_tile(cq_s[h], nrep)) - ck_ref[0, h:h + 1, :]
            if masked:
                s = jnp.where(causal, s, NEG)
            m_old = m_s[h]
            m_new = jnp.maximum(m_old, jnp.broadcast_to(jnp.max(s, axis=-1, keepdims=True), (tq, LANES)))
            alpha = jnp.exp2(m_old - m_new)
            p = jnp.exp2(s - _lane_tile(m_new, nrep))
            l_s[h] = l_s[h] * alpha + jnp.broadcast_to(jnp.sum(p, axis=-1, keepdims=True), (tq, LANES))
            m_s[h] = m_new
            acc_s[:, sl] = acc_s[:, sl] * alpha + jnp.dot(p.astype(BF16), v_ref[0, :, sl],
                                                          preferred_element_type=F32)

    @pl.when(ki < qi)
    def _():
        step(False)

    @pl.when(ki == qi)
    def _():
        step(True)
        ssq = jnp.zeros((tq, 1), F32)
        for h in range(FOX_HEADS):
            sl = slice(h * FOX_HD, (h + 1) * FOX_HD)
            o = acc_s[:, sl] / l_s[h]
            acc_s[:, sl] = o
            ssq = ssq + jnp.sum(o * o, axis=-1, keepdims=True)
        inv = lax.rsqrt(ssq / FOX_W + EPS)
        o_ref[0] = (acc_s[...] * inv * g_ref[...]).astype(o_ref.dtype)


def fox_prompt_attention(q, k, v, ccol, crow, g_out, b, t, tq):
    nq = t // tq
    pairs = [(i, j) for i in range(nq) for j in range(i + 1)]
    qi_tab = jnp.asarray(np.array([p[0] for p in pairs], np.int32))
    ki_tab = jnp.asarray(np.array([p[1] for p in pairs], np.int32))
    q3, k3, v3 = (a.reshape(b, t, FOX_W) for a in (q, k, v))
    grid_spec = pltpu.PrefetchScalarGridSpec(
        num_scalar_prefetch=2, grid=(b, len(pairs)),
        in_specs=[
            pl.BlockSpec((1, tq, FOX_W), lambda bi, p, qt, kt: (bi, qt[p], 0)),
            pl.BlockSpec((1, tq, FOX_W), lambda bi, p, qt, kt: (bi, kt[p], 0)),
            pl.BlockSpec((1, tq, FOX_W), lambda bi, p, qt, kt: (bi, kt[p], 0)),
            pl.BlockSpec((1, tq, TAIL_W), lambda bi, p, qt, kt: (bi, qt[p], 0)),
            pl.BlockSpec((1, FOX_HEADS, tq), lambda bi, p, qt, kt: (bi, 0, kt[p])),
            pl.BlockSpec((1, FOX_W), lambda bi, p, qt, kt: (0, 0)),
        ],
        out_specs=pl.BlockSpec((1, tq, FOX_W), lambda bi, p, qt, kt: (bi, qt[p], 0)),
        scratch_shapes=[pltpu.VMEM((FOX_HEADS, tq, LANES), F32), pltpu.VMEM((FOX_HEADS, tq, LANES), F32),
                        pltpu.VMEM((tq, FOX_W), F32), pltpu.VMEM((FOX_HEADS, tq, LANES), F32)],
    )
    out = pl.pallas_call(
        functools.partial(_fox_attn_kernel, tq=tq, tk=tq), grid_spec=grid_spec,
        out_shape=_sds((b, t, FOX_W), BF16),
        compiler_params=_params("parallel", "arbitrary"), name="fox_prompt_attention",
    )(qi_tab, ki_tab, q3, k3, v3, ccol, crow, g_out.reshape(1, FOX_W))
    return out.reshape(b * t, FOX_W)


def _fox_decode_kernel(pt_ref, q_ref, kn_ref, vn_ref, fn_ref, g_ref, *refs, pages_per_step, scale):
    pp = pages_per_step
    k_refs, v_refs, f_refs = refs[0:pp], refs[pp:2 * pp], refs[2 * pp:3 * pp]
    o_ref = refs[3 * pp]
    m_s, l_s, acc_s, r_s = refs[3 * pp + 1:]
    step_id = pl.program_id(1)
    q = q_ref[0] * scale

    @pl.when(step_id == 0)
    def _():
        m_s[...] = jnp.sum(q * kn_ref[0], axis=-1, keepdims=True)
        l_s[...] = jnp.ones(l_s.shape, F32)
        acc_s[...] = vn_ref[0]
        r_s[...] = fn_ref[0]

    tok_r = lax.broadcasted_iota(jnp.int32, (PAGE_SIZE, PAGE_SIZE), 0)
    tok_c = lax.broadcasted_iota(jnp.int32, (PAGE_SIZE, PAGE_SIZE), 1)
    later = (tok_r > tok_c).astype(F32)
    r_run = r_s[...]
    parts = []
    grp = DECODE_GROUP
    tok = lax.broadcasted_iota(jnp.int32, (grp, 1, FOX_HD), 0)
    lane = lax.broadcasted_iota(jnp.int32, (grp, 1, FOX_HD), 2)
    for j in range(pp):
        f = f_refs[j][...]
        r = jnp.dot(f, later, preferred_element_type=F32, precision=lax.Precision.HIGHEST) + r_run
        r_run = r_run + jnp.sum(f, axis=-1, keepdims=True)
        for g in range(PAGE_SIZE // grp):
            rows = pl.ds(g * grp, grp)
            prod = k_refs[j][rows] * q[None] + jnp.where(lane == tok + g * grp, r[None], 0.0)
            s = jnp.sum(prod, axis=-1, keepdims=True)
            m_g = jnp.max(s, axis=0)
            p = jnp.exp(s - m_g[None])
            parts.append((m_g, jnp.sum(p, axis=0), jnp.sum(p * v_refs[j][rows], axis=0)))
    m_old = m_s[...]
    m_new = m_old
    for m_j, _, _ in parts:
        m_new = jnp.maximum(m_new, m_j)
    alpha = jnp.exp(m_old - m_new)
    l_run, acc = l_s[...] * alpha, acc_s[...] * alpha
    for m_j, l_j, pv_j in parts:
        a_j = jnp.exp(m_j - m_new)
        l_run, acc = l_run + l_j * a_j, acc + pv_j * a_j
    m_s[...], l_s[...], acc_s[...], r_s[...] = m_new, l_run, acc, r_run

    @pl.when(step_id == pl.num_programs(1) - 1)
    def _():
        o = acc_s[...] / l_s[...]
        ssq = jnp.sum(jnp.sum(o * o, axis=-1, keepdims=True), axis=0, keepdims=True)
        o_ref[0] = (o * lax.rsqrt(ssq / FOX_W + EPS) * g_ref[...]).astype(o_ref.dtype)


def fox_sample_attention(layer, q, k_new, v_new, logf_new, cache_k, cache_v, cache_ft, page_table, g_out,
                         pages_per_step):
    db, n_pages = page_table.shape
    n_steps = n_pages // pages_per_step

    def page_map(j):
        def im(bi, si, pt):
            return (layer, pt[bi, n_pages - 1 - (si * pages_per_step + j)], 0, 0, 0)
        return im

    def page_map_f(j):
        def im(bi, si, pt):
            return (layer, pt[bi, n_pages - 1 - (si * pages_per_step + j)], 0, 0)
        return im

    vec = pl.BlockSpec((1, FOX_HEADS, FOX_HD), lambda bi, si, pt: (bi, 0, 0))
    in_specs = [vec, vec, vec,
                pl.BlockSpec((1, FOX_HEADS, 1), lambda bi, si, pt: (bi, 0, 0)),
                pl.BlockSpec((FOX_HEADS, FOX_HD), lambda bi, si, pt: (0, 0))]
    in_specs += [pl.BlockSpec((None, None, PAGE_SIZE, FOX_HEADS, FOX_HD), page_map(j)) for j in range(pages_per_step)]
    in_specs += [pl.BlockSpec((None, None, PAGE_SIZE, FOX_HEADS, FOX_HD), page_map(j)) for j in range(pages_per_step)]
    in_specs += [pl.BlockSpec((None, None, FOX_HEADS, PAGE_SIZE), page_map_f(j)) for j in range(pages_per_step)]
    grid_spec = pltpu.PrefetchScalarGridSpec(
        num_scalar_prefetch=1, grid=(db, n_steps), in_specs=in_specs,
        out_specs=pl.BlockSpec((1, FOX_HEADS, FOX_HD), lambda bi, si, pt: (bi, 0, 0)),
        scratch_shapes=[pltpu.VMEM((FOX_HEADS, 1), F32), pltpu.VMEM((FOX_HEADS, 1), F32),
                        pltpu.VMEM((FOX_HEADS, FOX_HD), F32), pltpu.VMEM((FOX_HEADS, 1), F32)],
    )
    return pl.pallas_call(
        functools.partial(_fox_decode_kernel, pages_per_step=pages_per_step, scale=FOX_HD ** -0.5),
        grid_spec=grid_spec, out_shape=_sds((db, FOX_HEADS, FOX_HD), F32),
        compiler_params=_params("parallel", "arbitrary"), name="fox_sample_attention",
    )(page_table, q, k_new, v_new, logf_new, g_out.reshape(FOX_HEADS, FOX_HD),
      *([cache_k] * pages_per_step), *([cache_v] * pages_per_step), *([cache_ft] * pages_per_step))


def _s5_param_kernel(are_ref, aim_ref, ldt_ref, bre_ref, bim_ref, lre_ref, lim_ref, bbre_ref, bbim_ref):
    a_re, a_im = are_ref[...], aim_ref[...]
    dt = jnp.exp(ldt_ref[...])
    mag = jnp.exp(a_re * dt)
    lam_re, lam_im = mag * jnp.cos(a_im * dt), mag * jnp.sin(a_im * dt)
    den = a_re * a_re + a_im * a_im
    z_re = ((lam_re - 1.0) * a_re + lam_im * a_im) / den
    z_im = (lam_im * a_re - (lam_re - 1.0) * a_im) / den
    lre_ref[...] = lam_re
    lim_ref[...] = lam_im
    b_re, b_im = bre_ref[...], bim_ref[...]
    bbre_ref[...] = z_re[:, None, :] * b_re - z_im[:, None, :] * b_im
    bbim_ref[...] = z_re[:, None, :] * b_im + z_im[:, None, :] * b_re


def s5_discretise(a_re, a_im, log_dt, b_re_t, b_im_t):
    g, p = a_re.shape
    c = b_re_t.shape[1]
    return pl.pallas_call(
        _s5_param_kernel,
        out_shape=[_sds((g, p), F32), _sds((g, p), F32), _sds((g, c, p), F32), _sds((g, c, p), F32)],
        name="s5_discretise",
    )(a_re, a_im, log_dt.reshape(g, 1), b_re_t, b_im_t)


def _block_diag(w, groups_per_block):
    g, a, b = w.shape
    nb = g // groups_per_block
    eye = jnp.eye(groups_per_block, dtype=w.dtype)
    w5 = w.reshape(nb, groups_per_block, a, b)
    return jnp.einsum("jgab,gh->jgahb", w5, eye).reshape(nb, groups_per_block * a, groups_per_block * b)


def _row_tile(x, n):
    tile = jnp.broadcast_to(x, (S5_SCAN_BLOCK, x.shape[1]))
    return tile if n == S5_SCAN_BLOCK else jnp.concatenate([tile] * (n // S5_SCAN_BLOCK), axis=0)


def _s5_kernel(u_ref, h0re_ref, h0im_ref, lre_ref, lim_ref, wb_ref, wc_ref, d_ref, wglu_ref, bglu_ref, gout_ref,
               y_ref, hre_out, him_out, hre_s, him_s, cre_s, cim_s, y_s, *, rows, scan, precise):
    t_id = pl.program_id(1)
    lam_re, lam_im = lre_ref[...], lim_ref[...]

    @pl.when(t_id == 0)
    def _():
        cre_s[...] = h0re_ref[0]
        cim_s[...] = h0im_ref[0]

    u = u_ref[0]
    for j in range(S5_BLOCKS):
        bu = _mxu(u[:, j * S5_BC:(j + 1) * S5_BC], wb_ref[j], precise)
        hre_s[:, j * S5_BS:(j + 1) * S5_BS] = bu[:, :S5_BS]
        him_s[:, j * S5_BS:(j + 1) * S5_BS] = bu[:, S5_BS:]
    c_re, c_im = cre_s[...], cim_s[...]
    if scan:
        hre_s[0:1, :] = hre_s[0:1, :] + (lam_re * c_re - lam_im * c_im)
        him_s[0:1, :] = him_s[0:1, :] + (lam_re * c_im + lam_im * c_re)
        seg = S5_SCAN_BLOCK
        steps = rows // seg
        l_re, l_im = _row_tile(lam_re, seg), _row_tile(lam_im, seg)
        p_re, p_im = hre_s[0:seg, :], him_s[0:seg, :]
        for v in range(1, steps):
            sl = slice(v * seg, (v + 1) * seg)
            p_re, p_im = (hre_s[sl, :] + (l_re * p_re - l_im * p_im), him_s[sl, :] + (l_re * p_im + l_im * p_re))
            hre_s[sl, :] = p_re
            him_s[sl, :] = p_im
        f_re, f_im = lam_re, lam_im
        n = 1
        while n < steps:
            f_re, f_im = f_re * f_re - f_im * f_im, 2.0 * f_re * f_im
            n *= 2
        f_re, f_im = _row_tile(f_re, seg), _row_tile(f_im, seg)
        pos = lax.broadcasted_iota(jnp.int32, (seg, S5_NS), 0)
        e_re, e_im = p_re, p_im
        s = 1
        while s < seg:
            m_re, m_im = jnp.where(pos >= s, f_re, 0.0), jnp.where(pos >= s, f_im, 0.0)
            b_re, b_im = pltpu.roll(e_re, s, axis=0), pltpu.roll(e_im, s, axis=0)
            e_re, e_im = e_re + (m_re * b_re - m_im * b_im), e_im + (m_re * b_im + m_im * b_re)
            f_re, f_im = f_re * f_re - f_im * f_im, 2.0 * f_re * f_im
            s *= 2
        c_re = jnp.where(pos >= 1, pltpu.roll(e_re, 1, axis=0), 0.0)
        c_im = jnp.where(pos >= 1, pltpu.roll(e_im, 1, axis=0), 0.0)
        for v in range(steps):
            c_re, c_im = l_re * c_re - l_im * c_im, l_re * c_im + l_im * c_re
            sl = slice(v * seg, (v + 1) * seg)
            hre_s[sl, :] = hre_s[sl, :] + c_re
            him_s[sl, :] = him_s[sl, :] + c_im
        cre_s[...] = hre_s[rows - 1:rows, :]
        cim_s[...] = him_s[rows - 1:rows, :]
        hre_out[0] = hre_s[rows - 1:rows, :]
        him_out[0] = him_s[rows - 1:rows, :]
    else:
        hre_s[...] = hre_s[...] + (lam_re * c_re - lam_im * c_im)
        him_s[...] = him_s[...] + (lam_re * c_im + lam_im * c_re)
        hre_out[0] = hre_s[...]
        him_out[0] = him_s[...]

    for j in range(S5_BLOCKS):
        sl = slice(j * S5_BS, (j + 1) * S5_BS)
        yj = _mxu(hre_s[:, sl], wc_ref[j, 0:S5_BS, :], precise)
        yj = yj + _mxu(him_s[:, sl], wc_ref[j, S5_BS:2 * S5_BS, :], precise)
        y_s[:, j * S5_BC:(j + 1) * S5_BC] = yj
    y = y_s[...] + d_ref[...] * u
    y = jax.nn.gelu(y)
    z = _mxu(y, wglu_ref[...], precise) + bglu_ref[...]
    y = y * jax.nn.sigmoid(z)
    y = y * lax.rsqrt(jnp.mean(y * y, axis=-1, keepdims=True) + EPS) * gout_ref[...]
    y_ref[0] = y.astype(y_ref.dtype)


def s5_mixer(u, h0_re, h0_im, lam_re, lam_im, wb, wc, d_skip, w_glu, b_glu, g_out, *, nseq, rows, nsteps, scan,
             col_block, precise=False):
    st_rows = 1 if scan else rows
    full = lambda shape: pl.BlockSpec(shape, lambda b, t: tuple(0 for _ in shape))
    state_spec = pl.BlockSpec((1, st_rows, S5_NS), lambda b, t: (b, 0, 0))
    y, hre, him = pl.pallas_call(
        functools.partial(_s5_kernel, rows=rows, scan=scan, precise=precise), grid=(nseq, nsteps),
        in_specs=[pl.BlockSpec((1, rows, S5_W), lambda b, t: (b, t, col_block)),
                  state_spec, state_spec, full((1, S5_NS)), full((1, S5_NS)),
                  full((S5_BLOCKS, S5_BC, 2 * S5_BS)), full((S5_BLOCKS, 2 * S5_BS, S5_BC)),
                  full((1, S5_W)), full((S5_W, S5_W)), full((1, S5_W)), full((1, S5_W))],
        out_specs=[pl.BlockSpec((1, rows, S5_W), lambda b, t: (b, t, 0)), state_spec, state_spec],
        out_shape=[_sds((nseq, nsteps * rows, S5_W), F32 if precise else BF16),
                   _sds((nseq, st_rows, S5_NS), F32), _sds((nseq, st_rows, S5_NS), F32)],
        scratch_shapes=[pltpu.VMEM((rows, S5_NS), F32), pltpu.VMEM((rows, S5_NS), F32),
                        pltpu.VMEM((st_rows, S5_NS), F32), pltpu.VMEM((st_rows, S5_NS), F32),
                        pltpu.VMEM((rows, S5_W), F32)],
        compiler_params=_params("parallel", "arbitrary"), name="s5_mixer",
    )(u, h0_re, h0_im, lam_re, lam_im, wb, wc, d_skip.reshape(1, S5_W), w_glu, b_glu.reshape(1, S5_W),
      g_out.reshape(1, S5_W))
    return y, hre, him


def _gla_kernel(qk_ref, v_ref, go_ref, ga_ref, wa_ref, ba_ref, gain_ref, s0_ref, o_ref, st_out,
                st_s, o_s, *, rows, t_valid, precise):
    t_id = pl.program_id(1)

    @pl.when(t_id == 0)
    def _():
        st_s[...] = s0_ref[0]

    x = _mxu(ga_ref[0], wa_ref[...], precise) + ba_ref[...]
    la = jax.nn.log_sigmoid(x) / GLA_TAU
    row = lax.broadcasted_iota(jnp.int32, la.shape, 0)
    if t_valid < rows:
        la = jnp.where(row < t_valid, la, 0.0)
    pos = row & (GLA_CHUNK - 1)
    b, e = la, la
    s = 1
    while s < GLA_CHUNK:
        b = b + jnp.where(pos >= s, pltpu.roll(b, s, axis=0), 0.0)
        e = e + jnp.where(pos < GLA_CHUNK - s, pltpu.roll(e, rows - s, axis=0), 0.0)
        s *= 2
    b_last = b + e - la
    qk = qk_ref[0]
    q, k = qk[:, 0:GLA_QK] * (GLA_DK ** -0.5), qk[:, GLA_QK:2 * GLA_QK]
    qe = q * jnp.exp(b)
    ke = k * jnp.exp(-b)
    kd = k * jnp.exp(b_last - b)
    dec = jnp.exp(b_last)
    v = v_ref[0]
    if not precise:
        qe, ke, kd, v = qe.astype(BF16), ke.astype(BF16), kd.astype(BF16), v.astype(BF16)
    r_i = lax.broadcasted_iota(jnp.int32, (rows, rows), 0)
    c_i = lax.broadcasted_iota(jnp.int32, (rows, rows), 1)
    chunk_shift = GLA_CHUNK.bit_length() - 1
    causal = ((r_i >> chunk_shift) == (c_i >> chunk_shift)) & (c_i <= r_i)
    for h in range(GLA_HEADS):
        ks = slice(h * GLA_DK, (h + 1) * GLA_DK)
        vs = slice(h * GLA_DV, (h + 1) * GLA_DV)
        att = _mxu(qe[:, ks], ke[:, ks], precise, _NT)
        att = jnp.where(causal, att, 0.0)
        o_s[:, vs] = _mxu(att, v[:, vs], precise)

    state = [st_s[h] for h in range(GLA_HEADS)]
    for j in range(rows // GLA_CHUNK):
        rs = slice(j * GLA_CHUNK, (j + 1) * GLA_CHUNK)
        for h in range(GLA_HEADS):
            ks = slice(h * GLA_DK, (h + 1) * GLA_DK)
            vs = slice(h * GLA_DV, (h + 1) * GLA_DV)
            o_s[rs, vs] = o_s[rs, vs] + _mxu(qe[rs, ks], state[h], precise, _NT)
            upd = _mxu(v[rs, vs], kd[rs, ks], precise, _TN)
            state[h] = state[h] * dec[j * GLA_CHUNK:j * GLA_CHUNK + 1, ks] + upd
    for h in range(GLA_HEADS):
        st_s[h] = state[h]
    st_out[0] = st_s[...]

    go = go_ref[0]
    for h in range(GLA_HEADS):
        vs = slice(h * GLA_DV, (h + 1) * GLA_DV)
        o = o_s[:, vs]
        y = o * lax.rsqrt(jnp.mean(o * o, axis=-1, keepdims=True) + EPS) * gain_ref[...]
        o_ref[0, :, vs] = (y * jax.nn.silu(go[:, vs])).astype(o_ref.dtype)


def gla_mixer(rest, tail, wa_pad, b_a, gain, s0_t, *, nseq, rows, nsteps, t_valid, precise=False):
    full = lambda shape: pl.BlockSpec(shape, lambda b, t: tuple(0 for _ in shape))
    st_spec = pl.BlockSpec((1, GLA_HEADS, GLA_DV, GLA_DK), lambda b, t: (b, 0, 0, 0))
    o, st = pl.pallas_call(
        functools.partial(_gla_kernel, rows=rows, t_valid=t_valid, precise=precise), grid=(nseq, nsteps),
        in_specs=[pl.BlockSpec((1, rows, 2 * GLA_QK), lambda b, t: (b, t, 1)),
                  pl.BlockSpec((1, rows, GLA_W), lambda b, t: (b, t, 2)),
                  pl.BlockSpec((1, rows, GLA_W), lambda b, t: (b, t, 3)),
                  pl.BlockSpec((1, rows, TAIL_W), lambda b, t: (b, t, 0)),
                  full((TAIL_W, GLA_QK)), full((1, GLA_QK)), full((1, GLA_DV)), st_spec],
        out_specs=[pl.BlockSpec((1, rows, GLA_W), lambda b, t: (b, t, 0)), st_spec],
        out_shape=[_sds((nseq, nsteps * rows, GLA_W), F32 if precise else BF16),
                   _sds((nseq, GLA_HEADS, GLA_DV, GLA_DK), F32)],
        scratch_shapes=[pltpu.VMEM((GLA_HEADS, GLA_DV, GLA_DK), F32), pltpu.VMEM((rows, GLA_W), F32)],
        compiler_params=_params("parallel", "arbitrary"), name="gla_mixer",
    )(rest, rest, rest, tail, wa_pad, b_a.reshape(1, GLA_QK), gain.reshape(1, GLA_DV), s0_t)
    return o, st


def _mem_attn_kernel(q_ref, k_ref, v_ref, o_ref, *, precise):
    for h in range(MEM_HEADS):
        sl = slice(h * MEM_HD, (h + 1) * MEM_HD)
        s = _mxu(q_ref[0, :, sl], k_ref[0, :, sl], precise, _NT)
        p = jnp.exp(s - jnp.max(s, axis=-1, keepdims=True))
        o = _mxu(p, v_ref[0, :, sl], precise)
        o_ref[0, :, sl] = (o / jnp.sum(p, axis=-1, keepdims=True)).astype(o_ref.dtype)


def memory_attention(q, mk, mv, tq, precise=False):
    b, t, _ = q.shape
    return pl.pallas_call(
        functools.partial(_mem_attn_kernel, precise=precise), grid=(b, t // tq),
        in_specs=[pl.BlockSpec((1, tq, MEM_W), lambda bi, i: (bi, i, 0)),
                  pl.BlockSpec((1, MEM_TOKENS, MEM_W), lambda bi, i: (bi, 0, 0)),
                  pl.BlockSpec((1, MEM_TOKENS, MEM_W), lambda bi, i: (bi, 0, 0))],
        out_specs=pl.BlockSpec((1, tq, MEM_W), lambda bi, i: (bi, i, 0)),
        out_shape=_sds((b, t, MEM_W), F32 if precise else BF16), compiler_params=_params("parallel", "parallel"),
        name="memory_attention",
    )(q, mk, mv)


def _router_gate(logits):
    lane = lax.broadcasted_iota(jnp.int32, logits.shape, 1)
    is_group = (lane >= N_EXPERTS) & (lane < N_EXPERTS + N_GROUPS)
    lg = jnp.where(is_group, logits, NEG)
    eg = jnp.where(is_group, jnp.exp(lg - jnp.max(lg, axis=-1, keepdims=True)), 0.0)
    pg = eg / jnp.sum(eg, axis=-1, keepdims=True)
    g_w = jnp.max(pg, axis=-1, keepdims=True)
    lane_f = lane.astype(F32)
    far = float(4 * LANES)
    g_lane = jnp.min(jnp.where(is_group & (pg == g_w), lane_f, far), axis=-1, keepdims=True)
    group_of_lane = (lane >> (EXP_PER_GROUP.bit_length() - 1)).astype(F32)
    sel = (lane < N_EXPERTS) & (group_of_lane == g_lane - float(N_EXPERTS))
    le = jnp.where(sel, logits, NEG)
    ee = jnp.where(sel, jnp.exp(le - jnp.max(le, axis=-1, keepdims=True)), 0.0)
    pe = ee / jnp.sum(ee, axis=-1, keepdims=True)
    m1 = jnp.max(jnp.where(sel, pe, -1.0), axis=-1, keepdims=True)
    i1 = jnp.min(jnp.where(sel & (pe == m1), lane_f, far), axis=-1, keepdims=True)
    rest = sel & (lane_f != i1)
    m2 = jnp.max(jnp.where(rest, pe, -1.0), axis=-1, keepdims=True)
    i2 = jnp.min(jnp.where(rest & (pe == m2), lane_f, far), axis=-1, keepdims=True)
    tot = m1 + m2
    w = jnp.where(lane_f == i1, m1 / tot, jnp.where(lane_f == i2, m2 / tot, 0.0))
    return w * g_w


def _router_kernel(x_ref, g_ref, wh_ref, wl_ref, b_ref, gate_ref):
    x = x_ref[...]
    xn = x * lax.rsqrt(jnp.mean(x * x, axis=-1, keepdims=True) + EPS) * g_ref[...]
    xh = xn.astype(BF16)
    xl = (xn - xh.astype(F32)).astype(BF16)
    wh = wh_ref[...]
    logits = (jnp.dot(xh, wh, preferred_element_type=F32) + jnp.dot(xl, wh, preferred_element_type=F32)
              + jnp.dot(xh, wl_ref[...], preferred_element_type=F32))
    gate_ref[...] = _router_gate(logits + b_ref[...])


def moe_router(x, g, w_r, b_r, tm):
    n, d = x.shape
    w_hi = w_r.astype(BF16)
    w_lo = (w_r - w_hi.astype(F32)).astype(BF16)
    w_spec = pl.BlockSpec((d, LANES), lambda i: (0, 0))
    return pl.pallas_call(
        _router_kernel, grid=(n // tm,),
        in_specs=[pl.BlockSpec((tm, d), lambda i: (i, 0)), pl.BlockSpec((1, d), lambda i: (0, 0)),
                  w_spec, w_spec, pl.BlockSpec((1, LANES), lambda i: (0, 0))],
        out_specs=pl.BlockSpec((tm, LANES), lambda i: (i, 0)),
        out_shape=_sds((n, LANES), F32), compiler_params=_params("parallel"), name="moe_router",
    )(x, g.reshape(1, d), w_hi, w_lo, b_r)


def _expert_kernel(x_ref, gate_ref, wg_ref, wu_ref, wd_ref, res_ref, *rest, precise, next_norm):
    if next_norm:
        g_ref, o_ref, h_ref, acc_s = rest
    else:
        o_ref, acc_s = rest
    e = pl.program_id(1)

    @pl.when(e == 0)
    def _():
        acc_s[...] = jnp.zeros(acc_s.shape, F32)

    x = x_ref[...]
    a = jax.nn.silu(_mxu(x, wg_ref[...], precise)) * _mxu(x, wu_ref[...], precise)
    gate = gate_ref[...]
    lane = lax.broadcasted_iota(jnp.int32, gate.shape, 1)
    g_col = jnp.sum(jnp.where(lane == e, gate, 0.0), axis=-1, keepdims=True)
    acc_s[...] += _mxu(a * g_col, wd_ref[...], precise)

    @pl.when(e == pl.num_programs(1) - 1)
    def _():
        x_new = res_ref[...] + acc_s[...]
        o_ref[...] = x_new
        if next_norm:
            y = x_new * lax.rsqrt(jnp.mean(x_new * x_new, axis=-1, keepdims=True) + EPS)
            h_ref[...] = (y * g_ref[...]).astype(h_ref.dtype)


def moe_experts(x, gate, w_gate, w_up, w_down, layer, res, tm, precise=False, g_next=None):
    n, d = x.shape
    row = pl.BlockSpec((tm, d), lambda i, e: (i, 0))
    in_specs = [row, pl.BlockSpec((tm, LANES), lambda i, e: (i, 0)),
                pl.BlockSpec((None, None, d, D_EXPERT), lambda i, e: (layer, e, 0, 0)),
                pl.BlockSpec((None, None, d, D_EXPERT), lambda i, e: (layer, e, 0, 0)),
                pl.BlockSpec((None, None, D_EXPERT, d), lambda i, e: (layer, e, 0, 0)), row]
    args = [x, gate, w_gate, w_up, w_down, res]
    out_specs, out_shape = row, _sds((n, d), F32)
    if g_next is not None:
        in_specs.append(pl.BlockSpec((1, d), lambda i, e: (0, 0)))
        args.append(g_next.reshape(1, d))
        out_specs, out_shape = [row, row], [_sds((n, d), F32), _sds((n, d), BF16)]
    return pl.pallas_call(
        functools.partial(_expert_kernel, precise=precise, next_norm=g_next is not None), grid=(n // tm, N_EXPERTS),
        in_specs=in_specs, out_specs=out_specs, out_shape=out_shape, scratch_shapes=[pltpu.VMEM((tm, d), F32)],
        compiler_params=_params("parallel", "arbitrary"), name="moe_experts",
    )(*args)


_BF16_STACKS = ("w_in", "w_out", "mem_w_q", "mem_w_k", "mem_w_v", "mem_w_o", "moe_w_gate", "moe_w_up", "moe_w_down")


def _layer_weights(l, p, pb):
    w_in = pb["w_in"][l]
    c = np.cumsum([0, FOX_W, FOX_W, FOX_W, FOX_HEADS, S5_W, GLA_QK, GLA_QK, GLA_W, GLA_RANK, GLA_W]).tolist()
    col = lambda i: w_in[:, c[i]:c[i + 1]]
    w = {}
    wrest = jnp.concatenate([col(4), col(5), col(6), col(7), col(9)], axis=1)
    w["wrest_tiles"] = jnp.swapaxes(wrest.reshape(D_MODEL, wrest.shape[1] // PROJ_TN, PROJ_TN), 0, 1)
    w["wtail"] = jnp.concatenate(
        [col(3), col(8), jnp.zeros((D_MODEL, TAIL_W - FOX_HEADS - GLA_RANK), BF16)], axis=1)
    w["tail_bias"] = jnp.concatenate([p["fox_f_bias"][l], jnp.zeros((TAIL_W - FOX_HEADS,), F32)])
    w["q_gain"] = jnp.tile(p["fox_q_gain"][l], FOX_HEADS)
    w["k_gain"] = jnp.tile(p["fox_k_gain"][l], FOX_HEADS)
    lam_re, lam_im, bb_re, bb_im = s5_discretise(
        p["s5_a_re"][l], p["s5_a_im"][l], p["s5_log_dt"][l],
        jnp.swapaxes(p["s5_b_re"][l], 1, 2), jnp.swapaxes(p["s5_b_im"][l], 1, 2))
    gpb = S5_GROUPS // S5_BLOCKS
    w["s5_lam_re"], w["s5_lam_im"] = lam_re.reshape(1, S5_NS), lam_im.reshape(1, S5_NS)
    w["s5_wb_f32"] = jnp.concatenate([_block_diag(bb_re, gpb), _block_diag(bb_im, gpb)], axis=2)
    w["s5_wb"] = w["s5_wb_f32"].astype(BF16)
    c_re_t = jnp.swapaxes(p["s5_c_re"][l], 1, 2)
    c_im_t = jnp.swapaxes(p["s5_c_im"][l], 1, 2)
    w["s5_wc_f32"] = jnp.concatenate([_block_diag(c_re_t, gpb), -_block_diag(c_im_t, gpb)], axis=1)
    w["s5_wc"] = w["s5_wc_f32"].astype(BF16)
    w["s5_wglu"] = p["s5_w_glu"][l].astype(BF16)
    w["gla_wa_f32"] = jnp.concatenate(
        [jnp.zeros((FOX_HEADS, GLA_QK), F32), p["gla_w_a2"][l],
         jnp.zeros((TAIL_W - FOX_HEADS - GLA_RANK, GLA_QK), F32)], axis=0)
    w["gla_wa"] = w["gla_wa_f32"].astype(BF16)
    w["mem_q_gain"] = jnp.tile(p["mem_q_gain"][l], MEM_HEADS)
    w["mem_k_gain"] = jnp.tile(p["mem_k_gain"][l], MEM_HEADS)
    w["w_router"] = jnp.concatenate(
        [p["moe_w_expert"][l], p["moe_w_group"][l],
         jnp.zeros((D_MODEL, LANES - N_EXPERTS - N_GROUPS), F32)], axis=1)
    w["b_router"] = jnp.concatenate(
        [p["moe_b_expert"][l], p["moe_b_group"][l], jnp.zeros((LANES - N_EXPERTS - N_GROUPS,), F32)]).reshape(1, LANES)
    return w


def _in_projection(h, w, pb, l, tm):
    tn = PROJ_TN
    (q_b,) = matmul(h, pb["w_qkv_tiles"], tm=tm, tn=tn, out_dtypes=(BF16,), gain=w["q_gain"],
                    post_scale=FOX_HD ** -0.5 * LOG2E, layer=l, cols=(0, FOX_W), col_tiled=True, name="proj_q")
    k_f, k_b = matmul(h, pb["w_qkv_tiles"], tm=tm, tn=tn, out_dtypes=(F32, BF16), gain=w["k_gain"], layer=l,
                      cols=(FOX_W, FOX_W), col_tiled=True, name="proj_k")
    v_f, v_b = matmul(h, pb["w_qkv_tiles"], tm=tm, tn=tn, out_dtypes=(F32, BF16), layer=l,
                      cols=(2 * FOX_W, FOX_W), col_tiled=True, name="proj_v")
    (rest,) = matmul(h, w["wrest_tiles"], tm=tm, tn=tn, col_tiled=True, name="proj_rest")
    (tail,) = matmul(h, w["wtail"], tm=tm, tn=TAIL_W, logsig_bias=w["tail_bias"], logsig_lanes=FOX_HEADS,
                     name="proj_tail")
    return q_b, k_f, k_b, v_f, v_b, rest, tail


def _post_mixer(x, branches, mk, mv, w, p, pb, l, b, t, tm, tq_mem, g_next):
    n = x.shape[0]
    tm_full = min(256, n)
    x, hq = proj_res_norm(branches, pb["w_out"], l, x, p["g_mem_pre"][l], tm_full, name="proj_out")
    (q,) = matmul(hq, pb["mem_w_q"], tm=tm, tn=512, out_dtypes=(BF16,), gain=w["mem_q_gain"],
                  post_scale=MEM_HD ** -0.5, layer=l, name="mem_q")
    o = memory_attention(q.reshape(b, t, MEM_W), mk, mv, tq_mem).reshape(n, MEM_W)
    x, hm = proj_res_norm([o], pb["mem_w_o"], l, x, p["g_moe"][l], tm_full, name="mem_o")
    gate = moe_router(x, p["g_moe"][l], w["w_router"], w["b_router"], tm)
    out = moe_experts(hm, gate, pb["moe_w_gate"], pb["moe_w_up"], pb["moe_w_down"], l, x, tm, g_next=g_next)
    return out if g_next is not None else (out, None)


def kernel(x_prompt, x_sample, cache_fox_k, cache_fox_v, cache_fox_logf, state_s5_re, state_s5_im, state_gla,
           cache_mem_k, cache_mem_v, page_table, mem_prompt, g_mix, w_in, fox_q_gain, fox_k_gain, fox_f_bias,
           s5_a_re, s5_a_im, s5_log_dt, s5_b_re, s5_b_im, s5_c_re, s5_c_im, s5_d, s5_w_glu, s5_b_glu,
           gla_w_a2, gla_b_a, gla_gain, g_fox_out, g_s5_out, w_out, g_mem_pre, g_mem_tok, mem_w_q, mem_w_k,
           mem_w_v, mem_q_gain, mem_k_gain, mem_w_o, g_moe, moe_w_group, moe_b_group, moe_w_expert,
           moe_b_expert, moe_w_gate, moe_w_up, moe_w_down):
    p = dict(g_mix=g_mix, w_in=w_in, fox_q_gain=fox_q_gain, fox_k_gain=fox_k_gain, fox_f_bias=fox_f_bias,
             s5_a_re=s5_a_re, s5_a_im=s5_a_im, s5_log_dt=s5_log_dt, s5_b_re=s5_b_re, s5_b_im=s5_b_im,
             s5_c_re=s5_c_re, s5_c_im=s5_c_im, s5_d=s5_d, s5_w_glu=s5_w_glu, s5_b_glu=s5_b_glu,
             gla_w_a2=gla_w_a2, gla_b_a=gla_b_a, gla_gain=gla_gain, g_fox_out=g_fox_out, g_s5_out=g_s5_out,
             w_out=w_out, g_mem_pre=g_mem_pre, g_mem_tok=g_mem_tok, mem_w_q=mem_w_q, mem_w_k=mem_w_k,
             mem_w_v=mem_w_v, mem_q_gain=mem_q_gain, mem_k_gain=mem_k_gain, mem_w_o=mem_w_o, g_moe=g_moe,
             moe_w_group=moe_w_group, moe_b_group=moe_b_group, moe_w_expert=moe_w_expert,
             moe_b_expert=moe_b_expert, moe_w_gate=moe_w_gate, moe_w_up=moe_w_up, moe_w_down=moe_w_down)
    depth = w_in.shape[0]
    b, t, d = x_prompt.shape
    db = x_sample.shape[0]
    n = b * t
    n_mem = b * MEM_TOKENS
    pb = {name: p[name].astype(BF16) for name in _BF16_STACKS}
    qkv = pb["w_in"][:, :, :3 * FOX_W].reshape(depth, D_MODEL, 3 * FOX_W // PROJ_TN, PROJ_TN)
    pb["w_qkv_tiles"] = jnp.swapaxes(qkv, 1, 2)
    tm = min(512, n)
    tq_fox = min(512, t)
    s5_rows = min(256, t)
    gla_rows = min(256, t)
    tq_mem = min(512, t)

    xp = x_prompt.reshape(n, d)
    xs = jnp.pad(x_sample.reshape(db, d), ((0, SAMPLE_ROWS - db), (0, 0)))
    mem2d = mem_prompt.reshape(n_mem, d)
    cache_ft = jnp.swapaxes(cache_fox_logf, 2, 3)
    zero_s5 = jnp.zeros((b, 1, S5_NS), F32)
    zero_gla = jnp.zeros((b, GLA_HEADS, GLA_DV, GLA_DK), F32)
    outs = {k: [] for k in ("pk", "pv", "pf", "sk", "sv", "sf", "p5r", "p5i", "s5r", "s5i", "pg", "sg", "pmk", "pmv")}

    h_prompt = rmsnorm_cast(xp, g_mix[0], tm)
    for l in range(depth):
        w = _layer_weights(l, p, pb)

        q_b, k_f, k_b, v_f, v_b, rest, tail = _in_projection(h_prompt, w, pb, l, tm)
        crow, ccol = fox_gate_cumsum(tail, b, t)
        fox_o = fox_prompt_attention(q_b, k_b, v_b, ccol, crow, g_fox_out[l], b, t, tq_fox)
        rest3, tail3 = rest.reshape(b, t, 4 * S5_W), tail.reshape(b, t, TAIL_W)
        seg_steps = s5_rows // S5_SCAN_BLOCK
        su = rest3[:, :, :S5_W].reshape(b, t // s5_rows, S5_SCAN_BLOCK, seg_steps, S5_W)
        su = jnp.swapaxes(su, 2, 3).reshape(b, t, S5_W)
        s5_o, hre, him = s5_mixer(su, zero_s5, zero_s5, w["s5_lam_re"], w["s5_lam_im"], w["s5_wb"], w["s5_wc"],
                                  s5_d[l], w["s5_wglu"], s5_b_glu[l], g_s5_out[l],
                                  nseq=b, rows=s5_rows, nsteps=t // s5_rows, scan=True, col_block=0)
        s5_o = s5_o.reshape(b, t // s5_rows, seg_steps, S5_SCAN_BLOCK, S5_W)
        s5_o = jnp.swapaxes(s5_o, 2, 3).reshape(b, t, S5_W)
        gla_o, st = gla_mixer(rest3, tail3, w["gla_wa"], gla_b_a[l], gla_gain[l], zero_gla,
                              nseq=b, rows=gla_rows, nsteps=t // gla_rows, t_valid=gla_rows)
        branches = [fox_o, s5_o.reshape(n, S5_W), gla_o.reshape(n, GLA_W)]
        outs["pk"].append(k_f.reshape(b, t, FOX_HEADS, FOX_HD))
        outs["pv"].append(v_f.reshape(b, t, FOX_HEADS, FOX_HD))
        outs["pf"].append(tail3[:, :, :FOX_HEADS])
        outs["p5r"].append(hre.reshape(b, S5_GROUPS, S5_STATE))
        outs["p5i"].append(him.reshape(b, S5_GROUPS, S5_STATE))
        outs["pg"].append(jnp.swapaxes(st, 2, 3))
        hmem = rmsnorm_cast(mem2d, g_mem_tok[l], tm=min(512, n_mem))
        (mk,) = matmul(hmem, pb["mem_w_k"], tm=min(512, n_mem), tn=512, gain=w["mem_k_gain"], layer=l, name="mem_k")
        (mv,) = matmul(hmem, pb["mem_w_v"], tm=min(512, n_mem), tn=512, layer=l, name="mem_v")
        outs["pmk"].append(mk.reshape(b, MEM_TOKENS, MEM_HEADS, MEM_HD))
        outs["pmv"].append(mv.reshape(b, MEM_TOKENS, MEM_HEADS, MEM_HD))
        xp, h_prompt = _post_mixer(xp, branches, mk.reshape(b, MEM_TOKENS, MEM_W), mv.reshape(b, MEM_TOKENS, MEM_W),
                                   w, p, pb, l, b, t, tm, tq_mem, g_mix[l + 1] if l + 1 < depth else None)

        ns = SAMPLE_ROWS
        h = rmsnorm_cast(xs, g_mix[l], ns, out_dtype=F32)
        c0 = np.cumsum([0, FOX_W, FOX_W, FOX_W, FOX_HEADS, S5_W, GLA_QK, GLA_QK, GLA_W, GLA_RANK, GLA_W]).tolist()
        in_cols = c0[-1]
        qk_gain = jnp.concatenate([w["q_gain"], w["k_gain"]]).reshape(1, 2 * FOX_W)
        f_bias = jnp.concatenate([fox_f_bias[l], jnp.zeros((in_cols - 3 * FOX_W - FOX_HEADS,), F32)]).reshape(1, -1)
        proj = pmatmul(h, w_in, l, tk=256, epi="inproj", aux=(qk_gain, f_bias), name="proj_in_sample")
        col = lambda i: proj[:, c0[i]:c0[i + 1]]
        hd3 = lambda a: a[:db].reshape(db, FOX_HEADS, FOX_HD)
        q_f, k_f, v_f, logf_new = col(0), col(1), col(2), col(3)[:db]
        rest = jnp.concatenate([col(4), col(5), col(6), col(7), col(9)], axis=1)
        tail = jnp.concatenate([col(3), col(8), jnp.zeros((ns, TAIL_W - FOX_HEADS - GLA_RANK), F32)], axis=1)
        fox_o = fox_sample_attention(l, hd3(q_f), hd3(k_f), hd3(v_f), logf_new.reshape(db, FOX_HEADS, 1),
                                     cache_fox_k, cache_fox_v, cache_ft, page_table, g_fox_out[l],
                                     pages_per_step=DECODE_PAGES_PER_STEP)
        fox_o = jnp.pad(fox_o.reshape(db, FOX_W), ((0, ns - db), (0, 0)))
        pad_state = lambda a: jnp.pad(a.reshape(1, db, S5_NS), ((0, 0), (0, ns - db), (0, 0)))
        s5_o, hre, him = s5_mixer(rest.reshape(1, ns, 4 * S5_W), pad_state(state_s5_re[l]), pad_state(state_s5_im[l]),
                                  w["s5_lam_re"], w["s5_lam_im"], w["s5_wb_f32"], w["s5_wc_f32"], s5_d[l],
                                  s5_w_glu[l], s5_b_glu[l], g_s5_out[l], nseq=1, rows=ns, nsteps=1, scan=False,
                                  col_block=0, precise=True)
        seq_pad = lambda a: jnp.pad(a[:db].reshape(db, 1, a.shape[1]), ((0, 0), (0, GLA_CHUNK - 1), (0, 0)))
        gla_o, st = gla_mixer(seq_pad(rest), seq_pad(tail), w["gla_wa_f32"], gla_b_a[l], gla_gain[l],
                              jnp.swapaxes(state_gla[l], 2, 3), nseq=db, rows=GLA_CHUNK, nsteps=1, t_valid=1,
                              precise=True)
        gla_o = jnp.pad(gla_o[:, 0, :], ((0, ns - db), (0, 0)))
        cat = jnp.concatenate([fox_o, s5_o.reshape(ns, S5_W), gla_o], axis=1)
        outs["sk"].append(hd3(k_f).reshape(db, 1, FOX_HEADS, FOX_HD))
        outs["sv"].append(hd3(v_f).reshape(db, 1, FOX_HEADS, FOX_HD))
        outs["sf"].append(logf_new.reshape(db, 1, FOX_HEADS))
        outs["s5r"].append(hre[0, :db].reshape(db, S5_GROUPS, S5_STATE))
        outs["s5i"].append(him[0, :db].reshape(db, S5_GROUPS, S5_STATE))
        outs["sg"].append(jnp.swapaxes(st, 2, 3))
        xs = _post_mixer_sample(xs, cat, cache_mem_k[l].reshape(db, MEM_TOKENS, MEM_W),
                                cache_mem_v[l].reshape(db, MEM_TOKENS, MEM_W), w, p, l, db)

    stk = lambda k: jnp.stack(outs[k])
    return (xp.reshape(b, t, d), xs[:db].reshape(db, 1, d),
            stk("pk"), stk("pv"), stk("pf"), stk("sk"), stk("sv"), stk("sf"),
            stk("p5r"), stk("p5i"), stk("s5r"), stk("s5i"), stk("pg"), stk("sg"), stk("pmk"), stk("pmv"))


def _post_mixer_sample(x, cat, mk, mv, w, p, l, db):
    ns = x.shape[0]
    x = pmatmul(cat, p["w_out"], l, tk=512, epi="res", aux=(x,), name="proj_out_sample")
    hq = rmsnorm_cast(x, p["g_mem_pre"][l], ns, out_dtype=F32)
    q = pmatmul(hq, p["mem_w_q"], l, tk=512, epi="headnorm", aux=(w["mem_q_gain"].reshape(1, MEM_W),),
                post_scale=MEM_HD ** -0.5, name="mem_q_sample")
    q3 = jnp.pad(q[:db].reshape(db, 1, MEM_W), ((0, 0), (0, SAMPLE_ROWS - 1), (0, 0)))
    o = memory_attention(q3, mk, mv, SAMPLE_ROWS, precise=True)[:, 0, :]
    o = jnp.pad(o, ((0, ns - db), (0, 0)))
    x = pmatmul(o, p["mem_w_o"], l, tk=MEM_W, epi="res", aux=(x,), name="mem_o_sample")
    hm = rmsnorm_cast(x, p["g_moe"][l], ns, out_dtype=F32)
    gate = pmatmul(hm, w["w_router"][None], 0, tk=512, epi="router", aux=(w["b_router"],), name="router_sample")
    return moe_experts(hm, gate, p["moe_w_gate"], p["moe_w_up"], p["moe_w_down"], l, x, ns, precise=True)
```

```python
import functools
import math

import numpy as np
import jax
import jax.numpy as jnp
from jax import lax
from jax.experimental import pallas as pl
from jax.experimental.pallas import tpu as pltpu

F32, BF16 = jnp.float32, jnp.bfloat16

D_MODEL = 2048
FOX_HEADS, FOX_HD = 8, 128
FOX_W = FOX_HEADS * FOX_HD
S5_W, S5_GROUP, S5_GROUPS, S5_STATE = 512, 16, 32, 64
S5_NS = S5_GROUPS * S5_STATE
S5_BLOCKS = 4
S5_BC = S5_W // S5_BLOCKS
S5_BS = S5_NS // S5_BLOCKS
S5_SCAN_BLOCK = 8
GLA_HEADS, GLA_DK, GLA_DV = 4, 64, 128
GLA_W = GLA_HEADS * GLA_DV
GLA_QK = GLA_HEADS * GLA_DK
GLA_RANK = 16
GLA_TAU = 16.0
GLA_CHUNK = 16
MEM_TOKENS, MEM_HEADS, MEM_HD = 256, 4, 128
MEM_W = MEM_HEADS * MEM_HD
N_GROUPS, EXP_PER_GROUP, N_EXPERTS, D_EXPERT = 4, 4, 16, 256
PAGE_SIZE = 128
EPS = 1e-6
NEG = -1e30
LOG2E = math.log2(math.e)

LANES = 128
VMEM_LIMIT_BYTES = 48 * 1024 * 1024

PROJ_TN = 512
DECODE_GROUP = 16
DECODE_PAGES_PER_STEP = 8
TAIL_W = LANES
SAMPLE_ROWS = 16


def _params(*sem):
    return pltpu.CompilerParams(dimension_semantics=sem, vmem_limit_bytes=VMEM_LIMIT_BYTES)


def _sds(shape, dtype):
    return jax.ShapeDtypeStruct(shape, dtype)


_NN = (((1,), (0,)), ((), ()))
_NT = (((1,), (1,)), ((), ()))
_TN = (((0,), (0,)), ((), ()))


def _mxu(a, b, precise, dims=_NN):
    if precise:
        return lax.dot_general(a.astype(F32), b.astype(F32), dims, preferred_element_type=F32,
                               precision=lax.Precision.HIGHEST)
    return lax.dot_general(a.astype(BF16), b.astype(BF16), dims, preferred_element_type=F32)


def _rms_kernel(x_ref, g_ref, o_ref):
    x = x_ref[...]
    y = x * lax.rsqrt(jnp.mean(x * x, axis=-1, keepdims=True) + EPS)
    o_ref[...] = (y * g_ref[...]).astype(o_ref.dtype)


def rmsnorm_cast(x, g, tm, out_dtype=BF16):
    n, d = x.shape
    return pl.pallas_call(
        _rms_kernel, grid=(n // tm,),
        in_specs=[pl.BlockSpec((tm, d), lambda i: (i, 0)), pl.BlockSpec((1, d), lambda i: (0, 0))],
        out_specs=pl.BlockSpec((tm, d), lambda i: (i, 0)),
        out_shape=_sds((n, d), out_dtype), compiler_params=_params("parallel"), name="rmsnorm_cast",
    )(x, g.reshape(1, d))


def _mm_kernel(*refs, headnorm, post_scale, logsig_lanes, has_res, n_out):
    x_ref, w_ref = refs[0], refs[1]
    pos = 2
    gain_ref = bias_ref = res_ref = None
    if headnorm:
        gain_ref, pos = refs[pos], pos + 1
    if logsig_lanes:
        bias_ref, pos = refs[pos], pos + 1
    if has_res:
        res_ref, pos = refs[pos], pos + 1
    outs = refs[pos:pos + n_out]
    acc = jnp.dot(x_ref[...], w_ref[...], preferred_element_type=F32)
    if logsig_lanes:
        lane = lax.broadcasted_iota(jnp.int32, acc.shape, 1)
        acc = jnp.where(lane < logsig_lanes, jax.nn.log_sigmoid(acc + bias_ref[...]), acc)
    if has_res:
        acc = res_ref[...] + acc
    if headnorm:
        for c in range(acc.shape[1] // LANES):
            sl = slice(c * LANES, (c + 1) * LANES)
            blk = acc[:, sl]
            y = blk * lax.rsqrt(jnp.mean(blk * blk, axis=-1, keepdims=True) + EPS)
            y = y * gain_ref[:, sl]
            if post_scale != 1.0:
                y = y * post_scale
            for o in outs:
                o[:, sl] = y.astype(o.dtype)
    else:
        for o in outs:
            o[...] = acc.astype(o.dtype)


def matmul(x, w, *, tm, tn, out_dtypes=(F32,), gain=None, post_scale=1.0, logsig_bias=None,
           logsig_lanes=0, res=None, layer=None, cols=None, name="matmul"):
    n, k = x.shape
    col0, c = cols if cols is not None else (0, w.shape[-1])
    jb = col0 // tn
    args = [x, w]
    if layer is None:
        w_spec = pl.BlockSpec((k, tn), lambda i, j: (0, jb + j))
    else:
        w_spec = pl.BlockSpec((None, k, tn), lambda i, j: (layer, 0, jb + j))
    in_specs = [pl.BlockSpec((tm, k), lambda i, j: (i, 0)), w_spec]
    if gain is not None:
        args.append(gain.reshape(1, c).astype(F32))
        in_specs.append(pl.BlockSpec((1, tn), lambda i, j: (0, j)))
    if logsig_lanes:
        args.append(logsig_bias.reshape(1, c).astype(F32))
        in_specs.append(pl.BlockSpec((1, tn), lambda i, j: (0, j)))
    if res is not None:
        args.append(res)
        in_specs.append(pl.BlockSpec((tm, tn), lambda i, j: (i, j)))
    outs = pl.pallas_call(
        functools.partial(_mm_kernel, headnorm=gain is not None, post_scale=post_scale,
                          logsig_lanes=logsig_lanes, has_res=res is not None, n_out=len(out_dtypes)),
        grid=(n // tm, c // tn), in_specs=in_specs,
        out_specs=[pl.BlockSpec((tm, tn), lambda i, j: (i, j)) for _ in out_dtypes],
        out_shape=[_sds((n, c), dt) for dt in out_dtypes],
        compiler_params=_params("parallel", "parallel"), name=name,
    )(*args)
    return outs


def _proj_norm_kernel(*refs, ks):
    nx = len(ks)
    x_refs, w_ref, res_ref, g_ref, o_ref, h_ref = refs[:nx], refs[nx], refs[nx + 1], refs[nx + 2], refs[nx + 3], refs[nx + 4]
    acc, off = None, 0
    for x_ref, k in zip(x_refs, ks):
        part = jnp.dot(x_ref[...], w_ref[off:off + k, :], preferred_element_type=F32)
        acc = part if acc is None else acc + part
        off += k
    x_new = res_ref[...] + acc
    o_ref[...] = x_new
    y = x_new * lax.rsqrt(jnp.mean(x_new * x_new, axis=-1, keepdims=True) + EPS)
    h_ref[...] = (y * g_ref[...]).astype(h_ref.dtype)


def proj_res_norm(xs, w, layer, res, g, tm, name):
    n, d = res.shape
    ks = tuple(x.shape[1] for x in xs)
    ktot = sum(ks)
    in_specs = [pl.BlockSpec((tm, k), lambda i: (i, 0)) for k in ks]
    in_specs += [pl.BlockSpec((None, ktot, d), lambda i: (layer, 0, 0)), pl.BlockSpec((tm, d), lambda i: (i, 0)),
                 pl.BlockSpec((1, d), lambda i: (0, 0))]
    return pl.pallas_call(
        functools.partial(_proj_norm_kernel, ks=ks), grid=(n // tm,), in_specs=in_specs,
        out_specs=[pl.BlockSpec((tm, d), lambda i: (i, 0)), pl.BlockSpec((tm, d), lambda i: (i, 0))],
        out_shape=[_sds((n, d), F32), _sds((n, d), BF16)], compiler_params=_params("parallel"), name=name,
    )(*xs, w, res, g.reshape(1, d))


def _head_rmsnorm(blk, gain):
    return blk * lax.rsqrt(jnp.mean(blk * blk, axis=-1, keepdims=True) + EPS) * gain


def _pmm_kernel(*refs, epi, n_aux, post_scale, w_cols_first):
    x_ref, w_ref = refs[0], refs[1]
    aux = refs[2:2 + n_aux]
    o_ref, acc_s = refs[2 + n_aux], refs[3 + n_aux]
    k = pl.program_id(0)

    @pl.when(k == 0)
    def _():
        acc_s[...] = jnp.zeros(acc_s.shape, F32)

    acc_s[...] += _mxu(x_ref[...], w_ref[...], True, _NT if w_cols_first else _NN)

    @pl.when(k == pl.num_programs(0) - 1)
    def _():
        acc = acc_s[...]
        if epi == "none":
            o_ref[...] = acc
        elif epi == "res":
            o_ref[...] = aux[0][...] + acc
        elif epi == "headnorm":
            for c in range(acc.shape[1] // LANES):
                sl = slice(c * LANES, (c + 1) * LANES)
                o_ref[:, sl] = _head_rmsnorm(acc[:, sl], aux[0][:, sl]) * post_scale
        elif epi == "inproj":
            gain_ref, bias_ref = aux
            for c in range(2 * FOX_W // LANES):
                sl = slice(c * LANES, (c + 1) * LANES)
                o_ref[:, sl] = _head_rmsnorm(acc[:, sl], gain_ref[:, sl])
            o_ref[:, 2 * FOX_W:3 * FOX_W] = acc[:, 2 * FOX_W:3 * FOX_W]
            t = acc[:, 3 * FOX_W:]
            lane = lax.broadcasted_iota(jnp.int32, t.shape, 1)
            o_ref[:, 3 * FOX_W:] = jnp.where(lane < FOX_HEADS, jax.nn.log_sigmoid(t + bias_ref[...]), t)
        elif epi == "router":
            o_ref[...] = _router_gate(acc + aux[0][...])


def pmatmul(x, w, layer, *, tk, epi="none", aux=(), post_scale=1.0, w_cols_first=False, name="pmatmul"):
    rows, kdim = x.shape
    if w_cols_first:
        c = w.shape[1]
        w_spec = pl.BlockSpec((None, c, tk), lambda k: (layer, 0, k))
    else:
        c = w.shape[2]
        w_spec = pl.BlockSpec((None, tk, c), lambda k: (layer, k, 0))
    in_specs = [pl.BlockSpec((rows, tk), lambda k: (0, k)), w_spec]
    in_specs += [pl.BlockSpec(a.shape, lambda k: (0, 0)) for a in aux]
    return pl.pallas_call(
        functools.partial(_pmm_kernel, epi=epi, n_aux=len(aux), post_scale=post_scale, w_cols_first=w_cols_first),
        grid=(kdim // tk,),
        in_specs=in_specs, out_specs=pl.BlockSpec((rows, c), lambda k: (0, 0)),
        out_shape=_sds((rows, c), F32), scratch_shapes=[pltpu.VMEM((rows, c), F32)],
        compiler_params=_params("arbitrary"), name=name,
    )(x, w, *aux)


def _fox_gate_kernel(t_ref, crow_ref, ccol_ref):
    x = t_ref[0]
    c = x.T[0:FOX_HEADS, :]
    t_len = c.shape[1]
    lane = lax.broadcasted_iota(jnp.int32, c.shape, 1)
    s = 1
    while s < t_len:
        c = c + jnp.where(lane >= s, pltpu.roll(c, s, axis=1), 0.0)
        s *= 2
    c = c * LOG2E
    crow_ref[0] = c
    cpad = jnp.concatenate([c, jnp.zeros((LANES - FOX_HEADS, t_len), F32)], axis=0)
    ccol_ref[0] = cpad.T


def fox_gate_cumsum(tail, b, t):
    return pl.pallas_call(
        _fox_gate_kernel, grid=(b,),
        in_specs=[pl.BlockSpec((1, t, TAIL_W), lambda i: (i, 0, 0))],
        out_specs=[pl.BlockSpec((1, FOX_HEADS, t), lambda i: (i, 0, 0)),
                   pl.BlockSpec((1, t, TAIL_W), lambda i: (i, 0, 0))],
        out_shape=[_sds((b, FOX_HEADS, t), F32), _sds((b, t, TAIL_W), F32)],
        compiler_params=_params("parallel"), name="fox_gate_cumsum",
    )(tail.reshape(b, t, TAIL_W))


def _lane_tile(x, n):
    return x if n == 1 else jnp.concatenate([x] * n, axis=1)


def _fox_attn_kernel(qi_tab, ki_tab, q_ref, k_ref, v_ref, cq_ref, ck_ref, g_ref, o_ref, m_s, l_s, acc_s, cq_s,
                     *, tq, tk):
    p_id = pl.program_id(1)
    qi, ki = qi_tab[p_id], ki_tab[p_id]
    nrep = tk // LANES

    @pl.when(ki == 0)
    def _():
        m_s[...] = jnp.full(m_s.shape, NEG, F32)
        l_s[...] = jnp.zeros(l_s.shape, F32)
        acc_s[...] = jnp.zeros(acc_s.shape, F32)
        for h in range(FOX_HEADS):
            cq_s[h] = jnp.broadcast_to(cq_ref[0, :, h:h + 1], (tq, LANES))

    def step(masked):
        if masked:
            row = lax.broadcasted_iota(jnp.int32, (tq, tk), 0)
            col = lax.broadcasted_iota(jnp.int32, (tq, tk), 1)
            causal = col <= row
        for h in range(FOX_HEADS):
            sl = slice(h * FOX_HD, (h + 1) * FOX_HD)
            s = lax.dot_general(q_ref[0, :, sl], k_ref[0, :, sl], _NT, preferred_element_type=F32)
            s = (s + _lane_tile(cq_s[h], nrep)) - ck_ref[0, h:h + 1, :]
            if masked:
                s = jnp.where(causal, s, NEG)
            m_old = m_s[h]
            m_new = jnp.maximum(m_old, jnp.broadcast_to(jnp.max(s, axis=-1, keepdims=True), (tq, LANES)))
            alpha = jnp.exp2(m_old - m_new)
            p = jnp.exp2(s - _lane_tile(m_new, nrep))
            l_s[h] = l_s[h] * alpha + jnp.broadcast_to(jnp.sum(p, axis=-1, keepdims=True), (tq, LANES))
            m_s[h] = m_new
            acc_s[:, sl] = acc_s[:, sl] * alpha + jnp.dot(p.astype(BF16), v_ref[0, :, sl],
                                                          preferred_element_type=F32)

    @pl.when(ki < qi)
    def _():
        step(False)

    @pl.when(ki == qi)
    def _():
        step(True)
        ssq = jnp.zeros((tq, 1), F32)
        for h in range(FOX_HEADS):
            sl = slice(h * FOX_HD, (h + 1) * FOX_HD)
            o = acc_s[:, sl] / l_s[h]
            acc_s[:, sl] = o
            ssq = ssq + jnp.sum(o * o, axis=-1, keepdims=True)
        inv = lax.rsqrt(ssq / FOX_W + EPS)
        o_ref[0] = (acc_s[...] * inv * g_ref[...]).astype(o_ref.dtype)


def fox_prompt_attention(q, k, v, ccol, crow, g_out, b, t, tq):
    nq = t // tq
    pairs = [(i, j) for i in range(nq) for j in range(i + 1)]
    qi_tab = jnp.asarray(np.array([p[0] for p in pairs], np.int32))
    ki_tab = jnp.asarray(np.array([p[1] for p in pairs], np.int32))
    q3, k3, v3 = (a.reshape(b, t, FOX_W) for a in (q, k, v))
    grid_spec = pltpu.PrefetchScalarGridSpec(
        num_scalar_prefetch=2, grid=(b, len(pairs)),
        in_specs=[
            pl.BlockSpec((1, tq, FOX_W), lambda bi, p, qt, kt: (bi, qt[p], 0)),
            pl.BlockSpec((1, tq, FOX_W), lambda bi, p, qt, kt: (bi, kt[p], 0)),
            pl.BlockSpec((1, tq, FOX_W), lambda bi, p, qt, kt: (bi, kt[p], 0)),
            pl.BlockSpec((1, tq, TAIL_W), lambda bi, p, qt, kt: (bi, qt[p], 0)),
            pl.BlockSpec((1, FOX_HEADS, tq), lambda bi, p, qt, kt: (bi, 0, kt[p])),
            pl.BlockSpec((1, FOX_W), lambda bi, p, qt, kt: (0, 0)),
        ],
        out_specs=pl.BlockSpec((1, tq, FOX_W), lambda bi, p, qt, kt: (bi, qt[p], 0)),
        scratch_shapes=[pltpu.VMEM((FOX_HEADS, tq, LANES), F32), pltpu.VMEM((FOX_HEADS, tq, LANES), F32),
                        pltpu.VMEM((tq, FOX_W), F32), pltpu.VMEM((FOX_HEADS, tq, LANES), F32)],
    )
    out = pl.pallas_call(
        functools.partial(_fox_attn_kernel, tq=tq, tk=tq), grid_spec=grid_spec,
        out_shape=_sds((b, t, FOX_W), BF16),
        compiler_params=_params("parallel", "arbitrary"), name="fox_prompt_attention",
    )(qi_tab, ki_tab, q3, k3, v3, ccol, crow, g_out.reshape(1, FOX_W))
    return out.reshape(b * t, FOX_W)


def _fox_decode_kernel(pt_ref, q_ref, kn_ref, vn_ref, fn_ref, g_ref, *refs, pages_per_step, scale):
    pp = pages_per_step
    k_refs, v_refs, f_refs = refs[0:pp], refs[pp:2 * pp], refs[2 * pp:3 * pp]
    o_ref = refs[3 * pp]
    m_s, l_s, acc_s, r_s = refs[3 * pp + 1:]
    step_id = pl.program_id(1)
    q = q_ref[0] * scale

    @pl.when(step_id == 0)
    def _():
        m_s[...] = jnp.sum(q * kn_ref[0], axis=-1, keepdims=True)
        l_s[...] = jnp.ones(l_s.shape, F32)
        acc_s[...] = vn_ref[0]
        r_s[...] = fn_ref[0]

    tok_r = lax.broadcasted_iota(jnp.int32, (PAGE_SIZE, PAGE_SIZE), 0)
    tok_c = lax.broadcasted_iota(jnp.int32, (PAGE_SIZE, PAGE_SIZE), 1)
    later = (tok_r > tok_c).astype(F32)
    r_run = r_s[...]
    parts = []
    grp = DECODE_GROUP
    tok = lax.broadcasted_iota(jnp.int32, (grp, 1, FOX_HD), 0)
    lane = lax.broadcasted_iota(jnp.int32, (grp, 1, FOX_HD), 2)
    for j in range(pp):
        f = f_refs[j][...]
        r = jnp.dot(f, later, preferred_element_type=F32, precision=lax.Precision.HIGHEST) + r_run
        r_run = r_run + jnp.sum(f, axis=-1, keepdims=True)
        for g in range(PAGE_SIZE // grp):
            rows = pl.ds(g * grp, grp)
            prod = k_refs[j][rows] * q[None] + jnp.where(lane == tok + g * grp, r[None], 0.0)
            s = jnp.sum(prod, axis=-1, keepdims=True)
            m_g = jnp.max(s, axis=0)
            p = jnp.exp(s - m_g[None])
            parts.append((m_g, jnp.sum(p, axis=0), jnp.sum(p * v_refs[j][rows], axis=0)))
    m_old = m_s[...]
    m_new = m_old
    for m_j, _, _ in parts:
        m_new = jnp.maximum(m_new, m_j)
    alpha = jnp.exp(m_old - m_new)
    l_run, acc = l_s[...] * alpha, acc_s[...] * alpha
    for m_j, l_j, pv_j in parts:
        a_j = jnp.exp(m_j - m_new)
        l_run, acc = l_run + l_j * a_j, acc + pv_j * a_j
    m_s[...], l_s[...], acc_s[...], r_s[...] = m_new, l_run, acc, r_run

    @pl.when(step_id == pl.num_programs(1) - 1)
    def _():
        o = acc_s[...] / l_s[...]
        ssq = jnp.sum(jnp.sum(o * o, axis=-1, keepdims=True), axis=0, keepdims=True)
        o_ref[0] = (o * lax.rsqrt(ssq / FOX_W + EPS) * g_ref[...]).astype(o_ref.dtype)


def fox_sample_attention(layer, q, k_new, v_new, logf_new, cache_k, cache_v, cache_ft, page_table, g_out,
                         pages_per_step):
    db, n_pages = page_table.shape
    n_steps = n_pages // pages_per_step

    def page_map(j):
        def im(bi, si, pt):
            return (layer, pt[bi, n_pages - 1 - (si * pages_per_step + j)], 0, 0, 0)
        return im

    def page_map_f(j):
        def im(bi, si, pt):
            return (layer, pt[bi, n_pages - 1 - (si * pages_per_step + j)], 0, 0)
        return im

    vec = pl.BlockSpec((1, FOX_HEADS, FOX_HD), lambda bi, si, pt: (bi, 0, 0))
    in_specs = [vec, vec, vec,
                pl.BlockSpec((1, FOX_HEADS, 1), lambda bi, si, pt: (bi, 0, 0)),
                pl.BlockSpec((FOX_HEADS, FOX_HD), lambda bi, si, pt: (0, 0))]
    in_specs += [pl.BlockSpec((None, None, PAGE_SIZE, FOX_HEADS, FOX_HD), page_map(j)) for j in range(pages_per_step)]
    in_specs += [pl.BlockSpec((None, None, PAGE_SIZE, FOX_HEADS, FOX_HD), page_map(j)) for j in range(pages_per_step)]
    in_specs += [pl.BlockSpec((None, None, FOX_HEADS, PAGE_SIZE), page_map_f(j)) for j in range(pages_per_step)]
    grid_spec = pltpu.PrefetchScalarGridSpec(
        num_scalar_prefetch=1, grid=(db, n_steps), in_specs=in_specs,
        out_specs=pl.BlockSpec((1, FOX_HEADS, FOX_HD), lambda bi, si, pt: (bi, 0, 0)),
        scratch_shapes=[pltpu.VMEM((FOX_HEADS, 1), F32), pltpu.VMEM((FOX_HEADS, 1), F32),
                        pltpu.VMEM((FOX_HEADS, FOX_HD), F32), pltpu.VMEM((FOX_HEADS, 1), F32)],
    )
    return pl.pallas_call(
        functools.partial(_fox_decode_kernel, pages_per_step=pages_per_step, scale=FOX_HD ** -0.5),
        grid_spec=grid_spec, out_shape=_sds((db, FOX_HEADS, FOX_HD), F32),
        compiler_params=_params("parallel", "arbitrary"), name="fox_sample_attention",
    )(page_table, q, k_new, v_new, logf_new, g_out.reshape(FOX_HEADS, FOX_HD),
      *([cache_k] * pages_per_step), *([cache_v] * pages_per_step), *([cache_ft] * pages_per_step))


def _s5_param_kernel(are_ref, aim_ref, ldt_ref, bre_ref, bim_ref, lre_ref, lim_ref, bbre_ref, bbim_ref):
    a_re, a_im = are_ref[...], aim_ref[...]
    dt = jnp.exp(ldt_ref[...])
    mag = jnp.exp(a_re * dt)
    lam_re, lam_im = mag * jnp.cos(a_im * dt), mag * jnp.sin(a_im * dt)
    den = a_re * a_re + a_im * a_im
    z_re = ((lam_re - 1.0) * a_re + lam_im * a_im) / den
    z_im = (lam_im * a_re - (lam_re - 1.0) * a_im) / den
    lre_ref[...] = lam_re
    lim_ref[...] = lam_im
    b_re, b_im = bre_ref[...], bim_ref[...]
    bbre_ref[...] = z_re[:, None, :] * b_re - z_im[:, None, :] * b_im
    bbim_ref[...] = z_re[:, None, :] * b_im + z_im[:, None, :] * b_re


def s5_discretise(a_re, a_im, log_dt, b_re_t, b_im_t):
    g, p = a_re.shape
    c = b_re_t.shape[1]
    return pl.pallas_call(
        _s5_param_kernel,
        out_shape=[_sds((g, p), F32), _sds((g, p), F32), _sds((g, c, p), F32), _sds((g, c, p), F32)],
        name="s5_discretise",
    )(a_re, a_im, log_dt.reshape(g, 1), b_re_t, b_im_t)


def _block_diag(w, groups_per_block):
    g, a, b = w.shape
    nb = g // groups_per_block
    eye = jnp.eye(groups_per_block, dtype=w.dtype)
    w5 = w.reshape(nb, groups_per_block, a, b)
    return jnp.einsum("jgab,gh->jgahb", w5, eye).reshape(nb, groups_per_block * a, groups_per_block * b)


def _row_tile(x, n):
    tile = jnp.broadcast_to(x, (S5_SCAN_BLOCK, x.shape[1]))
    return tile if n == S5_SCAN_BLOCK else jnp.concatenate([tile] * (n // S5_SCAN_BLOCK), axis=0)


def _s5_kernel(u_ref, h0re_ref, h0im_ref, lre_ref, lim_ref, wb_ref, wc_ref, d_ref, wglu_ref, bglu_ref, gout_ref,
               y_ref, hre_out, him_out, hre_s, him_s, cre_s, cim_s, y_s, *, rows, scan, precise):
    t_id = pl.program_id(1)
    lam_re, lam_im = lre_ref[...], lim_ref[...]

    @pl.when(t_id == 0)
    def _():
        cre_s[...] = h0re_ref[0]
        cim_s[...] = h0im_ref[0]

    u = u_ref[0]
    for j in range(S5_BLOCKS):
        bu = _mxu(u[:, j * S5_BC:(j + 1) * S5_BC], wb_ref[j], precise)
        hre_s[:, j * S5_BS:(j + 1) * S5_BS] = bu[:, :S5_BS]
        him_s[:, j * S5_BS:(j + 1) * S5_BS] = bu[:, S5_BS:]
    c_re, c_im = cre_s[...], cim_s[...]
    if scan:
        hre_s[0:1, :] = hre_s[0:1, :] + (lam_re * c_re - lam_im * c_im)
        him_s[0:1, :] = him_s[0:1, :] + (lam_re * c_im + lam_im * c_re)
        seg = S5_SCAN_BLOCK
        steps = rows // seg
        l_re, l_im = _row_tile(lam_re, seg), _row_tile(lam_im, seg)
        p_re, p_im = hre_s[0:seg, :], him_s[0:seg, :]
        for v in range(1, steps):
            sl = slice(v * seg, (v + 1) * seg)
            p_re, p_im = (hre_s[sl, :] + (l_re * p_re - l_im * p_im), him_s[sl, :] + (l_re * p_im + l_im * p_re))
            hre_s[sl, :] = p_re
            him_s[sl, :] = p_im
        f_re, f_im = lam_re, lam_im
        n = 1
        while n < steps:
            f_re, f_im = f_re * f_re - f_im * f_im, 2.0 * f_re * f_im
            n *= 2
        f_re, f_im = _row_tile(f_re, seg), _row_tile(f_im, seg)
        pos = lax.broadcasted_iota(jnp.int32, (seg, S5_NS), 0)
        e_re, e_im = p_re, p_im
        s = 1
        while s < seg:
            m_re, m_im = jnp.where(pos >= s, f_re, 0.0), jnp.where(pos >= s, f_im, 0.0)
            b_re, b_im = pltpu.roll(e_re, s, axis=0), pltpu.roll(e_im, s, axis=0)
            e_re, e_im = e_re + (m_re * b_re - m_im * b_im), e_im + (m_re * b_im + m_im * b_re)
            f_re, f_im = f_re * f_re - f_im * f_im, 2.0 * f_re * f_im
            s *= 2
        c_re = jnp.where(pos >= 1, pltpu.roll(e_re, 1, axis=0), 0.0)
        c_im = jnp.where(pos >= 1, pltpu.roll(e_im, 1, axis=0), 0.0)
        for v in range(steps):
            c_re, c_im = l_re * c_re - l_im * c_im, l_re * c_im + l_im * c_re
            sl = slice(v * seg, (v + 1) * seg)
            hre_s[sl, :] = hre_s[sl, :] + c_re
            him_s[sl, :] = him_s[sl, :] + c_im
        cre_s[...] = hre_s[rows - 1:rows, :]
        cim_s[...] = him_s[rows - 1:rows, :]
        hre_out[0] = hre_s[rows - 1:rows, :]
        him_out[0] = him_s[rows - 1:rows, :]
    else:
        hre_s[...] = hre_s[...] + (lam_re * c_re - lam_im * c_im)
        him_s[...] = him_s[...] + (lam_re * c_im + lam_im * c_re)
        hre_out[0] = hre_s[...]
        him_out[0] = him_s[...]

    for j in range(S5_BLOCKS):
        sl = slice(j * S5_BS, (j + 1) * S5_BS)
        yj = _mxu(hre_s[:, sl], wc_ref[j, 0:S5_BS, :], precise)
        yj = yj + _mxu(him_s[:, sl], wc_ref[j, S5_BS:2 * S5_BS, :], precise)
        y_s[:, j * S5_BC:(j + 1) * S5_BC] = yj
    y = y_s[...] + d_ref[...] * u
    y = jax.nn.gelu(y)
    z = _mxu(y, wglu_ref[...], precise) + bglu_ref[...]
    y = y * jax.nn.sigmoid(z)
    y = y * lax.rsqrt(jnp.mean(y * y, axis=-1, keepdims=True) + EPS) * gout_ref[...]
    y_ref[0] = y.astype(y_ref.dtype)


def s5_mixer(u, h0_re, h0_im, lam_re, lam_im, wb, wc, d_skip, w_glu, b_glu, g_out, *, nseq, rows, nsteps, scan,
             col_block, precise=False):
    st_rows = 1 if scan else rows
    full = lambda shape: pl.BlockSpec(shape, lambda b, t: tuple(0 for _ in shape))
    state_spec = pl.BlockSpec((1, st_rows, S5_NS), lambda b, t: (b, 0, 0))
    y, hre, him = pl.pallas_call(
        functools.partial(_s5_kernel, rows=rows, scan=scan, precise=precise), grid=(nseq, nsteps),
        in_specs=[pl.BlockSpec((1, rows, S5_W), lambda b, t: (b, t, col_block)),
                  state_spec, state_spec, full((1, S5_NS)), full((1, S5_NS)),
                  full((S5_BLOCKS, S5_BC, 2 * S5_BS)), full((S5_BLOCKS, 2 * S5_BS, S5_BC)),
                  full((1, S5_W)), full((S5_W, S5_W)), full((1, S5_W)), full((1, S5_W))],
        out_specs=[pl.BlockSpec((1, rows, S5_W), lambda b, t: (b, t, 0)), state_spec, state_spec],
        out_shape=[_sds((nseq, nsteps * rows, S5_W), F32 if precise else BF16),
                   _sds((nseq, st_rows, S5_NS), F32), _sds((nseq, st_rows, S5_NS), F32)],
        scratch_shapes=[pltpu.VMEM((rows, S5_NS), F32), pltpu.VMEM((rows, S5_NS), F32),
                        pltpu.VMEM((st_rows, S5_NS), F32), pltpu.VMEM((st_rows, S5_NS), F32),
                        pltpu.VMEM((rows, S5_W), F32)],
        compiler_params=_params("parallel", "arbitrary"), name="s5_mixer",
    )(u, h0_re, h0_im, lam_re, lam_im, wb, wc, d_skip.reshape(1, S5_W), w_glu, b_glu.reshape(1, S5_W),
      g_out.reshape(1, S5_W))
    return y, hre, him


def _gla_kernel(qk_ref, v_ref, go_ref, ga_ref, wa_ref, ba_ref, gain_ref, s0_ref, o_ref, st_out,
                st_s, o_s, *, rows, t_valid, precise):
    t_id = pl.program_id(1)

    @pl.when(t_id == 0)
    def _():
        st_s[...] = s0_ref[0]

    x = _mxu(ga_ref[0], wa_ref[...], precise) + ba_ref[...]
    la = jax.nn.log_sigmoid(x) / GLA_TAU
    row = lax.broadcasted_iota(jnp.int32, la.shape, 0)
    if t_valid < rows:
        la = jnp.where(row < t_valid, la, 0.0)
    pos = row & (GLA_CHUNK - 1)
    b, e = la, la
    s = 1
    while s < GLA_CHUNK:
        b = b + jnp.where(pos >= s, pltpu.roll(b, s, axis=0), 0.0)
        e = e + jnp.where(pos < GLA_CHUNK - s, pltpu.roll(e, rows - s, axis=0), 0.0)
        s *= 2
    b_last = b + e - la
    qk = qk_ref[0]
    q, k = qk[:, 0:GLA_QK] * (GLA_DK ** -0.5), qk[:, GLA_QK:2 * GLA_QK]
    qe = q * jnp.exp(b)
    ke = k * jnp.exp(-b)
    kd = k * jnp.exp(b_last - b)
    dec = jnp.exp(b_last)
    v = v_ref[0]
    if not precise:
        qe, ke, kd, v = qe.astype(BF16), ke.astype(BF16), kd.astype(BF16), v.astype(BF16)
    r_i = lax.broadcasted_iota(jnp.int32, (rows, rows), 0)
    c_i = lax.broadcasted_iota(jnp.int32, (rows, rows), 1)
    chunk_shift = GLA_CHUNK.bit_length() - 1
    causal = ((r_i >> chunk_shift) == (c_i >> chunk_shift)) & (c_i <= r_i)
    for h in range(GLA_HEADS):
        ks = slice(h * GLA_DK, (h + 1) * GLA_DK)
        vs = slice(h * GLA_DV, (h + 1) * GLA_DV)
        att = _mxu(qe[:, ks], ke[:, ks], precise, _NT)
        att = jnp.where(causal, att, 0.0)
        o_s[:, vs] = _mxu(att, v[:, vs], precise)

    state = [st_s[h] for h in range(GLA_HEADS)]
    for j in range(rows // GLA_CHUNK):
        rs = slice(j * GLA_CHUNK, (j + 1) * GLA_CHUNK)
        for h in range(GLA_HEADS):
            ks = slice(h * GLA_DK, (h + 1) * GLA_DK)
            vs = slice(h * GLA_DV, (h + 1) * GLA_DV)
            o_s[rs, vs] = o_s[rs, vs] + _mxu(qe[rs, ks], state[h], precise, _NT)
            upd = _mxu(v[rs, vs], kd[rs, ks], precise, _TN)
            state[h] = state[h] * dec[j * GLA_CHUNK:j * GLA_CHUNK + 1, ks] + upd
    for h in range(GLA_HEADS):
        st_s[h] = state[h]
    st_out[0] = st_s[...]

    go = go_ref[0]
    for h in range(GLA_HEADS):
        vs = slice(h * GLA_DV, (h + 1) * GLA_DV)
        o = o_s[:, vs]
        y = o * lax.rsqrt(jnp.mean(o * o, axis=-1, keepdims=True) + EPS) * gain_ref[...]
        o_ref[0, :, vs] = (y * jax.nn.silu(go[:, vs])).astype(o_ref.dtype)


def gla_mixer(rest, tail, wa_pad, b_a, gain, s0_t, *, nseq, rows, nsteps, t_valid, precise=False):
    full = lambda shape: pl.BlockSpec(shape, lambda b, t: tuple(0 for _ in shape))
    st_spec = pl.BlockSpec((1, GLA_HEADS, GLA_DV, GLA_DK), lambda b, t: (b, 0, 0, 0))
    o, st = pl.pallas_call(
        functools.partial(_gla_kernel, rows=rows, t_valid=t_valid, precise=precise), grid=(nseq, nsteps),
        in_specs=[pl.BlockSpec((1, rows, 2 * GLA_QK), lambda b, t: (b, t, 1)),
                  pl.BlockSpec((1, rows, GLA_W), lambda b, t: (b, t, 2)),
                  pl.BlockSpec((1, rows, GLA_W), lambda b, t: (b, t, 3)),
                  pl.BlockSpec((1, rows, TAIL_W), lambda b, t: (b, t, 0)),
                  full((TAIL_W, GLA_QK)), full((1, GLA_QK)), full((1, GLA_DV)), st_spec],
        out_specs=[pl.BlockSpec((1, rows, GLA_W), lambda b, t: (b, t, 0)), st_spec],
        out_shape=[_sds((nseq, nsteps * rows, GLA_W), F32 if precise else BF16),
                   _sds((nseq, GLA_HEADS, GLA_DV, GLA_DK), F32)],
        scratch_shapes=[pltpu.VMEM((GLA_HEADS, GLA_DV, GLA_DK), F32), pltpu.VMEM((rows, GLA_W), F32)],
        compiler_params=_params("parallel", "arbitrary"), name="gla_mixer",
    )(rest, rest, rest, tail, wa_pad, b_a.reshape(1, GLA_QK), gain.reshape(1, GLA_DV), s0_t)
    return o, st


def _mem_attn_kernel(q_ref, k_ref, v_ref, o_ref, *, precise):
    for h in range(MEM_HEADS):
        sl = slice(h * MEM_HD, (h + 1) * MEM_HD)
        s = _mxu(q_ref[0, :, sl], k_ref[0, :, sl], precise, _NT)
        p = jnp.exp(s - jnp.max(s, axis=-1, keepdims=True))
        o = _mxu(p, v_ref[0, :, sl], precise)
        o_ref[0, :, sl] = (o / jnp.sum(p, axis=-1, keepdims=True)).astype(o_ref.dtype)


def memory_attention(q, mk, mv, tq, precise=False):
    b, t, _ = q.shape
    return pl.pallas_call(
        functools.partial(_mem_attn_kernel, precise=precise), grid=(b, t // tq),
        in_specs=[pl.BlockSpec((1, tq, MEM_W), lambda bi, i: (bi, i, 0)),
                  pl.BlockSpec((1, MEM_TOKENS, MEM_W), lambda bi, i: (bi, 0, 0)),
                  pl.BlockSpec((1, MEM_TOKENS, MEM_W), lambda bi, i: (bi, 0, 0))],
        out_specs=pl.BlockSpec((1, tq, MEM_W), lambda bi, i: (bi, i, 0)),
        out_shape=_sds((b, t, MEM_W), F32 if precise else BF16), compiler_params=_params("parallel", "parallel"),
        name="memory_attention",
    )(q, mk, mv)


def _router_gate(logits):
    lane = lax.broadcasted_iota(jnp.int32, logits.shape, 1)
    is_group = (lane >= N_EXPERTS) & (lane < N_EXPERTS + N_GROUPS)
    lg = jnp.where(is_group, logits, NEG)
    eg = jnp.where(is_group, jnp.exp(lg - jnp.max(lg, axis=-1, keepdims=True)), 0.0)
    pg = eg / jnp.sum(eg, axis=-1, keepdims=True)
    g_w = jnp.max(pg, axis=-1, keepdims=True)
    lane_f = lane.astype(F32)
    far = float(4 * LANES)
    g_lane = jnp.min(jnp.where(is_group & (pg == g_w), lane_f, far), axis=-1, keepdims=True)
    group_of_lane = (lane >> (EXP_PER_GROUP.bit_length() - 1)).astype(F32)
    sel = (lane < N_EXPERTS) & (group_of_lane == g_lane - float(N_EXPERTS))
    le = jnp.where(sel, logits, NEG)
    ee = jnp.where(sel, jnp.exp(le - jnp.max(le, axis=-1, keepdims=True)), 0.0)
    pe = ee / jnp.sum(ee, axis=-1, keepdims=True)
    m1 = jnp.max(jnp.where(sel, pe, -1.0), axis=-1, keepdims=True)
    i1 = jnp.min(jnp.where(sel & (pe == m1), lane_f, far), axis=-1, keepdims=True)
    rest = sel & (lane_f != i1)
    m2 = jnp.max(jnp.where(rest, pe, -1.0), axis=-1, keepdims=True)
    i2 = jnp.min(jnp.where(rest & (pe == m2), lane_f, far), axis=-1, keepdims=True)
    tot = m1 + m2
    w = jnp.where(lane_f == i1, m1 / tot, jnp.where(lane_f == i2, m2 / tot, 0.0))
    return w * g_w


def _router_kernel(x_ref, g_ref, wh_ref, wl_ref, b_ref, gate_ref):
    x = x_ref[...]
    xn = x * lax.rsqrt(jnp.mean(x * x, axis=-1, keepdims=True) + EPS) * g_ref[...]
    xh = xn.astype(BF16)
    xl = (xn - xh.astype(F32)).astype(BF16)
    wh = wh_ref[...]
    logits = (jnp.dot(xh, wh, preferred_element_type=F32) + jnp.dot(xl, wh, preferred_element_type=F32)
              + jnp.dot(xh, wl_ref[...], preferred_element_type=F32))
    gate_ref[...] = _router_gate(logits + b_ref[...])


def moe_router(x, g, w_r, b_r, tm):
    n, d = x.shape
    w_hi = w_r.astype(BF16)
    w_lo = (w_r - w_hi.astype(F32)).astype(BF16)
    w_spec = pl.BlockSpec((d, LANES), lambda i: (0, 0))
    return pl.pallas_call(
        _router_kernel, grid=(n // tm,),
        in_specs=[pl.BlockSpec((tm, d), lambda i: (i, 0)), pl.BlockSpec((1, d), lambda i: (0, 0)),
                  w_spec, w_spec, pl.BlockSpec((1, LANES), lambda i: (0, 0))],
        out_specs=pl.BlockSpec((tm, LANES), lambda i: (i, 0)),
        out_shape=_sds((n, LANES), F32), compiler_params=_params("parallel"), name="moe_router",
    )(x, g.reshape(1, d), w_hi, w_lo, b_r)


def _expert_kernel(x_ref, gate_ref, wg_ref, wu_ref, wd_ref, res_ref, *rest, precise, next_norm):
    if next_norm:
        g_ref, o_ref, h_ref, acc_s = rest
    else:
        o_ref, acc_s = rest
    e = pl.program_id(1)

    @pl.when(e == 0)
    def _():
        acc_s[...] = jnp.zeros(acc_s.shape, F32)

    x = x_ref[...]
    a = jax.nn.silu(_mxu(x, wg_ref[...], precise)) * _mxu(x, wu_ref[...], precise)
    gate = gate_ref[...]
    lane = lax.broadcasted_iota(jnp.int32, gate.shape, 1)
    g_col = jnp.sum(jnp.where(lane == e, gate, 0.0), axis=-1, keepdims=True)
    acc_s[...] += _mxu(a * g_col, wd_ref[...], precise)

    @pl.when(e == pl.num_programs(1) - 1)
    def _():
        x_new = res_ref[...] + acc_s[...]
        o_ref[...] = x_new
        if next_norm:
            y = x_new * lax.rsqrt(jnp.mean(x_new * x_new, axis=-1, keepdims=True) + EPS)
            h_ref[...] = (y * g_ref[...]).astype(h_ref.dtype)


def moe_experts(x, gate, w_gate, w_up, w_down, layer, res, tm, precise=False, g_next=None):
    n, d = x.shape
    row = pl.BlockSpec((tm, d), lambda i, e: (i, 0))
    in_specs = [row, pl.BlockSpec((tm, LANES), lambda i, e: (i, 0)),
                pl.BlockSpec((None, None, d, D_EXPERT), lambda i, e: (layer, e, 0, 0)),
                pl.BlockSpec((None, None, d, D_EXPERT), lambda i, e: (layer, e, 0, 0)),
                pl.BlockSpec((None, None, D_EXPERT, d), lambda i, e: (layer, e, 0, 0)), row]
    args = [x, gate, w_gate, w_up, w_down, res]
    out_specs, out_shape = row, _sds((n, d), F32)
    if g_next is not None:
        in_specs.append(pl.BlockSpec((1, d), lambda i, e: (0, 0)))
        args.append(g_next.reshape(1, d))
        out_specs, out_shape = [row, row], [_sds((n, d), F32), _sds((n, d), BF16)]
    return pl.pallas_call(
        functools.partial(_expert_kernel, precise=precise, next_norm=g_next is not None), grid=(n // tm, N_EXPERTS),
        in_specs=in_specs, out_specs=out_specs, out_shape=out_shape, scratch_shapes=[pltpu.VMEM((tm, d), F32)],
        compiler_params=_params("parallel", "arbitrary"), name="moe_experts",
    )(*args)


_BF16_STACKS = ("w_in", "w_out", "mem_w_q", "mem_w_k", "mem_w_v", "mem_w_o", "moe_w_gate", "moe_w_up", "moe_w_down")


def _layer_weights(l, p, pb):
    w_in = pb["w_in"][l]
    c = np.cumsum([0, FOX_W, FOX_W, FOX_W, FOX_HEADS, S5_W, GLA_QK, GLA_QK, GLA_W, GLA_RANK, GLA_W]).tolist()
    col = lambda i: w_in[:, c[i]:c[i + 1]]
    w = {}
    w["wrest"] = jnp.concatenate([col(4), col(5), col(6), col(7), col(9)], axis=1)
    w["wtail"] = jnp.concatenate(
        [col(3), col(8), jnp.zeros((D_MODEL, TAIL_W - FOX_HEADS - GLA_RANK), BF16)], axis=1)
    w["tail_bias"] = jnp.concatenate([p["fox_f_bias"][l], jnp.zeros((TAIL_W - FOX_HEADS,), F32)])
    w["q_gain"] = jnp.tile(p["fox_q_gain"][l], FOX_HEADS)
    w["k_gain"] = jnp.tile(p["fox_k_gain"][l], FOX_HEADS)
    lam_re, lam_im, bb_re, bb_im = s5_discretise(
        p["s5_a_re"][l], p["s5_a_im"][l], p["s5_log_dt"][l],
        jnp.swapaxes(p["s5_b_re"][l], 1, 2), jnp.swapaxes(p["s5_b_im"][l], 1, 2))
    gpb = S5_GROUPS // S5_BLOCKS
    w["s5_lam_re"], w["s5_lam_im"] = lam_re.reshape(1, S5_NS), lam_im.reshape(1, S5_NS)
    w["s5_wb_f32"] = jnp.concatenate([_block_diag(bb_re, gpb), _block_diag(bb_im, gpb)], axis=2)
    w["s5_wb"] = w["s5_wb_f32"].astype(BF16)
    c_re_t = jnp.swapaxes(p["s5_c_re"][l], 1, 2)
    c_im_t = jnp.swapaxes(p["s5_c_im"][l], 1, 2)
    w["s5_wc_f32"] = jnp.concatenate([_block_diag(c_re_t, gpb), -_block_diag(c_im_t, gpb)], axis=1)
    w["s5_wc"] = w["s5_wc_f32"].astype(BF16)
    w["s5_wglu"] = p["s5_w_glu"][l].astype(BF16)
    w["gla_wa_f32"] = jnp.concatenate(
        [jnp.zeros((FOX_HEADS, GLA_QK), F32), p["gla_w_a2"][l],
         jnp.zeros((TAIL_W - FOX_HEADS - GLA_RANK, GLA_QK), F32)], axis=0)
    w["gla_wa"] = w["gla_wa_f32"].astype(BF16)
    w["mem_q_gain"] = jnp.tile(p["mem_q_gain"][l], MEM_HEADS)
    w["mem_k_gain"] = jnp.tile(p["mem_k_gain"][l], MEM_HEADS)
    w["w_router"] = jnp.concatenate(
        [p["moe_w_expert"][l], p["moe_w_group"][l],
         jnp.zeros((D_MODEL, LANES - N_EXPERTS - N_GROUPS), F32)], axis=1)
    w["b_router"] = jnp.concatenate(
        [p["moe_b_expert"][l], p["moe_b_group"][l], jnp.zeros((LANES - N_EXPERTS - N_GROUPS,), F32)]).reshape(1, LANES)
    return w


def _in_projection(h, w, pb, l, tm):
    tn = PROJ_TN
    (q_b,) = matmul(h, pb["w_in"], tm=tm, tn=tn, out_dtypes=(BF16,), gain=w["q_gain"],
                    post_scale=FOX_HD ** -0.5 * LOG2E, layer=l, cols=(0, FOX_W), name="proj_q")
    k_f, k_b = matmul(h, pb["w_in"], tm=tm, tn=tn, out_dtypes=(F32, BF16), gain=w["k_gain"], layer=l,
                      cols=(FOX_W, FOX_W), name="proj_k")
    v_f, v_b = matmul(h, pb["w_in"], tm=tm, tn=tn, out_dtypes=(F32, BF16), layer=l, cols=(2 * FOX_W, FOX_W),
                      name="proj_v")
    (rest,) = matmul(h, w["wrest"], tm=tm, tn=tn, name="proj_rest")
    (tail,) = matmul(h, w["wtail"], tm=tm, tn=TAIL_W, logsig_bias=w["tail_bias"], logsig_lanes=FOX_HEADS,
                     name="proj_tail")
    return q_b, k_f, k_b, v_f, v_b, rest, tail


def _post_mixer(x, branches, mk, mv, w, p, pb, l, b, t, tm, tq_mem, g_next):
    n = x.shape[0]
    tm_full = min(256, n)
    x, hq = proj_res_norm(branches, pb["w_out"], l, x, p["g_mem_pre"][l], tm_full, name="proj_out")
    (q,) = matmul(hq, pb["mem_w_q"], tm=tm, tn=512, out_dtypes=(BF16,), gain=w["mem_q_gain"],
                  post_scale=MEM_HD ** -0.5, layer=l, name="mem_q")
    o = memory_attention(q.reshape(b, t, MEM_W), mk, mv, tq_mem).reshape(n, MEM_W)
    x, hm = proj_res_norm([o], pb["mem_w_o"], l, x, p["g_moe"][l], tm_full, name="mem_o")
    gate = moe_router(x, p["g_moe"][l], w["w_router"], w["b_router"], tm)
    out = moe_experts(hm, gate, pb["moe_w_gate"], pb["moe_w_up"], pb["moe_w_down"], l, x, tm, g_next=g_next)
    return out if g_next is not None else (out, None)


def kernel(x_prompt, x_sample, cache_fox_k, cache_fox_v, cache_fox_logf, state_s5_re, state_s5_im, state_gla,
           cache_mem_k, cache_mem_v, page_table, mem_prompt, g_mix, w_in, fox_q_gain, fox_k_gain, fox_f_bias,
           s5_a_re, s5_a_im, s5_log_dt, s5_b_re, s5_b_im, s5_c_re, s5_c_im, s5_d, s5_w_glu, s5_b_glu,
           gla_w_a2, gla_b_a, gla_gain, g_fox_out, g_s5_out, w_out, g_mem_pre, g_mem_tok, mem_w_q, mem_w_k,
           mem_w_v, mem_q_gain, mem_k_gain, mem_w_o, g_moe, moe_w_group, moe_b_group, moe_w_expert,
           moe_b_expert, moe_w_gate, moe_w_up, moe_w_down):
    p = dict(g_mix=g_mix, w_in=w_in, fox_q_gain=fox_q_gain, fox_k_gain=fox_k_gain, fox_f_bias=fox_f_bias,
             s5_a_re=s5_a_re, s5_a_im=s5_a_im, s5_log_dt=s5_log_dt, s5_b_re=s5_b_re, s5_b_im=s5_b_im,
             s5_c_re=s5_c_re, s5_c_im=s5_c_im, s5_d=s5_d, s5_w_glu=s5_w_glu, s5_b_glu=s5_b_glu,
             gla_w_a2=gla_w_a2, gla_b_a=gla_b_a, gla_gain=gla_gain, g_fox_out=g_fox_out, g_s5_out=g_s5_out,
             w_out=w_out, g_mem_pre=g_mem_pre, g_mem_tok=g_mem_tok, mem_w_q=mem_w_q, mem_w_k=mem_w_k,
             mem_w_v=mem_w_v, mem_q_gain=mem_q_gain, mem_k_gain=mem_k_gain, mem_w_o=mem_w_o, g_moe=g_moe,
             moe_w_group=moe_w_group, moe_b_group=moe_b_group, moe_w_expert=moe_w_expert,
             moe_b_expert=moe_b_expert, moe_w_gate=moe_w_gate, moe_w_up=moe_w_up, moe_w_down=moe_w_down)
    depth = w_in.shape[0]
    b, t, d = x_prompt.shape
    db = x_sample.shape[0]
    n = b * t
    n_mem = b * MEM_TOKENS
    pb = {name: p[name].astype(BF16) for name in _BF16_STACKS}
    w_in_t = jnp.swapaxes(w_in, 1, 2)
    tm = min(512, n)
    tm_proj = min(1024, n)
    tq_fox = min(512, t)
    s5_rows = min(256, t)
    gla_rows = min(256, t)
    tq_mem = min(512, t)

    xp = x_prompt.reshape(n, d)
    xs = jnp.pad(x_sample.reshape(db, d), ((0, SAMPLE_ROWS - db), (0, 0)))
    mem2d = mem_prompt.reshape(n_mem, d)
    cache_ft = jnp.swapaxes(cache_fox_logf, 2, 3)
    zero_s5 = jnp.zeros((b, 1, S5_NS), F32)
    zero_gla = jnp.zeros((b, GLA_HEADS, GLA_DV, GLA_DK), F32)
    outs = {k: [] for k in ("pk", "pv", "pf", "sk", "sv", "sf", "p5r", "p5i", "s5r", "s5i", "pg", "sg", "pmk", "pmv")}

    h_prompt = rmsnorm_cast(xp, g_mix[0], tm)
    for l in range(depth):
        w = _layer_weights(l, p, pb)

        q_b, k_f, k_b, v_f, v_b, rest, tail = _in_projection(h_prompt, w, pb, l, tm_proj)
        crow, ccol = fox_gate_cumsum(tail, b, t)
        fox_o = fox_prompt_attention(q_b, k_b, v_b, ccol, crow, g_fox_out[l], b, t, tq_fox)
        rest3, tail3 = rest.reshape(b, t, 4 * S5_W), tail.reshape(b, t, TAIL_W)
        seg_steps = s5_rows // S5_SCAN_BLOCK
        su = rest3[:, :, :S5_W].reshape(b, t // s5_rows, S5_SCAN_BLOCK, seg_steps, S5_W)
        su = jnp.swapaxes(su, 2, 3).reshape(b, t, S5_W)
        s5_o, hre, him = s5_mixer(su, zero_s5, zero_s5, w["s5_lam_re"], w["s5_lam_im"], w["s5_wb"], w["s5_wc"],
                                  s5_d[l], w["s5_wglu"], s5_b_glu[l], g_s5_out[l],
                                  nseq=b, rows=s5_rows, nsteps=t // s5_rows, scan=True, col_block=0)
        s5_o = s5_o.reshape(b, t // s5_rows, seg_steps, S5_SCAN_BLOCK, S5_W)
        s5_o = jnp.swapaxes(s5_o, 2, 3).reshape(b, t, S5_W)
        gla_o, st = gla_mixer(rest3, tail3, w["gla_wa"], gla_b_a[l], gla_gain[l], zero_gla,
                              nseq=b, rows=gla_rows, nsteps=t // gla_rows, t_valid=gla_rows)
        branches = [fox_o, s5_o.reshape(n, S5_W), gla_o.reshape(n, GLA_W)]
        outs["pk"].append(k_f.reshape(b, t, FOX_HEADS, FOX_HD))
        outs["pv"].append(v_f.reshape(b, t, FOX_HEADS, FOX_HD))
        outs["pf"].append(tail3[:, :, :FOX_HEADS])
        outs["p5r"].append(hre.reshape(b, S5_GROUPS, S5_STATE))
        outs["p5i"].append(him.reshape(b, S5_GROUPS, S5_STATE))
        outs["pg"].append(jnp.swapaxes(st, 2, 3))
        hmem = rmsnorm_cast(mem2d, g_mem_tok[l], tm=min(512, n_mem))
        (mk,) = matmul(hmem, pb["mem_w_k"], tm=min(512, n_mem), tn=512, gain=w["mem_k_gain"], layer=l, name="mem_k")
        (mv,) = matmul(hmem, pb["mem_w_v"], tm=min(512, n_mem), tn=512, layer=l, name="mem_v")
        outs["pmk"].append(mk.reshape(b, MEM_TOKENS, MEM_HEADS, MEM_HD))
        outs["pmv"].append(mv.reshape(b, MEM_TOKENS, MEM_HEADS, MEM_HD))
        xp, h_prompt = _post_mixer(xp, branches, mk.reshape(b, MEM_TOKENS, MEM_W), mv.reshape(b, MEM_TOKENS, MEM_W),
                                   w, p, pb, l, b, t, tm, tq_mem, g_mix[l + 1] if l + 1 < depth else None)

        ns = SAMPLE_ROWS
        h = rmsnorm_cast(xs, g_mix[l], ns, out_dtype=F32)
        c0 = np.cumsum([0, FOX_W, FOX_W, FOX_W, FOX_HEADS, S5_W, GLA_QK, GLA_QK, GLA_W, GLA_RANK, GLA_W]).tolist()
        in_cols = c0[-1]
        qk_gain = jnp.concatenate([w["q_gain"], w["k_gain"]]).reshape(1, 2 * FOX_W)
        f_bias = jnp.concatenate([fox_f_bias[l], jnp.zeros((in_cols - 3 * FOX_W - FOX_HEADS,), F32)]).reshape(1, -1)
        proj = pmatmul(h, w_in_t, l, tk=256, epi="inproj", aux=(qk_gain, f_bias), w_cols_first=True,
                       name="proj_in_sample")
        col = lambda i: proj[:, c0[i]:c0[i + 1]]
        hd3 = lambda a: a[:db].reshape(db, FOX_HEADS, FOX_HD)
        q_f, k_f, v_f, logf_new = col(0), col(1), col(2), col(3)[:db]
        rest = jnp.concatenate([col(4), col(5), col(6), col(7), col(9)], axis=1)
        tail = jnp.concatenate([col(3), col(8), jnp.zeros((ns, TAIL_W - FOX_HEADS - GLA_RANK), F32)], axis=1)
        fox_o = fox_sample_attention(l, hd3(q_f), hd3(k_f), hd3(v_f), logf_new.reshape(db, FOX_HEADS, 1),
                                     cache_fox_k, cache_fox_v, cache_ft, page_table, g_fox_out[l],
                                     pages_per_step=DECODE_PAGES_PER_STEP)
        fox_o = jnp.pad(fox_o.reshape(db, FOX_W), ((0, ns - db), (0, 0)))
        pad_state = lambda a: jnp.pad(a.reshape(1, db, S5_NS), ((0, 0), (0, ns - db), (0, 0)))
        s5_o, hre, him = s5_mixer(rest.reshape(1, ns, 4 * S5_W), pad_state(state_s5_re[l]), pad_state(state_s5_im[l]),
                                  w["s5_lam_re"], w["s5_lam_im"], w["s5_wb_f32"], w["s5_wc_f32"], s5_d[l],
                                  s5_w_glu[l], s5_b_glu[l], g_s5_out[l], nseq=1, rows=ns, nsteps=1, scan=False,
                                  col_block=0, precise=True)
        seq_pad = lambda a: jnp.pad(a[:db].reshape(db, 1, a.shape[1]), ((0, 0), (0, GLA_CHUNK - 1), (0, 0)))
        gla_o, st = gla_mixer(seq_pad(rest), seq_pad(tail), w["gla_wa_f32"], gla_b_a[l], gla_gain[l],
                              jnp.swapaxes(state_gla[l], 2, 3), nseq=db, rows=GLA_CHUNK, nsteps=1, t_valid=1,
                              precise=True)
        gla_o = jnp.pad(gla_o[:, 0, :], ((0, ns - db), (0, 0)))
        cat = jnp.concatenate([fox_o, s5_o.reshape(ns, S5_W), gla_o], axis=1)
        outs["sk"].append(hd3(k_f).reshape(db, 1, FOX_HEADS, FOX_HD))
        outs["sv"].append(hd3(v_f).reshape(db, 1, FOX_HEADS, FOX_HD))
        outs["sf"].append(logf_new.reshape(db, 1, FOX_HEADS))
        outs["s5r"].append(hre[0, :db].reshape(db, S5_GROUPS, S5_STATE))
        outs["s5i"].append(him[0, :db].reshape(db, S5_GROUPS, S5_STATE))
        outs["sg"].append(jnp.swapaxes(st, 2, 3))
        xs = _post_mixer_sample(xs, cat, cache_mem_k[l].reshape(db, MEM_TOKENS, MEM_W),
                                cache_mem_v[l].reshape(db, MEM_TOKENS, MEM_W), w, p, l, db)

    stk = lambda k: jnp.stack(outs[k])
    return (xp.reshape(b, t, d), xs[:db].reshape(db, 1, d),
            stk("pk"), stk("pv"), stk("pf"), stk("sk"), stk("sv"), stk("sf"),
            stk("p5r"), stk("p5i"), stk("s5r"), stk("s5i"), stk("pg"), stk("sg"), stk("pmk"), stk("pmv"))


def _post_mixer_sample(x, cat, mk, mv, w, p, l, db):
    ns = x.shape[0]
    x = pmatmul(cat, p["w_out"], l, tk=512, epi="res", aux=(x,), name="proj_out_sample")
    hq = rmsnorm_cast(x, p["g_mem_pre"][l], ns, out_dtype=F32)
    q = pmatmul(hq, p["mem_w_q"], l, tk=512, epi="headnorm", aux=(w["mem_q_gain"].reshape(1, MEM_W),),
                post_scale=MEM_HD ** -0.5, name="mem_q_sample")
    q3 = jnp.pad(q[:db].reshape(db, 1, MEM_W), ((0, 0), (0, SAMPLE_ROWS - 1), (0, 0)))
    o = memory_attention(q3, mk, mv, SAMPLE_ROWS, precise=True)[:, 0, :]
    o = jnp.pad(o, ((0, ns - db), (0, 0)))
    x = pmatmul(o, p["mem_w_o"], l, tk=MEM_W, epi="res", aux=(x,), name="mem_o_sample")
    hm = rmsnorm_cast(x, p["g_moe"][l], ns, out_dtype=F32)
    gate = pmatmul(hm, w["w_router"][None], 0, tk=512, epi="router", aux=(w["b_router"],), name="router_sample")
    return moe_experts(hm, gate, p["moe_w_gate"], p["moe_w_up"], p["moe_w_down"], l, x, ns, precise=True)
```

```python
import functools
import math

import numpy as np
import jax
import jax.numpy as jnp
from jax import lax
from jax.experimental import pallas as pl
from jax.experimental.pallas import tpu as pltpu

F32, BF16 = jnp.float32, jnp.bfloat16

D_MODEL = 2048
FOX_HEADS, FOX_HD = 8, 128
FOX_W = FOX_HEADS * FOX_HD
S5_W, S5_GROUP, S5_GROUPS, S5_STATE = 512, 16, 32, 64
S5_NS = S5_GROUPS * S5_STATE
S5_BLOCKS = 4
S5_BC = S5_W // S5_BLOCKS
S5_BS = S5_NS // S5_BLOCKS
S5_SCAN_BLOCK = 8
GLA_HEADS, GLA_DK, GLA_DV = 4, 64, 128
GLA_W = GLA_HEADS * GLA_DV
GLA_QK = GLA_HEADS * GLA_DK
GLA_RANK = 16
GLA_TAU = 16.0
GLA_CHUNK = 16
MEM_TOKENS, MEM_HEADS, MEM_HD = 256, 4, 128
MEM_W = MEM_HEADS * MEM_HD
N_GROUPS, EXP_PER_GROUP, N_EXPERTS, D_EXPERT = 4, 4, 16, 256
PAGE_SIZE = 128
EPS = 1e-6
NEG = -1e30
LOG2E = math.log2(math.e)

LANES = 128
VMEM_LIMIT_BYTES = 48 * 1024 * 1024

PROJ_TN = 512
MOE_EXPERTS_PER_STEP = 2
DECODE_GROUP = 16
DECODE_PAGES_PER_STEP = 8
TAIL_W = LANES
SAMPLE_ROWS = 16


def _params(*sem):
    return pltpu.CompilerParams(dimension_semantics=sem, vmem_limit_bytes=VMEM_LIMIT_BYTES)


def _sds(shape, dtype):
    return jax.ShapeDtypeStruct(shape, dtype)


_NN = (((1,), (0,)), ((), ()))
_NT = (((1,), (1,)), ((), ()))
_TN = (((0,), (0,)), ((), ()))


def _mxu(a, b, precise, dims=_NN):
    if precise:
        return lax.dot_general(a.astype(F32), b.astype(F32), dims, preferred_element_type=F32,
                               precision=lax.Precision.HIGHEST)
    return lax.dot_general(a.astype(BF16), b.astype(BF16), dims, preferred_element_type=F32)


def _rms_kernel(x_ref, g_ref, o_ref):
    x = x_ref[...]
    y = x * lax.rsqrt(jnp.mean(x * x, axis=-1, keepdims=True) + EPS)
    o_ref[...] = (y * g_ref[...]).astype(o_ref.dtype)


def rmsnorm_cast(x, g, tm, out_dtype=BF16):
    n, d = x.shape
    return pl.pallas_call(
        _rms_kernel, grid=(n // tm,),
        in_specs=[pl.BlockSpec((tm, d), lambda i: (i, 0)), pl.BlockSpec((1, d), lambda i: (0, 0))],
        out_specs=pl.BlockSpec((tm, d), lambda i: (i, 0)),
        out_shape=_sds((n, d), out_dtype), compiler_params=_params("parallel"), name="rmsnorm_cast",
    )(x, g.reshape(1, d))


def _mm_kernel(*refs, headnorm, post_scale, logsig_lanes, has_res, n_out):
    x_ref, w_ref = refs[0], refs[1]
    pos = 2
    gain_ref = bias_ref = res_ref = None
    if headnorm:
        gain_ref, pos = refs[pos], pos + 1
    if logsig_lanes:
        bias_ref, pos = refs[pos], pos + 1
    if has_res:
        res_ref, pos = refs[pos], pos + 1
    outs = refs[pos:pos + n_out]
    acc = jnp.dot(x_ref[...], w_ref[...], preferred_element_type=F32)
    if logsig_lanes:
        lane = lax.broadcasted_iota(jnp.int32, acc.shape, 1)
        acc = jnp.where(lane < logsig_lanes, jax.nn.log_sigmoid(acc + bias_ref[...]), acc)
    if has_res:
        acc = res_ref[...] + acc
    if headnorm:
        for c in range(acc.shape[1] // LANES):
            sl = slice(c * LANES, (c + 1) * LANES)
            blk = acc[:, sl]
            y = blk * lax.rsqrt(jnp.mean(blk * blk, axis=-1, keepdims=True) + EPS)
            y = y * gain_ref[:, sl]
            if post_scale != 1.0:
                y = y * post_scale
            for o in outs:
                o[:, sl] = y.astype(o.dtype)
    else:
        for o in outs:
            o[...] = acc.astype(o.dtype)


def matmul(x, w, *, tm, tn, out_dtypes=(F32,), gain=None, post_scale=1.0, logsig_bias=None,
           logsig_lanes=0, res=None, layer=None, cols=None, name="matmul"):
    n, k = x.shape
    col0, c = cols if cols is not None else (0, w.shape[-1])
    jb = col0 // tn
    args = [x, w]
    if layer is None:
        w_spec = pl.BlockSpec((k, tn), lambda i, j: (0, jb + j))
    else:
        w_spec = pl.BlockSpec((None, k, tn), lambda i, j: (layer, 0, jb + j))
    in_specs = [pl.BlockSpec((tm, k), lambda i, j: (i, 0)), w_spec]
    if gain is not None:
        args.append(gain.reshape(1, c).astype(F32))
        in_specs.append(pl.BlockSpec((1, tn), lambda i, j: (0, j)))
    if logsig_lanes:
        args.append(logsig_bias.reshape(1, c).astype(F32))
        in_specs.append(pl.BlockSpec((1, tn), lambda i, j: (0, j)))
    if res is not None:
        args.append(res)
        in_specs.append(pl.BlockSpec((tm, tn), lambda i, j: (i, j)))
    outs = pl.pallas_call(
        functools.partial(_mm_kernel, headnorm=gain is not None, post_scale=post_scale,
                          logsig_lanes=logsig_lanes, has_res=res is not None, n_out=len(out_dtypes)),
        grid=(n // tm, c // tn), in_specs=in_specs,
        out_specs=[pl.BlockSpec((tm, tn), lambda i, j: (i, j)) for _ in out_dtypes],
        out_shape=[_sds((n, c), dt) for dt in out_dtypes],
        compiler_params=_params("parallel", "parallel"), name=name,
    )(*args)
    return outs


def _proj_norm_kernel(*refs, ks):
    nx = len(ks)
    x_refs, w_ref, res_ref, g_ref, o_ref, h_ref = refs[:nx], refs[nx], refs[nx + 1], refs[nx + 2], refs[nx + 3], refs[nx + 4]
    acc, off = None, 0
    for x_ref, k in zip(x_refs, ks):
        part = jnp.dot(x_ref[...], w_ref[off:off + k, :], preferred_element_type=F32)
        acc = part if acc is None else acc + part
        off += k
    x_new = res_ref[...] + acc
    o_ref[...] = x_new
    y = x_new * lax.rsqrt(jnp.mean(x_new * x_new, axis=-1, keepdims=True) + EPS)
    h_ref[...] = (y * g_ref[...]).astype(h_ref.dtype)


def proj_res_norm(xs, w, layer, res, g, tm, name):
    n, d = res.shape
    ks = tuple(x.shape[1] for x in xs)
    ktot = sum(ks)
    in_specs = [pl.BlockSpec((tm, k), lambda i: (i, 0)) for k in ks]
    in_specs += [pl.BlockSpec((None, ktot, d), lambda i: (layer, 0, 0)), pl.BlockSpec((tm, d), lambda i: (i, 0)),
                 pl.BlockSpec((1, d), lambda i: (0, 0))]
    return pl.pallas_call(
        functools.partial(_proj_norm_kernel, ks=ks), grid=(n // tm,), in_specs=in_specs,
        out_specs=[pl.BlockSpec((tm, d), lambda i: (i, 0)), pl.BlockSpec((tm, d), lambda i: (i, 0))],
        out_shape=[_sds((n, d), F32), _sds((n, d), BF16)], compiler_params=_params("parallel"), name=name,
    )(*xs, w, res, g.reshape(1, d))


def _head_rmsnorm(blk, gain):
    return blk * lax.rsqrt(jnp.mean(blk * blk, axis=-1, keepdims=True) + EPS) * gain


def _pmm_kernel(*refs, epi, n_aux, post_scale, w_cols_first):
    x_ref, w_ref = refs[0], refs[1]
    aux = refs[2:2 + n_aux]
    o_ref, acc_s = refs[2 + n_aux], refs[3 + n_aux]
    k = pl.program_id(0)

    @pl.when(k == 0)
    def _():
        acc_s[...] = jnp.zeros(acc_s.shape, F32)

    x, w = x_ref[...], w_ref[...]
    xh, wh = x.astype(BF16), w.astype(BF16)
    xl, wl = (x - xh.astype(F32)).astype(BF16), (w - wh.astype(F32)).astype(BF16)
    dims = _NT if w_cols_first else _NN
    acc_s[...] += (lax.dot_general(xh, wh, dims, preferred_element_type=F32)
                   + lax.dot_general(xl, wh, dims, preferred_element_type=F32)
                   + lax.dot_general(xh, wl, dims, preferred_element_type=F32))

    @pl.when(k == pl.num_programs(0) - 1)
    def _():
        acc = acc_s[...]
        if epi == "none":
            o_ref[...] = acc
        elif epi == "res":
            o_ref[...] = aux[0][...] + acc
        elif epi == "headnorm":
            for c in range(acc.shape[1] // LANES):
                sl = slice(c * LANES, (c + 1) * LANES)
                o_ref[:, sl] = _head_rmsnorm(acc[:, sl], aux[0][:, sl]) * post_scale
        elif epi == "inproj":
            gain_ref, bias_ref = aux
            for c in range(2 * FOX_W // LANES):
                sl = slice(c * LANES, (c + 1) * LANES)
                o_ref[:, sl] = _head_rmsnorm(acc[:, sl], gain_ref[:, sl])
            o_ref[:, 2 * FOX_W:3 * FOX_W] = acc[:, 2 * FOX_W:3 * FOX_W]
            t = acc[:, 3 * FOX_W:]
            lane = lax.broadcasted_iota(jnp.int32, t.shape, 1)
            o_ref[:, 3 * FOX_W:] = jnp.where(lane < FOX_HEADS, jax.nn.log_sigmoid(t + bias_ref[...]), t)
        elif epi == "router":
            o_ref[...] = _router_gate(acc + aux[0][...])


def pmatmul(x, w, layer, *, tk, epi="none", aux=(), post_scale=1.0, w_cols_first=False, name="pmatmul"):
    rows, kdim = x.shape
    if w_cols_first:
        c = w.shape[1]
        w_spec = pl.BlockSpec((None, c, tk), lambda k: (layer, 0, k))
    else:
        c = w.shape[2]
        w_spec = pl.BlockSpec((None, tk, c), lambda k: (layer, k, 0))
    in_specs = [pl.BlockSpec((rows, tk), lambda k: (0, k)), w_spec]
    in_specs += [pl.BlockSpec(a.shape, lambda k: (0, 0)) for a in aux]
    return pl.pallas_call(
        functools.partial(_pmm_kernel, epi=epi, n_aux=len(aux), post_scale=post_scale, w_cols_first=w_cols_first),
        grid=(kdim // tk,),
        in_specs=in_specs, out_specs=pl.BlockSpec((rows, c), lambda k: (0, 0)),
        out_shape=_sds((rows, c), F32), scratch_shapes=[pltpu.VMEM((rows, c), F32)],
        compiler_params=_params("arbitrary"), name=name,
    )(x, w, *aux)


def _fox_gate_kernel(t_ref, crow_ref, ccol_ref):
    x = t_ref[0]
    c = x.T[0:FOX_HEADS, :]
    t_len = c.shape[1]
    lane = lax.broadcasted_iota(jnp.int32, c.shape, 1)
    s = 1
    while s < t_len:
        c = c + jnp.where(lane >= s, pltpu.roll(c, s, axis=1), 0.0)
        s *= 2
    c = c * LOG2E
    crow_ref[0] = c
    cpad = jnp.concatenate([c, jnp.zeros((LANES - FOX_HEADS, t_len), F32)], axis=0)
    ccol_ref[0] = cpad.T


def fox_gate_cumsum(tail, b, t):
    return pl.pallas_call(
        _fox_gate_kernel, grid=(b,),
        in_specs=[pl.BlockSpec((1, t, TAIL_W), lambda i: (i, 0, 0))],
        out_specs=[pl.BlockSpec((1, FOX_HEADS, t), lambda i: (i, 0, 0)),
                   pl.BlockSpec((1, t, TAIL_W), lambda i: (i, 0, 0))],
        out_shape=[_sds((b, FOX_HEADS, t), F32), _sds((b, t, TAIL_W), F32)],
        compiler_params=_params("parallel"), name="fox_gate_cumsum",
    )(tail.reshape(b, t, TAIL_W))


def _lane_tile(x, n):
    return x if n == 1 else jnp.concatenate([x] * n, axis=1)


def _fox_attn_kernel(qi_tab, ki_tab, q_ref, k_ref, v_ref, cq_ref, ck_ref, g_ref, o_ref, m_s, l_s, acc_s, cq_s,
                     *, tq, tk):
    p_id = pl.program_id(1)
    qi, ki = qi_tab[p_id], ki_tab[p_id]
    nrep = tk // LANES

    @pl.when(ki == 0)
    def _():
        m_s[...] = jnp.full(m_s.shape, NEG, F32)
        l_s[...] = jnp.zeros(l_s.shape, F32)
        acc_s[...] = jnp.zeros(acc_s.shape, F32)
        for h in range(FOX_HEADS):
            cq_s[h] = jnp.broadcast_to(cq_ref[0, :, h:h + 1], (tq, LANES))

    def step(masked):
        if masked:
            row = lax.broadcasted_iota(jnp.int32, (tq, tk), 0)
            col = lax.broadcasted_iota(jnp.int32, (tq, tk), 1)
            causal = col <= row
        for h in range(FOX_HEADS):
            sl = slice(h * FOX_HD, (h + 1) * FOX_HD)
            s = lax.dot_general(q_ref[0, :, sl], k_ref[0, :, sl], _NT, preferred_element_type=F32)
            s = s - ck_ref[0, h:h + 1, :]
            if masked:
                s = jnp.where(causal, s, NEG)
            c_q = cq_s[h]
            m_old = m_s[h]
            m_new = jnp.maximum(m_old, jnp.broadcast_to(jnp.max(s, axis=-1, keepdims=True), (tq, LANES)) + c_q)
            alpha = jnp.exp2(m_old - m_new)
            p = jnp.exp2(s + _lane_tile(c_q - m_new, nrep))
            l_s[h] = l_s[h] * alpha + jnp.broadcast_to(jnp.sum(p, axis=-1, keepdims=True), (tq, LANES))
            m_s[h] = m_new
            acc_s[:, sl] = acc_s[:, sl] * alpha + jnp.dot(p.astype(BF16), v_ref[0, :, sl],
                                                          preferred_element_type=F32)

    @pl.when(ki < qi)
    def _():
        step(False)

    @pl.when(ki == qi)
    def _():
        step(True)
        ssq = jnp.zeros((tq, 1), F32)
        for h in range(FOX_HEADS):
            sl = slice(h * FOX_HD, (h + 1) * FOX_HD)
            o = acc_s[:, sl] / l_s[h]
            acc_s[:, sl] = o
            ssq = ssq + jnp.sum(o * o, axis=-1, keepdims=True)
        inv = lax.rsqrt(ssq / FOX_W + EPS)
        o_ref[0] = (acc_s[...] * inv * g_ref[...]).astype(o_ref.dtype)


def fox_prompt_attention(q, k, v, ccol, crow, g_out, b, t, tq):
    nq = t // tq
    pairs = [(i, j) for i in range(nq) for j in range(i + 1)]
    qi_tab = jnp.asarray(np.array([p[0] for p in pairs], np.int32))
    ki_tab = jnp.asarray(np.array([p[1] for p in pairs], np.int32))
    q3, k3, v3 = (a.reshape(b, t, FOX_W) for a in (q, k, v))
    grid_spec = pltpu.PrefetchScalarGridSpec(
        num_scalar_prefetch=2, grid=(b, len(pairs)),
        in_specs=[
            pl.BlockSpec((1, tq, FOX_W), lambda bi, p, qt, kt: (bi, qt[p], 0)),
            pl.BlockSpec((1, tq, FOX_W), lambda bi, p, qt, kt: (bi, kt[p], 0)),
            pl.BlockSpec((1, tq, FOX_W), lambda bi, p, qt, kt: (bi, kt[p], 0)),
            pl.BlockSpec((1, tq, TAIL_W), lambda bi, p, qt, kt: (bi, qt[p], 0)),
            pl.BlockSpec((1, FOX_HEADS, tq), lambda bi, p, qt, kt: (bi, 0, kt[p])),
            pl.BlockSpec((1, FOX_W), lambda bi, p, qt, kt: (0, 0)),
        ],
        out_specs=pl.BlockSpec((1, tq, FOX_W), lambda bi, p, qt, kt: (bi, qt[p], 0)),
        scratch_shapes=[pltpu.VMEM((FOX_HEADS, tq, LANES), F32), pltpu.VMEM((FOX_HEADS, tq, LANES), F32),
                        pltpu.VMEM((tq, FOX_W), F32), pltpu.VMEM((FOX_HEADS, tq, LANES), F32)],
    )
    out = pl.pallas_call(
        functools.partial(_fox_attn_kernel, tq=tq, tk=tq), grid_spec=grid_spec,
        out_shape=_sds((b, t, FOX_W), BF16),
        compiler_params=_params("parallel", "arbitrary"), name="fox_prompt_attention",
    )(qi_tab, ki_tab, q3, k3, v3, ccol, crow, g_out.reshape(1, FOX_W))
    return out.reshape(b * t, FOX_W)


def _fox_decode_kernel(pt_ref, q_ref, kn_ref, vn_ref, fn_ref, g_ref, *refs, pages_per_step, scale):
    pp = pages_per_step
    k_refs, v_refs, f_refs = refs[0:pp], refs[pp:2 * pp], refs[2 * pp:3 * pp]
    o_ref = refs[3 * pp]
    m_s, l_s, acc_s, r_s = refs[3 * pp + 1:]
    step_id = pl.program_id(1)
    q = q_ref[0] * scale

    @pl.when(step_id == 0)
    def _():
        m_s[...] = jnp.sum(q * kn_ref[0], axis=-1, keepdims=True)
        l_s[...] = jnp.ones(l_s.shape, F32)
        acc_s[...] = vn_ref[0]
        r_s[...] = fn_ref[0]

    tok_r = lax.broadcasted_iota(jnp.int32, (PAGE_SIZE, PAGE_SIZE), 0)
    tok_c = lax.broadcasted_iota(jnp.int32, (PAGE_SIZE, PAGE_SIZE), 1)
    later = (tok_r > tok_c).astype(F32)
    r_run = r_s[...]
    parts = []
    grp = DECODE_GROUP
    tok = lax.broadcasted_iota(jnp.int32, (grp, 1, FOX_HD), 0)
    lane = lax.broadcasted_iota(jnp.int32, (grp, 1, FOX_HD), 2)
    for j in range(pp):
        f = f_refs[j][...]
        r = jnp.dot(f, later, preferred_element_type=F32, precision=lax.Precision.HIGHEST) + r_run
        r_run = r_run + jnp.sum(f, axis=-1, keepdims=True)
        for g in range(PAGE_SIZE // grp):
            rows = pl.ds(g * grp, grp)
            prod = k_refs[j][rows] * q[None] + jnp.where(lane == tok + g * grp, r[None], 0.0)
            s = jnp.sum(prod, axis=-1, keepdims=True)
            m_g = jnp.max(s, axis=0)
            p = jnp.exp(s - m_g[None])
            parts.append((m_g, jnp.sum(p, axis=0), jnp.sum(p * v_refs[j][rows], axis=0)))
    m_old = m_s[...]
    m_new = m_old
    for m_j, _, _ in parts:
        m_new = jnp.maximum(m_new, m_j)
    alpha = jnp.exp(m_old - m_new)
    l_run, acc = l_s[...] * alpha, acc_s[...] * alpha
    for m_j, l_j, pv_j in parts:
        a_j = jnp.exp(m_j - m_new)
        l_run, acc = l_run + l_j * a_j, acc + pv_j * a_j
    m_s[...], l_s[...], acc_s[...], r_s[...] = m_new, l_run, acc, r_run

    @pl.when(step_id == pl.num_programs(1) - 1)
    def _():
        o = acc_s[...] / l_s[...]
        ssq = jnp.sum(jnp.sum(o * o, axis=-1, keepdims=True), axis=0, keepdims=True)
        o_ref[0] = (o * lax.rsqrt(ssq / FOX_W + EPS) * g_ref[...]).astype(o_ref.dtype)


def fox_sample_attention(layer, q, k_new, v_new, logf_new, cache_k, cache_v, cache_ft, page_table, g_out,
                         pages_per_step):
    db, n_pages = page_table.shape
    n_steps = n_pages // pages_per_step

    def page_map(j):
        def im(bi, si, pt):
            return (layer, pt[bi, n_pages - 1 - (si * pages_per_step + j)], 0, 0, 0)
        return im

    def page_map_f(j):
        def im(bi, si, pt):
            return (layer, pt[bi, n_pages - 1 - (si * pages_per_step + j)], 0, 0)
        return im

    vec = pl.BlockSpec((1, FOX_HEADS, FOX_HD), lambda bi, si, pt: (bi, 0, 0))
    in_specs = [vec, vec, vec,
                pl.BlockSpec((1, FOX_HEADS, 1), lambda bi, si, pt: (bi, 0, 0)),
                pl.BlockSpec((FOX_HEADS, FOX_HD), lambda bi, si, pt: (0, 0))]
    in_specs += [pl.BlockSpec((None, None, PAGE_SIZE, FOX_HEADS, FOX_HD), page_map(j)) for j in range(pages_per_step)]
    in_specs += [pl.BlockSpec((None, None, PAGE_SIZE, FOX_HEADS, FOX_HD), page_map(j)) for j in range(pages_per_step)]
    in_specs += [pl.BlockSpec((None, None, FOX_HEADS, PAGE_SIZE), page_map_f(j)) for j in range(pages_per_step)]
    grid_spec = pltpu.PrefetchScalarGridSpec(
        num_scalar_prefetch=1, grid=(db, n_steps), in_specs=in_specs,
        out_specs=pl.BlockSpec((1, FOX_HEADS, FOX_HD), lambda bi, si, pt: (bi, 0, 0)),
        scratch_shapes=[pltpu.VMEM((FOX_HEADS, 1), F32), pltpu.VMEM((FOX_HEADS, 1), F32),
                        pltpu.VMEM((FOX_HEADS, FOX_HD), F32), pltpu.VMEM((FOX_HEADS, 1), F32)],
    )
    return pl.pallas_call(
        functools.partial(_fox_decode_kernel, pages_per_step=pages_per_step, scale=FOX_HD ** -0.5),
        grid_spec=grid_spec, out_shape=_sds((db, FOX_HEADS, FOX_HD), F32),
        compiler_params=_params("parallel", "arbitrary"), name="fox_sample_attention",
    )(page_table, q, k_new, v_new, logf_new, g_out.reshape(FOX_HEADS, FOX_HD),
      *([cache_k] * pages_per_step), *([cache_v] * pages_per_step), *([cache_ft] * pages_per_step))


def _s5_param_kernel(are_ref, aim_ref, ldt_ref, bre_ref, bim_ref, lre_ref, lim_ref, bbre_ref, bbim_ref):
    a_re, a_im = are_ref[...], aim_ref[...]
    dt = jnp.exp(ldt_ref[...])
    mag = jnp.exp(a_re * dt)
    lam_re, lam_im = mag * jnp.cos(a_im * dt), mag * jnp.sin(a_im * dt)
    den = a_re * a_re + a_im * a_im
    z_re = ((lam_re - 1.0) * a_re + lam_im * a_im) / den
    z_im = (lam_im * a_re - (lam_re - 1.0) * a_im) / den
    lre_ref[...] = lam_re
    lim_ref[...] = lam_im
    b_re, b_im = bre_ref[...], bim_ref[...]
    bbre_ref[...] = z_re[:, None, :] * b_re - z_im[:, None, :] * b_im
    bbim_ref[...] = z_re[:, None, :] * b_im + z_im[:, None, :] * b_re


def s5_discretise(a_re, a_im, log_dt, b_re_t, b_im_t):
    g, p = a_re.shape
    c = b_re_t.shape[1]
    return pl.pallas_call(
        _s5_param_kernel,
        out_shape=[_sds((g, p), F32), _sds((g, p), F32), _sds((g, c, p), F32), _sds((g, c, p), F32)],
        name="s5_discretise",
    )(a_re, a_im, log_dt.reshape(g, 1), b_re_t, b_im_t)


def _block_diag(w, groups_per_block):
    g, a, b = w.shape
    nb = g // groups_per_block
    eye = jnp.eye(groups_per_block, dtype=w.dtype)
    w5 = w.reshape(nb, groups_per_block, a, b)
    return jnp.einsum("jgab,gh->jgahb", w5, eye).reshape(nb, groups_per_block * a, groups_per_block * b)


def _row_tile(x, n):
    tile = jnp.broadcast_to(x, (S5_SCAN_BLOCK, x.shape[1]))
    return tile if n == S5_SCAN_BLOCK else jnp.concatenate([tile] * (n // S5_SCAN_BLOCK), axis=0)


def _s5_kernel(u_ref, h0re_ref, h0im_ref, lre_ref, lim_ref, wb_ref, wc_ref, d_ref, wglu_ref, bglu_ref, gout_ref,
               y_ref, hre_out, him_out, hre_s, him_s, cre_s, cim_s, y_s, *, rows, scan, precise):
    t_id = pl.program_id(1)
    lam_re, lam_im = lre_ref[...], lim_ref[...]

    @pl.when(t_id == 0)
    def _():
        cre_s[...] = h0re_ref[0]
        cim_s[...] = h0im_ref[0]

    u = u_ref[0]
    for j in range(S5_BLOCKS):
        bu = _mxu(u[:, j * S5_BC:(j + 1) * S5_BC], wb_ref[j], precise)
        hre_s[:, j * S5_BS:(j + 1) * S5_BS] = bu[:, :S5_BS]
        him_s[:, j * S5_BS:(j + 1) * S5_BS] = bu[:, S5_BS:]
    c_re, c_im = cre_s[...], cim_s[...]
    if scan:
        hre_s[0:1, :] = hre_s[0:1, :] + (lam_re * c_re - lam_im * c_im)
        him_s[0:1, :] = him_s[0:1, :] + (lam_re * c_im + lam_im * c_re)
        seg = S5_SCAN_BLOCK
        steps = rows // seg
        l_re, l_im = _row_tile(lam_re, seg), _row_tile(lam_im, seg)
        p_re, p_im = hre_s[0:seg, :], him_s[0:seg, :]
        for v in range(1, steps):
            sl = slice(v * seg, (v + 1) * seg)
            p_re, p_im = (hre_s[sl, :] + (l_re * p_re - l_im * p_im), him_s[sl, :] + (l_re * p_im + l_im * p_re))
            hre_s[sl, :] = p_re
            him_s[sl, :] = p_im
        f_re, f_im = lam_re, lam_im
        n = 1
        while n < steps:
            f_re, f_im = f_re * f_re - f_im * f_im, 2.0 * f_re * f_im
            n *= 2
        f_re, f_im = _row_tile(f_re, seg), _row_tile(f_im, seg)
        pos = lax.broadcasted_iota(jnp.int32, (seg, S5_NS), 0)
        e_re, e_im = p_re, p_im
        s = 1
        while s < seg:
            m_re, m_im = jnp.where(pos >= s, f_re, 0.0), jnp.where(pos >= s, f_im, 0.0)
            b_re, b_im = pltpu.roll(e_re, s, axis=0), pltpu.roll(e_im, s, axis=0)
            e_re, e_im = e_re + (m_re * b_re - m_im * b_im), e_im + (m_re * b_im + m_im * b_re)
            f_re, f_im = f_re * f_re - f_im * f_im, 2.0 * f_re * f_im
            s *= 2
        c_re = jnp.where(pos >= 1, pltpu.roll(e_re, 1, axis=0), 0.0)
        c_im = jnp.where(pos >= 1, pltpu.roll(e_im, 1, axis=0), 0.0)
        for v in range(steps):
            c_re, c_im = l_re * c_re - l_im * c_im, l_re * c_im + l_im * c_re
            sl = slice(v * seg, (v + 1) * seg)
            hre_s[sl, :] = hre_s[sl, :] + c_re
            him_s[sl, :] = him_s[sl, :] + c_im
        cre_s[...] = hre_s[rows - 1:rows, :]
        cim_s[...] = him_s[rows - 1:rows, :]
        hre_out[0] = hre_s[rows - 1:rows, :]
        him_out[0] = him_s[rows - 1:rows, :]
    else:
        hre_s[...] = hre_s[...] + (lam_re * c_re - lam_im * c_im)
        him_s[...] = him_s[...] + (lam_re * c_im + lam_im * c_re)
        hre_out[0] = hre_s[...]
        him_out[0] = him_s[...]

    for j in range(S5_BLOCKS):
        sl = slice(j * S5_BS, (j + 1) * S5_BS)
        yj = _mxu(hre_s[:, sl], wc_ref[j, 0:S5_BS, :], precise)
        yj = yj + _mxu(him_s[:, sl], wc_ref[j, S5_BS:2 * S5_BS, :], precise)
        y_s[:, j * S5_BC:(j + 1) * S5_BC] = yj
    y = y_s[...] + d_ref[...] * u
    y = jax.nn.gelu(y)
    z = _mxu(y, wglu_ref[...], precise) + bglu_ref[...]
    y = y * jax.nn.sigmoid(z)
    y = y * lax.rsqrt(jnp.mean(y * y, axis=-1, keepdims=True) + EPS) * gout_ref[...]
    y_ref[0] = y.astype(y_ref.dtype)


def s5_mixer(u, h0_re, h0_im, lam_re, lam_im, wb, wc, d_skip, w_glu, b_glu, g_out, *, nseq, rows, nsteps, scan,
             col_block, precise=False):
    st_rows = 1 if scan else rows
    full = lambda shape: pl.BlockSpec(shape, lambda b, t: tuple(0 for _ in shape))
    state_spec = pl.BlockSpec((1, st_rows, S5_NS), lambda b, t: (b, 0, 0))
    y, hre, him = pl.pallas_call(
        functools.partial(_s5_kernel, rows=rows, scan=scan, precise=precise), grid=(nseq, nsteps),
        in_specs=[pl.BlockSpec((1, rows, S5_W), lambda b, t: (b, t, col_block)),
                  state_spec, state_spec, full((1, S5_NS)), full((1, S5_NS)),
                  full((S5_BLOCKS, S5_BC, 2 * S5_BS)), full((S5_BLOCKS, 2 * S5_BS, S5_BC)),
                  full((1, S5_W)), full((S5_W, S5_W)), full((1, S5_W)), full((1, S5_W))],
        out_specs=[pl.BlockSpec((1, rows, S5_W), lambda b, t: (b, t, 0)), state_spec, state_spec],
        out_shape=[_sds((nseq, nsteps * rows, S5_W), F32 if precise else BF16),
                   _sds((nseq, st_rows, S5_NS), F32), _sds((nseq, st_rows, S5_NS), F32)],
        scratch_shapes=[pltpu.VMEM((rows, S5_NS), F32), pltpu.VMEM((rows, S5_NS), F32),
                        pltpu.VMEM((st_rows, S5_NS), F32), pltpu.VMEM((st_rows, S5_NS), F32),
                        pltpu.VMEM((rows, S5_W), F32)],
        compiler_params=_params("parallel", "arbitrary"), name="s5_mixer",
    )(u, h0_re, h0_im, lam_re, lam_im, wb, wc, d_skip.reshape(1, S5_W), w_glu, b_glu.reshape(1, S5_W),
      g_out.reshape(1, S5_W))
    return y, hre, him


def _gla_kernel(qk_ref, v_ref, go_ref, ga_ref, wa_ref, ba_ref, gain_ref, s0_ref, o_ref, st_out,
                st_s, o_s, *, rows, t_valid, precise):
    t_id = pl.program_id(1)

    @pl.when(t_id == 0)
    def _():
        st_s[...] = s0_ref[0]

    x = _mxu(ga_ref[0], wa_ref[...], precise) + ba_ref[...]
    la = jax.nn.log_sigmoid(x) / GLA_TAU
    row = lax.broadcasted_iota(jnp.int32, la.shape, 0)
    if t_valid < rows:
        la = jnp.where(row < t_valid, la, 0.0)
    pos = row & (GLA_CHUNK - 1)
    b, e = la, la
    s = 1
    while s < GLA_CHUNK:
        b = b + jnp.where(pos >= s, pltpu.roll(b, s, axis=0), 0.0)
        e = e + jnp.where(pos < GLA_CHUNK - s, pltpu.roll(e, rows - s, axis=0), 0.0)
        s *= 2
    b_last = b + e - la
    qk = qk_ref[0]
    q, k = qk[:, 0:GLA_QK] * (GLA_DK ** -0.5), qk[:, GLA_QK:2 * GLA_QK]
    qe = q * jnp.exp(b)
    ke = k * jnp.exp(-b)
    kd = k * jnp.exp(b_last - b)
    dec = jnp.exp(b_last)
    v = v_ref[0]
    if not precise:
        qe, ke, kd, v = qe.astype(BF16), ke.astype(BF16), kd.astype(BF16), v.astype(BF16)
    r_i = lax.broadcasted_iota(jnp.int32, (rows, rows), 0)
    c_i = lax.broadcasted_iota(jnp.int32, (rows, rows), 1)
    chunk_shift = GLA_CHUNK.bit_length() - 1
    causal = ((r_i >> chunk_shift) == (c_i >> chunk_shift)) & (c_i <= r_i)
    for h in range(GLA_HEADS):
        ks = slice(h * GLA_DK, (h + 1) * GLA_DK)
        vs = slice(h * GLA_DV, (h + 1) * GLA_DV)
        att = _mxu(qe[:, ks], ke[:, ks], precise, _NT)
        att = jnp.where(causal, att, 0.0)
        o_s[:, vs] = _mxu(att, v[:, vs], precise)

    state = [st_s[h] for h in range(GLA_HEADS)]
    for j in range(rows // GLA_CHUNK):
        rs = slice(j * GLA_CHUNK, (j + 1) * GLA_CHUNK)
        for h in range(GLA_HEADS):
            ks = slice(h * GLA_DK, (h + 1) * GLA_DK)
            vs = slice(h * GLA_DV, (h + 1) * GLA_DV)
            o_s[rs, vs] = o_s[rs, vs] + _mxu(qe[rs, ks], state[h], precise, _NT)
            upd = _mxu(v[rs, vs], kd[rs, ks], precise, _TN)
            state[h] = state[h] * dec[j * GLA_CHUNK:j * GLA_CHUNK + 1, ks] + upd
    for h in range(GLA_HEADS):
        st_s[h] = state[h]
    st_out[0] = st_s[...]

    go = go_ref[0]
    for h in range(GLA_HEADS):
        vs = slice(h * GLA_DV, (h + 1) * GLA_DV)
        o = o_s[:, vs]
        y = o * lax.rsqrt(jnp.mean(o * o, axis=-1, keepdims=True) + EPS) * gain_ref[...]
        o_ref[0, :, vs] = (y * jax.nn.silu(go[:, vs])).astype(o_ref.dtype)


def gla_mixer(rest, tail, wa_pad, b_a, gain, s0_t, *, nseq, rows, nsteps, t_valid, precise=False):
    full = lambda shape: pl.BlockSpec(shape, lambda b, t: tuple(0 for _ in shape))
    st_spec = pl.BlockSpec((1, GLA_HEADS, GLA_DV, GLA_DK), lambda b, t: (b, 0, 0, 0))
    o, st = pl.pallas_call(
        functools.partial(_gla_kernel, rows=rows, t_valid=t_valid, precise=precise), grid=(nseq, nsteps),
        in_specs=[pl.BlockSpec((1, rows, 2 * GLA_QK), lambda b, t: (b, t, 1)),
                  pl.BlockSpec((1, rows, GLA_W), lambda b, t: (b, t, 2)),
                  pl.BlockSpec((1, rows, GLA_W), lambda b, t: (b, t, 3)),
                  pl.BlockSpec((1, rows, TAIL_W), lambda b, t: (b, t, 0)),
                  full((TAIL_W, GLA_QK)), full((1, GLA_QK)), full((1, GLA_DV)), st_spec],
        out_specs=[pl.BlockSpec((1, rows, GLA_W), lambda b, t: (b, t, 0)), st_spec],
        out_shape=[_sds((nseq, nsteps * rows, GLA_W), F32 if precise else BF16),
                   _sds((nseq, GLA_HEADS, GLA_DV, GLA_DK), F32)],
        scratch_shapes=[pltpu.VMEM((GLA_HEADS, GLA_DV, GLA_DK), F32), pltpu.VMEM((rows, GLA_W), F32)],
        compiler_params=_params("parallel", "arbitrary"), name="gla_mixer",
    )(rest, rest, rest, tail, wa_pad, b_a.reshape(1, GLA_QK), gain.reshape(1, GLA_DV), s0_t)
    return o, st


def _mem_attn_kernel(q_ref, k_ref, v_ref, o_ref, *, precise):
    for h in range(MEM_HEADS):
        sl = slice(h * MEM_HD, (h + 1) * MEM_HD)
        s = _mxu(q_ref[0, :, sl], k_ref[0, :, sl], precise, _NT)
        p = jnp.exp(s - jnp.max(s, axis=-1, keepdims=True))
        o = _mxu(p, v_ref[0, :, sl], precise)
        o_ref[0, :, sl] = (o / jnp.sum(p, axis=-1, keepdims=True)).astype(o_ref.dtype)


def memory_attention(q, mk, mv, tq, precise=False):
    b, t, _ = q.shape
    return pl.pallas_call(
        functools.partial(_mem_attn_kernel, precise=precise), grid=(b, t // tq),
        in_specs=[pl.BlockSpec((1, tq, MEM_W), lambda bi, i: (bi, i, 0)),
                  pl.BlockSpec((1, MEM_TOKENS, MEM_W), lambda bi, i: (bi, 0, 0)),
                  pl.BlockSpec((1, MEM_TOKENS, MEM_W), lambda bi, i: (bi, 0, 0))],
        out_specs=pl.BlockSpec((1, tq, MEM_W), lambda bi, i: (bi, i, 0)),
        out_shape=_sds((b, t, MEM_W), F32 if precise else BF16), compiler_params=_params("parallel", "parallel"),
        name="memory_attention",
    )(q, mk, mv)


def _router_gate(logits):
    lane = lax.broadcasted_iota(jnp.int32, logits.shape, 1)
    is_group = (lane >= N_EXPERTS) & (lane < N_EXPERTS + N_GROUPS)
    lg = jnp.where(is_group, logits, NEG)
    eg = jnp.where(is_group, jnp.exp(lg - jnp.max(lg, axis=-1, keepdims=True)), 0.0)
    pg = eg / jnp.sum(eg, axis=-1, keepdims=True)
    g_w = jnp.max(pg, axis=-1, keepdims=True)
    lane_f = lane.astype(F32)
    far = float(4 * LANES)
    g_lane = jnp.min(jnp.where(is_group & (pg == g_w), lane_f, far), axis=-1, keepdims=True)
    group_of_lane = (lane >> (EXP_PER_GROUP.bit_length() - 1)).astype(F32)
    sel = (lane < N_EXPERTS) & (group_of_lane == g_lane - float(N_EXPERTS))
    le = jnp.where(sel, logits, NEG)
    ee = jnp.where(sel, jnp.exp(le - jnp.max(le, axis=-1, keepdims=True)), 0.0)
    pe = ee / jnp.sum(ee, axis=-1, keepdims=True)
    m1 = jnp.max(jnp.where(sel, pe, -1.0), axis=-1, keepdims=True)
    i1 = jnp.min(jnp.where(sel & (pe == m1), lane_f, far), axis=-1, keepdims=True)
    rest = sel & (lane_f != i1)
    m2 = jnp.max(jnp.where(rest, pe, -1.0), axis=-1, keepdims=True)
    i2 = jnp.min(jnp.where(rest & (pe == m2), lane_f, far), axis=-1, keepdims=True)
    tot = m1 + m2
    w = jnp.where(lane_f == i1, m1 / tot, jnp.where(lane_f == i2, m2 / tot, 0.0))
    return w * g_w


def _router_kernel(x_ref, g_ref, wh_ref, wl_ref, b_ref, gate_ref):
    x = x_ref[...]
    xn = x * lax.rsqrt(jnp.mean(x * x, axis=-1, keepdims=True) + EPS) * g_ref[...]
    xh = xn.astype(BF16)
    xl = (xn - xh.astype(F32)).astype(BF16)
    wh = wh_ref[...]
    logits = (jnp.dot(xh, wh, preferred_element_type=F32) + jnp.dot(xl, wh, preferred_element_type=F32)
              + jnp.dot(xh, wl_ref[...], preferred_element_type=F32))
    gate_ref[...] = _router_gate(logits + b_ref[...])


def moe_router(x, g, w_r, b_r, tm):
    n, d = x.shape
    w_hi = w_r.astype(BF16)
    w_lo = (w_r - w_hi.astype(F32)).astype(BF16)
    w_spec = pl.BlockSpec((d, LANES), lambda i: (0, 0))
    return pl.pallas_call(
        _router_kernel, grid=(n // tm,),
        in_specs=[pl.BlockSpec((tm, d), lambda i: (i, 0)), pl.BlockSpec((1, d), lambda i: (0, 0)),
                  w_spec, w_spec, pl.BlockSpec((1, LANES), lambda i: (0, 0))],
        out_specs=pl.BlockSpec((tm, LANES), lambda i: (i, 0)),
        out_shape=_sds((n, LANES), F32), compiler_params=_params("parallel"), name="moe_router",
    )(x, g.reshape(1, d), w_hi, w_lo, b_r)


def _expert_kernel(x_ref, gate_ref, wg_ref, wu_ref, wd_ref, res_ref, *rest, precise, next_norm):
    if next_norm:
        g_ref, o_ref, h_ref, acc_s = rest
    else:
        o_ref, acc_s = rest
    e = pl.program_id(1)

    @pl.when(e == 0)
    def _():
        acc_s[...] = jnp.zeros(acc_s.shape, F32)

    x = x_ref[...]
    gate = gate_ref[...]
    lane = lax.broadcasted_iota(jnp.int32, gate.shape, 1)
    parts = []
    for u in range(MOE_EXPERTS_PER_STEP):
        a = jax.nn.silu(_mxu(x, wg_ref[u], precise)) * _mxu(x, wu_ref[u], precise)
        g_col = jnp.sum(jnp.where(lane == e * MOE_EXPERTS_PER_STEP + u, gate, 0.0), axis=-1, keepdims=True)
        a = a * g_col
        parts.append(a if precise else a.astype(BF16))
    a_cat = parts[0] if len(parts) == 1 else jnp.concatenate(parts, axis=1)
    w_down = wd_ref[...].reshape(MOE_EXPERTS_PER_STEP * D_EXPERT, wd_ref.shape[-1])
    acc_s[...] += _mxu(a_cat, w_down, precise)

    @pl.when(e == pl.num_programs(1) - 1)
    def _():
        x_new = res_ref[...] + acc_s[...]
        o_ref[...] = x_new
        if next_norm:
            y = x_new * lax.rsqrt(jnp.mean(x_new * x_new, axis=-1, keepdims=True) + EPS)
            h_ref[...] = (y * g_ref[...]).astype(h_ref.dtype)


def moe_experts(x, gate, w_gate, w_up, w_down, layer, res, tm, precise=False, g_next=None):
    n, d = x.shape
    row = pl.BlockSpec((tm, d), lambda i, e: (i, 0))
    in_specs = [row, pl.BlockSpec((tm, LANES), lambda i, e: (i, 0)),
                pl.BlockSpec((None, MOE_EXPERTS_PER_STEP, d, D_EXPERT), lambda i, e: (layer, e, 0, 0)),
                pl.BlockSpec((None, MOE_EXPERTS_PER_STEP, d, D_EXPERT), lambda i, e: (layer, e, 0, 0)),
                pl.BlockSpec((None, MOE_EXPERTS_PER_STEP, D_EXPERT, d), lambda i, e: (layer, e, 0, 0)), row]
    args = [x, gate, w_gate, w_up, w_down, res]
    out_specs, out_shape = row, _sds((n, d), F32)
    if g_next is not None:
        in_specs.append(pl.BlockSpec((1, d), lambda i, e: (0, 0)))
        args.append(g_next.reshape(1, d))
        out_specs, out_shape = [row, row], [_sds((n, d), F32), _sds((n, d), BF16)]
    return pl.pallas_call(
        functools.partial(_expert_kernel, precise=precise, next_norm=g_next is not None),
        grid=(n // tm, N_EXPERTS // MOE_EXPERTS_PER_STEP),
        in_specs=in_specs, out_specs=out_specs, out_shape=out_shape, scratch_shapes=[pltpu.VMEM((tm, d), F32)],
        compiler_params=_params("parallel", "arbitrary"), name="moe_experts",
    )(*args)


_BF16_STACKS = ("w_in", "w_out", "mem_w_q", "mem_w_k", "mem_w_v", "mem_w_o", "moe_w_gate", "moe_w_up", "moe_w_down")


def _layer_weights(l, p, pb):
    w_in = pb["w_in"][l]
    c = np.cumsum([0, FOX_W, FOX_W, FOX_W, FOX_HEADS, S5_W, GLA_QK, GLA_QK, GLA_W, GLA_RANK, GLA_W]).tolist()
    col = lambda i: w_in[:, c[i]:c[i + 1]]
    w = {}
    w["wrest"] = jnp.concatenate([col(4), col(5), col(6), col(7), col(9)], axis=1)
    w["wtail"] = jnp.concatenate(
        [col(3), col(8), jnp.zeros((D_MODEL, TAIL_W - FOX_HEADS - GLA_RANK), BF16)], axis=1)
    w["tail_bias"] = jnp.concatenate([p["fox_f_bias"][l], jnp.zeros((TAIL_W - FOX_HEADS,), F32)])
    w["q_gain"] = jnp.tile(p["fox_q_gain"][l], FOX_HEADS)
    w["k_gain"] = jnp.tile(p["fox_k_gain"][l], FOX_HEADS)
    lam_re, lam_im, bb_re, bb_im = s5_discretise(
        p["s5_a_re"][l], p["s5_a_im"][l], p["s5_log_dt"][l],
        jnp.swapaxes(p["s5_b_re"][l], 1, 2), jnp.swapaxes(p["s5_b_im"][l], 1, 2))
    gpb = S5_GROUPS // S5_BLOCKS
    w["s5_lam_re"], w["s5_lam_im"] = lam_re.reshape(1, S5_NS), lam_im.reshape(1, S5_NS)
    w["s5_wb_f32"] = jnp.concatenate([_block_diag(bb_re, gpb), _block_diag(bb_im, gpb)], axis=2)
    w["s5_wb"] = w["s5_wb_f32"].astype(BF16)
    c_re_t = jnp.swapaxes(p["s5_c_re"][l], 1, 2)
    c_im_t = jnp.swapaxes(p["s5_c_im"][l], 1, 2)
    w["s5_wc_f32"] = jnp.concatenate([_block_diag(c_re_t, gpb), -_block_diag(c_im_t, gpb)], axis=1)
    w["s5_wc"] = w["s5_wc_f32"].astype(BF16)
    w["s5_wglu"] = p["s5_w_glu"][l].astype(BF16)
    w["gla_wa_f32"] = jnp.concatenate(
        [jnp.zeros((FOX_HEADS, GLA_QK), F32), p["gla_w_a2"][l],
         jnp.zeros((TAIL_W - FOX_HEADS - GLA_RANK, GLA_QK), F32)], axis=0)
    w["gla_wa"] = w["gla_wa_f32"].astype(BF16)
    w["mem_q_gain"] = jnp.tile(p["mem_q_gain"][l], MEM_HEADS)
    w["mem_k_gain"] = jnp.tile(p["mem_k_gain"][l], MEM_HEADS)
    w["w_router"] = jnp.concatenate(
        [p["moe_w_expert"][l], p["moe_w_group"][l],
         jnp.zeros((D_MODEL, LANES - N_EXPERTS - N_GROUPS), F32)], axis=1)
    w["b_router"] = jnp.concatenate(
        [p["moe_b_expert"][l], p["moe_b_group"][l], jnp.zeros((LANES - N_EXPERTS - N_GROUPS,), F32)]).reshape(1, LANES)
    return w


def _in_projection(h, w, pb, l, tm):
    tn = PROJ_TN
    (q_b,) = matmul(h, pb["w_in"], tm=tm, tn=tn, out_dtypes=(BF16,), gain=w["q_gain"],
                    post_scale=FOX_HD ** -0.5 * LOG2E, layer=l, cols=(0, FOX_W), name="proj_q")
    k_f, k_b = matmul(h, pb["w_in"], tm=tm, tn=tn, out_dtypes=(F32, BF16), gain=w["k_gain"], layer=l,
                      cols=(FOX_W, FOX_W), name="proj_k")
    v_f, v_b = matmul(h, pb["w_in"], tm=tm, tn=tn, out_dtypes=(F32, BF16), layer=l, cols=(2 * FOX_W, FOX_W),
                      name="proj_v")
    (rest,) = matmul(h, w["wrest"], tm=tm, tn=tn, name="proj_rest")
    (tail,) = matmul(h, w["wtail"], tm=tm, tn=TAIL_W, logsig_bias=w["tail_bias"], logsig_lanes=FOX_HEADS,
                     name="proj_tail")
    return q_b, k_f, k_b, v_f, v_b, rest, tail


def _post_mixer(x, branches, mk, mv, w, p, pb, l, b, t, tm, tq_mem, g_next):
    n = x.shape[0]
    tm_full = min(256, n)
    x, hq = proj_res_norm(branches, pb["w_out"], l, x, p["g_mem_pre"][l], tm_full, name="proj_out")
    (q,) = matmul(hq, pb["mem_w_q"], tm=tm, tn=512, out_dtypes=(BF16,), gain=w["mem_q_gain"],
                  post_scale=MEM_HD ** -0.5, layer=l, name="mem_q")
    o = memory_attention(q.reshape(b, t, MEM_W), mk, mv, tq_mem).reshape(n, MEM_W)
    x, hm = proj_res_norm([o], pb["mem_w_o"], l, x, p["g_moe"][l], tm_full, name="mem_o")
    gate = moe_router(x, p["g_moe"][l], w["w_router"], w["b_router"], tm)
    out = moe_experts(hm, gate, pb["moe_w_gate"], pb["moe_w_up"], pb["moe_w_down"], l, x, tm, g_next=g_next)
    return out if g_next is not None else (out, None)


def kernel(x_prompt, x_sample, cache_fox_k, cache_fox_v, cache_fox_logf, state_s5_re, state_s5_im, state_gla,
           cache_mem_k, cache_mem_v, page_table, mem_prompt, g_mix, w_in, fox_q_gain, fox_k_gain, fox_f_bias,
           s5_a_re, s5_a_im, s5_log_dt, s5_b_re, s5_b_im, s5_c_re, s5_c_im, s5_d, s5_w_glu, s5_b_glu,
           gla_w_a2, gla_b_a, gla_gain, g_fox_out, g_s5_out, w_out, g_mem_pre, g_mem_tok, mem_w_q, mem_w_k,
           mem_w_v, mem_q_gain, mem_k_gain, mem_w_o, g_moe, moe_w_group, moe_b_group, moe_w_expert,
           moe_b_expert, moe_w_gate, moe_w_up, moe_w_down):
    p = dict(g_mix=g_mix, w_in=w_in, fox_q_gain=fox_q_gain, fox_k_gain=fox_k_gain, fox_f_bias=fox_f_bias,
             s5_a_re=s5_a_re, s5_a_im=s5_a_im, s5_log_dt=s5_log_dt, s5_b_re=s5_b_re, s5_b_im=s5_b_im,
             s5_c_re=s5_c_re, s5_c_im=s5_c_im, s5_d=s5_d, s5_w_glu=s5_w_glu, s5_b_glu=s5_b_glu,
             gla_w_a2=gla_w_a2, gla_b_a=gla_b_a, gla_gain=gla_gain, g_fox_out=g_fox_out, g_s5_out=g_s5_out,
             w_out=w_out, g_mem_pre=g_mem_pre, g_mem_tok=g_mem_tok, mem_w_q=mem_w_q, mem_w_k=mem_w_k,
             mem_w_v=mem_w_v, mem_q_gain=mem_q_gain, mem_k_gain=mem_k_gain, mem_w_o=mem_w_o, g_moe=g_moe,
             moe_w_group=moe_w_group, moe_b_group=moe_b_group, moe_w_expert=moe_w_expert,
             moe_b_expert=moe_b_expert, moe_w_gate=moe_w_gate, moe_w_up=moe_w_up, moe_w_down=moe_w_down)
    depth = w_in.shape[0]
    b, t, d = x_prompt.shape
    db = x_sample.shape[0]
    n = b * t
    n_mem = b * MEM_TOKENS
    pb = {name: p[name].astype(BF16) for name in _BF16_STACKS}
    w_in_t = jnp.swapaxes(w_in, 1, 2)
    tm = min(512, n)
    tm_proj = min(1024, n)
    tq_fox = min(512, t)
    s5_rows = min(256, t)
    gla_rows = min(256, t)
    tq_mem = min(512, t)

    xp = x_prompt.reshape(n, d)
    xs = jnp.pad(x_sample.reshape(db, d), ((0, SAMPLE_ROWS - db), (0, 0)))
    mem2d = mem_prompt.reshape(n_mem, d)
    cache_ft = jnp.swapaxes(cache_fox_logf, 2, 3)
    zero_s5 = jnp.zeros((b, 1, S5_NS), F32)
    zero_gla = jnp.zeros((b, GLA_HEADS, GLA_DV, GLA_DK), F32)
    outs = {k: [] for k in ("pk", "pv", "pf", "sk", "sv", "sf", "p5r", "p5i", "s5r", "s5i", "pg", "sg", "pmk", "pmv")}

    h_prompt = rmsnorm_cast(xp, g_mix[0], tm)
    for l in range(depth):
        w = _layer_weights(l, p, pb)

        q_b, k_f, k_b, v_f, v_b, rest, tail = _in_projection(h_prompt, w, pb, l, tm_proj)
        crow, ccol = fox_gate_cumsum(tail, b, t)
        fox_o = fox_prompt_attention(q_b, k_b, v_b, ccol, crow, g_fox_out[l], b, t, tq_fox)
        rest3, tail3 = rest.reshape(b, t, 4 * S5_W), tail.reshape(b, t, TAIL_W)
        seg_steps = s5_rows // S5_SCAN_BLOCK
        su = rest3[:, :, :S5_W].reshape(b, t // s5_rows, S5_SCAN_BLOCK, seg_steps, S5_W)
        su = jnp.swapaxes(su, 2, 3).reshape(b, t, S5_W)
        s5_o, hre, him = s5_mixer(su, zero_s5, zero_s5, w["s5_lam_re"], w["s5_lam_im"], w["s5_wb"], w["s5_wc"],
                                  s5_d[l], w["s5_wglu"], s5_b_glu[l], g_s5_out[l],
                                  nseq=b, rows=s5_rows, nsteps=t // s5_rows, scan=True, col_block=0)
        s5_o = s5_o.reshape(b, t // s5_rows, seg_steps, S5_SCAN_BLOCK, S5_W)
        s5_o = jnp.swapaxes(s5_o, 2, 3).reshape(b, t, S5_W)
        gla_o, st = gla_mixer(rest3, tail3, w["gla_wa"], gla_b_a[l], gla_gain[l], zero_gla,
                              nseq=b, rows=gla_rows, nsteps=t // gla_rows, t_valid=gla_rows)
        branches = [fox_o, s5_o.reshape(n, S5_W), gla_o.reshape(n, GLA_W)]
        outs["pk"].append(k_f.reshape(b, t, FOX_HEADS, FOX_HD))
        outs["pv"].append(v_f.reshape(b, t, FOX_HEADS, FOX_HD))
        outs["pf"].append(tail3[:, :, :FOX_HEADS])
        outs["p5r"].append(hre.reshape(b, S5_GROUPS, S5_STATE))
        outs["p5i"].append(him.reshape(b, S5_GROUPS, S5_STATE))
        outs["pg"].append(jnp.swapaxes(st, 2, 3))
        hmem = rmsnorm_cast(mem2d, g_mem_tok[l], tm=min(512, n_mem))
        (mk,) = matmul(hmem, pb["mem_w_k"], tm=min(512, n_mem), tn=512, gain=w["mem_k_gain"], layer=l, name="mem_k")
        (mv,) = matmul(hmem, pb["mem_w_v"], tm=min(512, n_mem), tn=512, layer=l, name="mem_v")
        outs["pmk"].append(mk.reshape(b, MEM_TOKENS, MEM_HEADS, MEM_HD))
        outs["pmv"].append(mv.reshape(b, MEM_TOKENS, MEM_HEADS, MEM_HD))
        xp, h_prompt = _post_mixer(xp, branches, mk.reshape(b, MEM_TOKENS, MEM_W), mv.reshape(b, MEM_TOKENS, MEM_W),
                                   w, p, pb, l, b, t, tm, tq_mem, g_mix[l + 1] if l + 1 < depth else None)

        ns = SAMPLE_ROWS
        h = rmsnorm_cast(xs, g_mix[l], ns, out_dtype=F32)
        c0 = np.cumsum([0, FOX_W, FOX_W, FOX_W, FOX_HEADS, S5_W, GLA_QK, GLA_QK, GLA_W, GLA_RANK, GLA_W]).tolist()
        in_cols = c0[-1]
        qk_gain = jnp.concatenate([w["q_gain"], w["k_gain"]]).reshape(1, 2 * FOX_W)
        f_bias = jnp.concatenate([fox_f_bias[l], jnp.zeros((in_cols - 3 * FOX_W - FOX_HEADS,), F32)]).reshape(1, -1)
        proj = pmatmul(h, w_in_t, l, tk=256, epi="inproj", aux=(qk_gain, f_bias), w_cols_first=True,
                       name="proj_in_sample")
        col = lambda i: proj[:, c0[i]:c0[i + 1]]
        hd3 = lambda a: a[:db].reshape(db, FOX_HEADS, FOX_HD)
        q_f, k_f, v_f, logf_new = col(0), col(1), col(2), col(3)[:db]
        rest = jnp.concatenate([col(4), col(5), col(6), col(7), col(9)], axis=1)
        tail = jnp.concatenate([col(3), col(8), jnp.zeros((ns, TAIL_W - FOX_HEADS - GLA_RANK), F32)], axis=1)
        fox_o = fox_sample_attention(l, hd3(q_f), hd3(k_f), hd3(v_f), logf_new.reshape(db, FOX_HEADS, 1),
                                     cache_fox_k, cache_fox_v, cache_ft, page_table, g_fox_out[l],
                                     pages_per_step=DECODE_PAGES_PER_STEP)
        fox_o = jnp.pad(fox_o.reshape(db, FOX_W), ((0, ns - db), (0, 0)))
        pad_state = lambda a: jnp.pad(a.reshape(1, db, S5_NS), ((0, 0), (0, ns - db), (0, 0)))
        s5_o, hre, him = s5_mixer(rest.reshape(1, ns, 4 * S5_W), pad_state(state_s5_re[l]), pad_state(state_s5_im[l]),
                                  w["s5_lam_re"], w["s5_lam_im"], w["s5_wb_f32"], w["s5_wc_f32"], s5_d[l],
                                  s5_w_glu[l], s5_b_glu[l], g_s5_out[l], nseq=1, rows=ns, nsteps=1, scan=False,
                                  col_block=0, precise=True)
        seq_pad = lambda a: jnp.pad(a[:db].reshape(db, 1, a.shape[1]), ((0, 0), (0, GLA_CHUNK - 1), (0, 0)))
        gla_o, st = gla_mixer(seq_pad(rest), seq_pad(tail), w["gla_wa_f32"], gla_b_a[l], gla_gain[l],
                              jnp.swapaxes(state_gla[l], 2, 3), nseq=db, rows=GLA_CHUNK, nsteps=1, t_valid=1,
                              precise=True)
        gla_o = jnp.pad(gla_o[:, 0, :], ((0, ns - db), (0, 0)))
        cat = jnp.concatenate([fox_o, s5_o.reshape(ns, S5_W), gla_o], axis=1)
        outs["sk"].append(hd3(k_f).reshape(db, 1, FOX_HEADS, FOX_HD))
        outs["sv"].append(hd3(v_f).reshape(db, 1, FOX_HEADS, FOX_HD))
        outs["sf"].append(logf_new.reshape(db, 1, FOX_HEADS))
        outs["s5r"].append(hre[0, :db].reshape(db, S5_GROUPS, S5_STATE))
        outs["s5i"].append(him[0, :db].reshape(db, S5_GROUPS, S5_STATE))
        outs["sg"].append(jnp.swapaxes(st, 2, 3))
        xs = _post_mixer_sample(xs, cat, cache_mem_k[l].reshape(db, MEM_TOKENS, MEM_W),
                                cache_mem_v[l].reshape(db, MEM_TOKENS, MEM_W), w, p, l, db)

    stk = lambda k: jnp.stack(outs[k])
    return (xp.reshape(b, t, d), xs[:db].reshape(db, 1, d),
            stk("pk"), stk("pv"), stk("pf"), stk("sk"), stk("sv"), stk("sf"),
            stk("p5r"), stk("p5i"), stk("s5r"), stk("s5i"), stk("pg"), stk("sg"), stk("pmk"), stk("pmv"))


def _post_mixer_sample(x, cat, mk, mv, w, p, l, db):
    ns = x.shape[0]
    x = pmatmul(cat, p["w_out"], l, tk=512, epi="res", aux=(x,), name="proj_out_sample")
    hq = rmsnorm_cast(x, p["g_mem_pre"][l], ns, out_dtype=F32)
    q = pmatmul(hq, p["mem_w_q"], l, tk=512, epi="headnorm", aux=(w["mem_q_gain"].reshape(1, MEM_W),),
                post_scale=MEM_HD ** -0.5, name="mem_q_sample")
    q3 = jnp.pad(q[:db].reshape(db, 1, MEM_W), ((0, 0), (0, SAMPLE_ROWS - 1), (0, 0)))
    o = memory_attention(q3, mk, mv, SAMPLE_ROWS, precise=True)[:, 0, :]
    o = jnp.pad(o, ((0, ns - db), (0, 0)))
    x = pmatmul(o, p["mem_w_o"], l, tk=MEM_W, epi="res", aux=(x,), name="mem_o_sample")
    hm = rmsnorm_cast(x, p["g_moe"][l], ns, out_dtype=F32)
    gate = pmatmul(hm, w["w_router"][None], 0, tk=512, epi="router", aux=(w["b_router"],), name="router_sample")
    return moe_experts(hm, gate, p["moe_w_gate"], p["moe_w_up"], p["moe_w_down"], l, x, ns, precise=True)
```

```python
import functools
import math

import numpy as np
import jax
import jax.numpy as jnp
from jax import lax
from jax.experimental import pallas as pl
from jax.experimental.pallas import tpu as pltpu

F32, BF16 = jnp.float32, jnp.bfloat16

D_MODEL = 2048
FOX_HEADS, FOX_HD = 8, 128
FOX_W = FOX_HEADS * FOX_HD
S5_W, S5_GROUP, S5_GROUPS, S5_STATE = 512, 16, 32, 64
S5_NS = S5_GROUPS * S5_STATE
S5_BLOCKS = 4
S5_BC = S5_W // S5_BLOCKS
S5_BS = S5_NS // S5_BLOCKS
S5_SCAN_BLOCK = 8
GLA_HEADS, GLA_DK, GLA_DV = 4, 64, 128
GLA_W = GLA_HEADS * GLA_DV
GLA_QK = GLA_HEADS * GLA_DK
GLA_RANK = 16
GLA_TAU = 16.0
GLA_CHUNK = 16
MEM_TOKENS, MEM_HEADS, MEM_HD = 256, 4, 128
MEM_W = MEM_HEADS * MEM_HD
N_GROUPS, EXP_PER_GROUP, N_EXPERTS, D_EXPERT = 4, 4, 16, 256
PAGE_SIZE = 128
EPS = 1e-6
NEG = -1e30
LOG2E = math.log2(math.e)

LANES = 128
VMEM_LIMIT_BYTES = 48 * 1024 * 1024

PROJ_TN = 512
MOE_EXPERTS_PER_STEP = 2
DECODE_GROUP = 16
DECODE_PAGES_PER_STEP = 8
TAIL_W = LANES
SAMPLE_ROWS = 16


def _params(*sem):
    return pltpu.CompilerParams(dimension_semantics=sem, vmem_limit_bytes=VMEM_LIMIT_BYTES)


def _sds(shape, dtype):
    return jax.ShapeDtypeStruct(shape, dtype)


_NN = (((1,), (0,)), ((), ()))
_NT = (((1,), (1,)), ((), ()))
_TN = (((0,), (0,)), ((), ()))


def _mxu(a, b, precise, dims=_NN):
    if precise:
        a, b = a.astype(F32), b.astype(F32)
        ah, bh = a.astype(BF16), b.astype(BF16)
        al, bl = (a - ah.astype(F32)).astype(BF16), (b - bh.astype(F32)).astype(BF16)
        return (lax.dot_general(ah, bh, dims, preferred_element_type=F32)
                + lax.dot_general(al, bh, dims, preferred_element_type=F32)
                + lax.dot_general(ah, bl, dims, preferred_element_type=F32))
    return lax.dot_general(a.astype(BF16), b.astype(BF16), dims, preferred_element_type=F32)


def _rms_kernel(x_ref, g_ref, o_ref):
    x = x_ref[...]
    y = x * lax.rsqrt(jnp.mean(x * x, axis=-1, keepdims=True) + EPS)
    o_ref[...] = (y * g_ref[...]).astype(o_ref.dtype)


def rmsnorm_cast(x, g, tm, out_dtype=BF16):
    n, d = x.shape
    return pl.pallas_call(
        _rms_kernel, grid=(n // tm,),
        in_specs=[pl.BlockSpec((tm, d), lambda i: (i, 0)), pl.BlockSpec((1, d), lambda i: (0, 0))],
        out_specs=pl.BlockSpec((tm, d), lambda i: (i, 0)),
        out_shape=_sds((n, d), out_dtype), compiler_params=_params("parallel"), name="rmsnorm_cast",
    )(x, g.reshape(1, d))


def _mm_kernel(*refs, headnorm, post_scale, logsig_lanes, has_res, n_out):
    x_ref, w_ref = refs[0], refs[1]
    pos = 2
    gain_ref = bias_ref = res_ref = None
    if headnorm:
        gain_ref, pos = refs[pos], pos + 1
    if logsig_lanes:
        bias_ref, pos = refs[pos], pos + 1
    if has_res:
        res_ref, pos = refs[pos], pos + 1
    outs = refs[pos:pos + n_out]
    acc = jnp.dot(x_ref[...], w_ref[...], preferred_element_type=F32)
    if logsig_lanes:
        lane = lax.broadcasted_iota(jnp.int32, acc.shape, 1)
        acc = jnp.where(lane < logsig_lanes, jax.nn.log_sigmoid(acc + bias_ref[...]), acc)
    if has_res:
        acc = res_ref[...] + acc
    if headnorm:
        for c in range(acc.shape[1] // LANES):
            sl = slice(c * LANES, (c + 1) * LANES)
            blk = acc[:, sl]
            y = blk * lax.rsqrt(jnp.mean(blk * blk, axis=-1, keepdims=True) + EPS)
            y = y * gain_ref[:, sl]
            if post_scale != 1.0:
                y = y * post_scale
            for o in outs:
                o[:, sl] = y.astype(o.dtype)
    else:
        for o in outs:
            o[...] = acc.astype(o.dtype)


def matmul(x, w, *, tm, tn, out_dtypes=(F32,), gain=None, post_scale=1.0, logsig_bias=None,
           logsig_lanes=0, res=None, layer=None, cols=None, name="matmul"):
    n, k = x.shape
    col0, c = cols if cols is not None else (0, w.shape[-1])
    jb = col0 // tn
    args = [x, w]
    if layer is None:
        w_spec = pl.BlockSpec((k, tn), lambda i, j: (0, jb + j))
    else:
        w_spec = pl.BlockSpec((None, k, tn), lambda i, j: (layer, 0, jb + j))
    in_specs = [pl.BlockSpec((tm, k), lambda i, j: (i, 0)), w_spec]
    if gain is not None:
        args.append(gain.reshape(1, c).astype(F32))
        in_specs.append(pl.BlockSpec((1, tn), lambda i, j: (0, j)))
    if logsig_lanes:
        args.append(logsig_bias.reshape(1, c).astype(F32))
        in_specs.append(pl.BlockSpec((1, tn), lambda i, j: (0, j)))
    if res is not None:
        args.append(res)
        in_specs.append(pl.BlockSpec((tm, tn), lambda i, j: (i, j)))
    outs = pl.pallas_call(
        functools.partial(_mm_kernel, headnorm=gain is not None, post_scale=post_scale,
                          logsig_lanes=logsig_lanes, has_res=res is not None, n_out=len(out_dtypes)),
        grid=(n // tm, c // tn), in_specs=in_specs,
        out_specs=[pl.BlockSpec((tm, tn), lambda i, j: (i, j)) for _ in out_dtypes],
        out_shape=[_sds((n, c), dt) for dt in out_dtypes],
        compiler_params=_params("parallel", "parallel"), name=name,
    )(*args)
    return outs


def _proj_norm_kernel(*refs, ks):
    nx = len(ks)
    x_refs, w_ref, res_ref, g_ref, o_ref, h_ref = refs[:nx], refs[nx], refs[nx + 1], refs[nx + 2], refs[nx + 3], refs[nx + 4]
    acc, off = None, 0
    for x_ref, k in zip(x_refs, ks):
        part = jnp.dot(x_ref[...], w_ref[off:off + k, :], preferred_element_type=F32)
        acc = part if acc is None else acc + part
        off += k
    x_new = res_ref[...] + acc
    o_ref[...] = x_new
    y = x_new * lax.rsqrt(jnp.mean(x_new * x_new, axis=-1, keepdims=True) + EPS)
    h_ref[...] = (y * g_ref[...]).astype(h_ref.dtype)


def proj_res_norm(xs, w, layer, res, g, tm, name):
    n, d = res.shape
    ks = tuple(x.shape[1] for x in xs)
    ktot = sum(ks)
    in_specs = [pl.BlockSpec((tm, k), lambda i: (i, 0)) for k in ks]
    in_specs += [pl.BlockSpec((None, ktot, d), lambda i: (layer, 0, 0)), pl.BlockSpec((tm, d), lambda i: (i, 0)),
                 pl.BlockSpec((1, d), lambda i: (0, 0))]
    return pl.pallas_call(
        functools.partial(_proj_norm_kernel, ks=ks), grid=(n // tm,), in_specs=in_specs,
        out_specs=[pl.BlockSpec((tm, d), lambda i: (i, 0)), pl.BlockSpec((tm, d), lambda i: (i, 0))],
        out_shape=[_sds((n, d), F32), _sds((n, d), BF16)], compiler_params=_params("parallel"), name=name,
    )(*xs, w, res, g.reshape(1, d))


def _head_rmsnorm(blk, gain):
    return blk * lax.rsqrt(jnp.mean(blk * blk, axis=-1, keepdims=True) + EPS) * gain


def _pmm_kernel(*refs, epi, n_aux, post_scale, w_cols_first):
    x_ref, w_ref = refs[0], refs[1]
    aux = refs[2:2 + n_aux]
    o_ref, acc_s = refs[2 + n_aux], refs[3 + n_aux]
    k = pl.program_id(0)

    @pl.when(k == 0)
    def _():
        acc_s[...] = jnp.zeros(acc_s.shape, F32)

    acc_s[...] += _mxu(x_ref[...], w_ref[...], True, _NT if w_cols_first else _NN)

    @pl.when(k == pl.num_programs(0) - 1)
    def _():
        acc = acc_s[...]
        if epi == "none":
            o_ref[...] = acc
        elif epi == "res":
            o_ref[...] = aux[0][...] + acc
        elif epi == "headnorm":
            for c in range(acc.shape[1] // LANES):
                sl = slice(c * LANES, (c + 1) * LANES)
                o_ref[:, sl] = _head_rmsnorm(acc[:, sl], aux[0][:, sl]) * post_scale
        elif epi == "inproj":
            gain_ref, bias_ref = aux
            for c in range(2 * FOX_W // LANES):
                sl = slice(c * LANES, (c + 1) * LANES)
                o_ref[:, sl] = _head_rmsnorm(acc[:, sl], gain_ref[:, sl])
            o_ref[:, 2 * FOX_W:3 * FOX_W] = acc[:, 2 * FOX_W:3 * FOX_W]
            t = acc[:, 3 * FOX_W:]
            lane = lax.broadcasted_iota(jnp.int32, t.shape, 1)
            o_ref[:, 3 * FOX_W:] = jnp.where(lane < FOX_HEADS, jax.nn.log_sigmoid(t + bias_ref[...]), t)
        elif epi == "router":
            o_ref[...] = _router_gate(acc + aux[0][...])


def pmatmul(x, w, layer, *, tk, epi="none", aux=(), post_scale=1.0, w_cols_first=False, name="pmatmul"):
    rows, kdim = x.shape
    if w_cols_first:
        c = w.shape[1]
        w_spec = pl.BlockSpec((None, c, tk), lambda k: (layer, 0, k))
    else:
        c = w.shape[2]
        w_spec = pl.BlockSpec((None, tk, c), lambda k: (layer, k, 0))
    in_specs = [pl.BlockSpec((rows, tk), lambda k: (0, k)), w_spec]
    in_specs += [pl.BlockSpec(a.shape, lambda k: (0, 0)) for a in aux]
    return pl.pallas_call(
        functools.partial(_pmm_kernel, epi=epi, n_aux=len(aux), post_scale=post_scale, w_cols_first=w_cols_first),
        grid=(kdim // tk,),
        in_specs=in_specs, out_specs=pl.BlockSpec((rows, c), lambda k: (0, 0)),
        out_shape=_sds((rows, c), F32), scratch_shapes=[pltpu.VMEM((rows, c), F32)],
        compiler_params=_params("arbitrary"), name=name,
    )(x, w, *aux)


def _fox_gate_kernel(t_ref, crow_ref, ccol_ref):
    x = t_ref[0]
    c = x.T[0:FOX_HEADS, :]
    t_len = c.shape[1]
    lane = lax.broadcasted_iota(jnp.int32, c.shape, 1)
    s = 1
    while s < t_len:
        c = c + jnp.where(lane >= s, pltpu.roll(c, s, axis=1), 0.0)
        s *= 2
    c = c * LOG2E
    crow_ref[0] = c
    cpad = jnp.concatenate([c, jnp.zeros((LANES - FOX_HEADS, t_len), F32)], axis=0)
    ccol_ref[0] = cpad.T


def fox_gate_cumsum(tail, b, t):
    return pl.pallas_call(
        _fox_gate_kernel, grid=(b,),
        in_specs=[pl.BlockSpec((1, t, TAIL_W), lambda i: (i, 0, 0))],
        out_specs=[pl.BlockSpec((1, FOX_HEADS, t), lambda i: (i, 0, 0)),
                   pl.BlockSpec((1, t, TAIL_W), lambda i: (i, 0, 0))],
        out_shape=[_sds((b, FOX_HEADS, t), F32), _sds((b, t, TAIL_W), F32)],
        compiler_params=_params("parallel"), name="fox_gate_cumsum",
    )(tail.reshape(b, t, TAIL_W))


def _lane_tile(x, n):
    return x if n == 1 else jnp.concatenate([x] * n, axis=1)


def _fox_attn_kernel(qi_tab, ki_tab, q_ref, k_ref, v_ref, cq_ref, ck_ref, g_ref, o_ref, m_s, l_s, acc_s, cq_s,
                     *, tq, tk):
    p_id = pl.program_id(1)
    qi, ki = qi_tab[p_id], ki_tab[p_id]
    nrep = tk // LANES

    @pl.when(ki == 0)
    def _():
        m_s[...] = jnp.full(m_s.shape, NEG, F32)
        l_s[...] = jnp.zeros(l_s.shape, F32)
        acc_s[...] = jnp.zeros(acc_s.shape, F32)
        for h in range(FOX_HEADS):
            cq_s[h] = jnp.broadcast_to(cq_ref[0, :, h:h + 1], (tq, LANES))

    def step(masked):
        if masked:
            row = lax.broadcasted_iota(jnp.int32, (tq, tk), 0)
            col = lax.broadcasted_iota(jnp.int32, (tq, tk), 1)
            causal = col <= row
        for h in range(FOX_HEADS):
            sl = slice(h * FOX_HD, (h + 1) * FOX_HD)
            s = lax.dot_general(q_ref[0, :, sl], k_ref[0, :, sl], _NT, preferred_element_type=F32)
            s = s - ck_ref[0, h:h + 1, :]
            if masked:
                s = jnp.where(causal, s, NEG)
            c_q = cq_s[h]
            m_old = m_s[h]
            m_new = jnp.maximum(m_old, jnp.broadcast_to(jnp.max(s, axis=-1, keepdims=True), (tq, LANES)) + c_q)
            alpha = jnp.exp2(m_old - m_new)
            p = jnp.exp2(s + _lane_tile(c_q - m_new, nrep))
            l_s[h] = l_s[h] * alpha + jnp.broadcast_to(jnp.sum(p, axis=-1, keepdims=True), (tq, LANES))
            m_s[h] = m_new
            acc_s[:, sl] = acc_s[:, sl] * alpha + jnp.dot(p.astype(BF16), v_ref[0, :, sl],
                                                          preferred_element_type=F32)

    @pl.when(ki < qi)
    def _():
        step(False)

    @pl.when(ki == qi)
    def _():
        step(True)
        ssq = jnp.zeros((tq, 1), F32)
        for h in range(FOX_HEADS):
            sl = slice(h * FOX_HD, (h + 1) * FOX_HD)
            o = acc_s[:, sl] / l_s[h]
            acc_s[:, sl] = o
            ssq = ssq + jnp.sum(o * o, axis=-1, keepdims=True)
        inv = lax.rsqrt(ssq / FOX_W + EPS)
        o_ref[0] = (acc_s[...] * inv * g_ref[...]).astype(o_ref.dtype)


def fox_prompt_attention(q, k, v, ccol, crow, g_out, b, t, tq):
    nq = t // tq
    pairs = [(i, j) for i in range(nq) for j in range(i + 1)]
    qi_tab = jnp.asarray(np.array([p[0] for p in pairs], np.int32))
    ki_tab = jnp.asarray(np.array([p[1] for p in pairs], np.int32))
    q3, k3, v3 = (a.reshape(b, t, FOX_W) for a in (q, k, v))
    grid_spec = pltpu.PrefetchScalarGridSpec(
        num_scalar_prefetch=2, grid=(b, len(pairs)),
        in_specs=[
            pl.BlockSpec((1, tq, FOX_W), lambda bi, p, qt, kt: (bi, qt[p], 0)),
            pl.BlockSpec((1, tq, FOX_W), lambda bi, p, qt, kt: (bi, kt[p], 0)),
            pl.BlockSpec((1, tq, FOX_W), lambda bi, p, qt, kt: (bi, kt[p], 0)),
            pl.BlockSpec((1, tq, TAIL_W), lambda bi, p, qt, kt: (bi, qt[p], 0)),
            pl.BlockSpec((1, FOX_HEADS, tq), lambda bi, p, qt, kt: (bi, 0, kt[p])),
            pl.BlockSpec((1, FOX_W), lambda bi, p, qt, kt: (0, 0)),
        ],
        out_specs=pl.BlockSpec((1, tq, FOX_W), lambda bi, p, qt, kt: (bi, qt[p], 0)),
        scratch_shapes=[pltpu.VMEM((FOX_HEADS, tq, LANES), F32), pltpu.VMEM((FOX_HEADS, tq, LANES), F32),
                        pltpu.VMEM((tq, FOX_W), F32), pltpu.VMEM((FOX_HEADS, tq, LANES), F32)],
    )
    out = pl.pallas_call(
        functools.partial(_fox_attn_kernel, tq=tq, tk=tq), grid_spec=grid_spec,
        out_shape=_sds((b, t, FOX_W), BF16),
        compiler_params=_params("parallel", "arbitrary"), name="fox_prompt_attention",
    )(qi_tab, ki_tab, q3, k3, v3, ccol, crow, g_out.reshape(1, FOX_W))
    return out.reshape(b * t, FOX_W)


def _fox_decode_kernel(pt_ref, q_ref, kn_ref, vn_ref, fn_ref, g_ref, *refs, pages_per_step, scale):
    pp = pages_per_step
    k_refs, v_refs, f_refs = refs[0:pp], refs[pp:2 * pp], refs[2 * pp:3 * pp]
    o_ref = refs[3 * pp]
    m_s, l_s, acc_s, r_s = refs[3 * pp + 1:]
    step_id = pl.program_id(1)
    q = q_ref[0] * scale

    @pl.when(step_id == 0)
    def _():
        m_s[...] = jnp.sum(q * kn_ref[0], axis=-1, keepdims=True)
        l_s[...] = jnp.ones(l_s.shape, F32)
        acc_s[...] = vn_ref[0]
        r_s[...] = fn_ref[0]

    tok_r = lax.broadcasted_iota(jnp.int32, (PAGE_SIZE, PAGE_SIZE), 0)
    tok_c = lax.broadcasted_iota(jnp.int32, (PAGE_SIZE, PAGE_SIZE), 1)
    later = (tok_r > tok_c).astype(F32)
    r_run = r_s[...]
    parts = []
    grp = DECODE_GROUP
    tok = lax.broadcasted_iota(jnp.int32, (grp, 1, FOX_HD), 0)
    lane = lax.broadcasted_iota(jnp.int32, (grp, 1, FOX_HD), 2)
    for j in range(pp):
        f = f_refs[j][...]
        r = jnp.dot(f, later, preferred_element_type=F32, precision=lax.Precision.HIGHEST) + r_run
        r_run = r_run + jnp.sum(f, axis=-1, keepdims=True)
        for g in range(PAGE_SIZE // grp):
            rows = pl.ds(g * grp, grp)
            prod = k_refs[j][rows] * q[None] + jnp.where(lane == tok + g * grp, r[None], 0.0)
            s = jnp.sum(prod, axis=-1, keepdims=True)
            m_g = jnp.max(s, axis=0)
            p = jnp.exp(s - m_g[None])
            parts.append((m_g, jnp.sum(p, axis=0), jnp.sum(p * v_refs[j][rows], axis=0)))
    m_old = m_s[...]
    m_new = m_old
    for m_j, _, _ in parts:
        m_new = jnp.maximum(m_new, m_j)
    alpha = jnp.exp(m_old - m_new)
    l_run, acc = l_s[...] * alpha, acc_s[...] * alpha
    for m_j, l_j, pv_j in parts:
        a_j = jnp.exp(m_j - m_new)
        l_run, acc = l_run + l_j * a_j, acc + pv_j * a_j
    m_s[...], l_s[...], acc_s[...], r_s[...] = m_new, l_run, acc, r_run

    @pl.when(step_id == pl.num_programs(1) - 1)
    def _():
        o = acc_s[...] / l_s[...]
        ssq = jnp.sum(jnp.sum(o * o, axis=-1, keepdims=True), axis=0, keepdims=True)
        o_ref[0] = (o * lax.rsqrt(ssq / FOX_W + EPS) * g_ref[...]).astype(o_ref.dtype)


def fox_sample_attention(layer, q, k_new, v_new, logf_new, cache_k, cache_v, cache_ft, page_table, g_out,
                         pages_per_step):
    db, n_pages = page_table.shape
    n_steps = n_pages // pages_per_step

    def page_map(j):
        def im(bi, si, pt):
            return (layer, pt[bi, n_pages - 1 - (si * pages_per_step + j)], 0, 0, 0)
        return im

    def page_map_f(j):
        def im(bi, si, pt):
            return (layer, pt[bi, n_pages - 1 - (si * pages_per_step + j)], 0, 0)
        return im

    vec = pl.BlockSpec((1, FOX_HEADS, FOX_HD), lambda bi, si, pt: (bi, 0, 0))
    in_specs = [vec, vec, vec,
                pl.BlockSpec((1, FOX_HEADS, 1), lambda bi, si, pt: (bi, 0, 0)),
                pl.BlockSpec((FOX_HEADS, FOX_HD), lambda bi, si, pt: (0, 0))]
    in_specs += [pl.BlockSpec((None, None, PAGE_SIZE, FOX_HEADS, FOX_HD), page_map(j)) for j in range(pages_per_step)]
    in_specs += [pl.BlockSpec((None, None, PAGE_SIZE, FOX_HEADS, FOX_HD), page_map(j)) for j in range(pages_per_step)]
    in_specs += [pl.BlockSpec((None, None, FOX_HEADS, PAGE_SIZE), page_map_f(j)) for j in range(pages_per_step)]
    grid_spec = pltpu.PrefetchScalarGridSpec(
        num_scalar_prefetch=1, grid=(db, n_steps), in_specs=in_specs,
        out_specs=pl.BlockSpec((1, FOX_HEADS, FOX_HD), lambda bi, si, pt: (bi, 0, 0)),
        scratch_shapes=[pltpu.VMEM((FOX_HEADS, 1), F32), pltpu.VMEM((FOX_HEADS, 1), F32),
                        pltpu.VMEM((FOX_HEADS, FOX_HD), F32), pltpu.VMEM((FOX_HEADS, 1), F32)],
    )
    return pl.pallas_call(
        functools.partial(_fox_decode_kernel, pages_per_step=pages_per_step, scale=FOX_HD ** -0.5),
        grid_spec=grid_spec, out_shape=_sds((db, FOX_HEADS, FOX_HD), F32),
        compiler_params=_params("parallel", "arbitrary"), name="fox_sample_attention",
    )(page_table, q, k_new, v_new, logf_new, g_out.reshape(FOX_HEADS, FOX_HD),
      *([cache_k] * pages_per_step), *([cache_v] * pages_per_step), *([cache_ft] * pages_per_step))


def _s5_param_kernel(are_ref, aim_ref, ldt_ref, bre_ref, bim_ref, lre_ref, lim_ref, bbre_ref, bbim_ref):
    a_re, a_im = are_ref[...], aim_ref[...]
    dt = jnp.exp(ldt_ref[...])
    mag = jnp.exp(a_re * dt)
    lam_re, lam_im = mag * jnp.cos(a_im * dt), mag * jnp.sin(a_im * dt)
    den = a_re * a_re + a_im * a_im
    z_re = ((lam_re - 1.0) * a_re + lam_im * a_im) / den
    z_im = (lam_im * a_re - (lam_re - 1.0) * a_im) / den
    lre_ref[...] = lam_re
    lim_ref[...] = lam_im
    b_re, b_im = bre_ref[...], bim_ref[...]
    bbre_ref[...] = z_re[:, None, :] * b_re - z_im[:, None, :] * b_im
    bbim_ref[...] = z_re[:, None, :] * b_im + z_im[:, None, :] * b_re


def s5_discretise(a_re, a_im, log_dt, b_re_t, b_im_t):
    g, p = a_re.shape
    c = b_re_t.shape[1]
    return pl.pallas_call(
        _s5_param_kernel,
        out_shape=[_sds((g, p), F32), _sds((g, p), F32), _sds((g, c, p), F32), _sds((g, c, p), F32)],
        name="s5_discretise",
    )(a_re, a_im, log_dt.reshape(g, 1), b_re_t, b_im_t)


def _block_diag(w, groups_per_block):
    g, a, b = w.shape
    nb = g // groups_per_block
    eye = jnp.eye(groups_per_block, dtype=w.dtype)
    w5 = w.reshape(nb, groups_per_block, a, b)
    return jnp.einsum("jgab,gh->jgahb", w5, eye).reshape(nb, groups_per_block * a, groups_per_block * b)


def _row_tile(x, n):
    tile = jnp.broadcast_to(x, (S5_SCAN_BLOCK, x.shape[1]))
    return tile if n == S5_SCAN_BLOCK else jnp.concatenate([tile] * (n // S5_SCAN_BLOCK), axis=0)


def _s5_kernel(u_ref, h0re_ref, h0im_ref, lre_ref, lim_ref, wb_ref, wc_ref, d_ref, wglu_ref, bglu_ref, gout_ref,
               y_ref, hre_out, him_out, hre_s, him_s, cre_s, cim_s, y_s, *, rows, scan, precise):
    t_id = pl.program_id(1)
    lam_re, lam_im = lre_ref[...], lim_ref[...]

    @pl.when(t_id == 0)
    def _():
        cre_s[...] = h0re_ref[0]
        cim_s[...] = h0im_ref[0]

    u = u_ref[0]
    for j in range(S5_BLOCKS):
        bu = _mxu(u[:, j * S5_BC:(j + 1) * S5_BC], wb_ref[j], precise)
        hre_s[:, j * S5_BS:(j + 1) * S5_BS] = bu[:, :S5_BS]
        him_s[:, j * S5_BS:(j + 1) * S5_BS] = bu[:, S5_BS:]
    c_re, c_im = cre_s[...], cim_s[...]
    if scan:
        hre_s[0:1, :] = hre_s[0:1, :] + (lam_re * c_re - lam_im * c_im)
        him_s[0:1, :] = him_s[0:1, :] + (lam_re * c_im + lam_im * c_re)
        seg = S5_SCAN_BLOCK
        steps = rows // seg
        l_re, l_im = _row_tile(lam_re, seg), _row_tile(lam_im, seg)
        p_re, p_im = hre_s[0:seg, :], him_s[0:seg, :]
        for v in range(1, steps):
            sl = slice(v * seg, (v + 1) * seg)
            p_re, p_im = (hre_s[sl, :] + (l_re * p_re - l_im * p_im), him_s[sl, :] + (l_re * p_im + l_im * p_re))
            hre_s[sl, :] = p_re
            him_s[sl, :] = p_im
        f_re, f_im = lam_re, lam_im
        n = 1
        while n < steps:
            f_re, f_im = f_re * f_re - f_im * f_im, 2.0 * f_re * f_im
            n *= 2
        f_re, f_im = _row_tile(f_re, seg), _row_tile(f_im, seg)
        pos = lax.broadcasted_iota(jnp.int32, (seg, S5_NS), 0)
        e_re, e_im = p_re, p_im
        s = 1
        while s < seg:
            m_re, m_im = jnp.where(pos >= s, f_re, 0.0), jnp.where(pos >= s, f_im, 0.0)
            b_re, b_im = pltpu.roll(e_re, s, axis=0), pltpu.roll(e_im, s, axis=0)
            e_re, e_im = e_re + (m_re * b_re - m_im * b_im), e_im + (m_re * b_im + m_im * b_re)
            f_re, f_im = f_re * f_re - f_im * f_im, 2.0 * f_re * f_im
            s *= 2
        c_re = jnp.where(pos >= 1, pltpu.roll(e_re, 1, axis=0), 0.0)
        c_im = jnp.where(pos >= 1, pltpu.roll(e_im, 1, axis=0), 0.0)
        for v in range(steps):
            c_re, c_im = l_re * c_re - l_im * c_im, l_re * c_im + l_im * c_re
            sl = slice(v * seg, (v + 1) * seg)
            hre_s[sl, :] = hre_s[sl, :] + c_re
            him_s[sl, :] = him_s[sl, :] + c_im
        cre_s[...] = hre_s[rows - 1:rows, :]
        cim_s[...] = him_s[rows - 1:rows, :]
        hre_out[0] = hre_s[rows - 1:rows, :]
        him_out[0] = him_s[rows - 1:rows, :]
    else:
        hre_s[...] = hre_s[...] + (lam_re * c_re - lam_im * c_im)
        him_s[...] = him_s[...] + (lam_re * c_im + lam_im * c_re)
        hre_out[0] = hre_s[...]
        him_out[0] = him_s[...]

    for j in range(S5_BLOCKS):
        sl = slice(j * S5_BS, (j + 1) * S5_BS)
        yj = _mxu(hre_s[:, sl], wc_ref[j, 0:S5_BS, :], precise)
        yj = yj + _mxu(him_s[:, sl], wc_ref[j, S5_BS:2 * S5_BS, :], precise)
        y_s[:, j * S5_BC:(j + 1) * S5_BC] = yj
    y = y_s[...] + d_ref[...] * u
    y = jax.nn.gelu(y)
    z = _mxu(y, wglu_ref[...], precise) + bglu_ref[...]
    y = y * jax.nn.sigmoid(z)
    y = y * lax.rsqrt(jnp.mean(y * y, axis=-1, keepdims=True) + EPS) * gout_ref[...]
    y_ref[0] = y.astype(y_ref.dtype)


def s5_mixer(u, h0_re, h0_im, lam_re, lam_im, wb, wc, d_skip, w_glu, b_glu, g_out, *, nseq, rows, nsteps, scan,
             col_block, precise=False):
    st_rows = 1 if scan else rows
    full = lambda shape: pl.BlockSpec(shape, lambda b, t: tuple(0 for _ in shape))
    state_spec = pl.BlockSpec((1, st_rows, S5_NS), lambda b, t: (b, 0, 0))
    y, hre, him = pl.pallas_call(
        functools.partial(_s5_kernel, rows=rows, scan=scan, precise=precise), grid=(nseq, nsteps),
        in_specs=[pl.BlockSpec((1, rows, S5_W), lambda b, t: (b, t, col_block)),
                  state_spec, state_spec, full((1, S5_NS)), full((1, S5_NS)),
                  full((S5_BLOCKS, S5_BC, 2 * S5_BS)), full((S5_BLOCKS, 2 * S5_BS, S5_BC)),
                  full((1, S5_W)), full((S5_W, S5_W)), full((1, S5_W)), full((1, S5_W))],
        out_specs=[pl.BlockSpec((1, rows, S5_W), lambda b, t: (b, t, 0)), state_spec, state_spec],
        out_shape=[_sds((nseq, nsteps * rows, S5_W), F32 if precise else BF16),
                   _sds((nseq, st_rows, S5_NS), F32), _sds((nseq, st_rows, S5_NS), F32)],
        scratch_shapes=[pltpu.VMEM((rows, S5_NS), F32), pltpu.VMEM((rows, S5_NS), F32),
                        pltpu.VMEM((st_rows, S5_NS), F32), pltpu.VMEM((st_rows, S5_NS), F32),
                        pltpu.VMEM((rows, S5_W), F32)],
        compiler_params=_params("parallel", "arbitrary"), name="s5_mixer",
    )(u, h0_re, h0_im, lam_re, lam_im, wb, wc, d_skip.reshape(1, S5_W), w_glu, b_glu.reshape(1, S5_W),
      g_out.reshape(1, S5_W))
    return y, hre, him


def _gla_kernel(qk_ref, v_ref, go_ref, ga_ref, wa_ref, ba_ref, gain_ref, s0_ref, o_ref, st_out,
                st_s, o_s, *, rows, t_valid, precise):
    t_id = pl.program_id(1)

    @pl.when(t_id == 0)
    def _():
        st_s[...] = s0_ref[0]

    x = _mxu(ga_ref[0], wa_ref[...], precise) + ba_ref[...]
    la = jax.nn.log_sigmoid(x) / GLA_TAU
    row = lax.broadcasted_iota(jnp.int32, la.shape, 0)
    if t_valid < rows:
        la = jnp.where(row < t_valid, la, 0.0)
    pos = row & (GLA_CHUNK - 1)
    b, e = la, la
    s = 1
    while s < GLA_CHUNK:
        b = b + jnp.where(pos >= s, pltpu.roll(b, s, axis=0), 0.0)
        e = e + jnp.where(pos < GLA_CHUNK - s, pltpu.roll(e, rows - s, axis=0), 0.0)
        s *= 2
    b_last = b + e - la
    qk = qk_ref[0]
    q, k = qk[:, 0:GLA_QK] * (GLA_DK ** -0.5), qk[:, GLA_QK:2 * GLA_QK]
    qe = q * jnp.exp(b)
    ke = k * jnp.exp(-b)
    kd = k * jnp.exp(b_last - b)
    dec = jnp.exp(b_last)
    v = v_ref[0]
    if not precise:
        qe, ke, kd, v = qe.astype(BF16), ke.astype(BF16), kd.astype(BF16), v.astype(BF16)
    r_i = lax.broadcasted_iota(jnp.int32, (rows, rows), 0)
    c_i = lax.broadcasted_iota(jnp.int32, (rows, rows), 1)
    chunk_shift = GLA_CHUNK.bit_length() - 1
    causal = ((r_i >> chunk_shift) == (c_i >> chunk_shift)) & (c_i <= r_i)
    for h in range(GLA_HEADS):
        ks = slice(h * GLA_DK, (h + 1) * GLA_DK)
        vs = slice(h * GLA_DV, (h + 1) * GLA_DV)
        att = _mxu(qe[:, ks], ke[:, ks], precise, _NT)
        att = jnp.where(causal, att, 0.0)
        o_s[:, vs] = _mxu(att, v[:, vs], precise)

    state = [st_s[h] for h in range(GLA_HEADS)]
    for j in range(rows // GLA_CHUNK):
        rs = slice(j * GLA_CHUNK, (j + 1) * GLA_CHUNK)
        for h in range(GLA_HEADS):
            ks = slice(h * GLA_DK, (h + 1) * GLA_DK)
            vs = slice(h * GLA_DV, (h + 1) * GLA_DV)
            o_s[rs, vs] = o_s[rs, vs] + _mxu(qe[rs, ks], state[h], precise, _NT)
            upd = _mxu(v[rs, vs], kd[rs, ks], precise, _TN)
            state[h] = state[h] * dec[j * GLA_CHUNK:j * GLA_CHUNK + 1, ks] + upd
    for h in range(GLA_HEADS):
        st_s[h] = state[h]
    st_out[0] = st_s[...]

    go = go_ref[0]
    for h in range(GLA_HEADS):
        vs = slice(h * GLA_DV, (h + 1) * GLA_DV)
        o = o_s[:, vs]
        y = o * lax.rsqrt(jnp.mean(o * o, axis=-1, keepdims=True) + EPS) * gain_ref[...]
        o_ref[0, :, vs] = (y * jax.nn.silu(go[:, vs])).astype(o_ref.dtype)


def gla_mixer(rest, tail, wa_pad, b_a, gain, s0_t, *, nseq, rows, nsteps, t_valid, precise=False):
    full = lambda shape: pl.BlockSpec(shape, lambda b, t: tuple(0 for _ in shape))
    st_spec = pl.BlockSpec((1, GLA_HEADS, GLA_DV, GLA_DK), lambda b, t: (b, 0, 0, 0))
    o, st = pl.pallas_call(
        functools.partial(_gla_kernel, rows=rows, t_valid=t_valid, precise=precise), grid=(nseq, nsteps),
        in_specs=[pl.BlockSpec((1, rows, 2 * GLA_QK), lambda b, t: (b, t, 1)),
                  pl.BlockSpec((1, rows, GLA_W), lambda b, t: (b, t, 2)),
                  pl.BlockSpec((1, rows, GLA_W), lambda b, t: (b, t, 3)),
                  pl.BlockSpec((1, rows, TAIL_W), lambda b, t: (b, t, 0)),
                  full((TAIL_W, GLA_QK)), full((1, GLA_QK)), full((1, GLA_DV)), st_spec],
        out_specs=[pl.BlockSpec((1, rows, GLA_W), lambda b, t: (b, t, 0)), st_spec],
        out_shape=[_sds((nseq, nsteps * rows, GLA_W), F32 if precise else BF16),
                   _sds((nseq, GLA_HEADS, GLA_DV, GLA_DK), F32)],
        scratch_shapes=[pltpu.VMEM((GLA_HEADS, GLA_DV, GLA_DK), F32), pltpu.VMEM((rows, GLA_W), F32)],
        compiler_params=_params("parallel", "arbitrary"), name="gla_mixer",
    )(rest, rest, rest, tail, wa_pad, b_a.reshape(1, GLA_QK), gain.reshape(1, GLA_DV), s0_t)
    return o, st


def _mem_attn_kernel(q_ref, k_ref, v_ref, o_ref, *, precise):
    for h in range(MEM_HEADS):
        sl = slice(h * MEM_HD, (h + 1) * MEM_HD)
        s = _mxu(q_ref[0, :, sl], k_ref[0, :, sl], precise, _NT)
        p = jnp.exp(s - jnp.max(s, axis=-1, keepdims=True))
        o = _mxu(p, v_ref[0, :, sl], precise)
        o_ref[0, :, sl] = (o / jnp.sum(p, axis=-1, keepdims=True)).astype(o_ref.dtype)


def memory_attention(q, mk, mv, tq, precise=False):
    b, t, _ = q.shape
    return pl.pallas_call(
        functools.partial(_mem_attn_kernel, precise=precise), grid=(b, t // tq),
        in_specs=[pl.BlockSpec((1, tq, MEM_W), lambda bi, i: (bi, i, 0)),
                  pl.BlockSpec((1, MEM_TOKENS, MEM_W), lambda bi, i: (bi, 0, 0)),
                  pl.BlockSpec((1, MEM_TOKENS, MEM_W), lambda bi, i: (bi, 0, 0))],
        out_specs=pl.BlockSpec((1, tq, MEM_W), lambda bi, i: (bi, i, 0)),
        out_shape=_sds((b, t, MEM_W), F32 if precise else BF16), compiler_params=_params("parallel", "parallel"),
        name="memory_attention",
    )(q, mk, mv)


def _router_gate(logits):
    lane = lax.broadcasted_iota(jnp.int32, logits.shape, 1)
    is_group = (lane >= N_EXPERTS) & (lane < N_EXPERTS + N_GROUPS)
    lg = jnp.where(is_group, logits, NEG)
    eg = jnp.where(is_group, jnp.exp(lg - jnp.max(lg, axis=-1, keepdims=True)), 0.0)
    pg = eg / jnp.sum(eg, axis=-1, keepdims=True)
    g_w = jnp.max(pg, axis=-1, keepdims=True)
    lane_f = lane.astype(F32)
    far = float(4 * LANES)
    g_lane = jnp.min(jnp.where(is_group & (pg == g_w), lane_f, far), axis=-1, keepdims=True)
    group_of_lane = (lane >> (EXP_PER_GROUP.bit_length() - 1)).astype(F32)
    sel = (lane < N_EXPERTS) & (group_of_lane == g_lane - float(N_EXPERTS))
    le = jnp.where(sel, logits, NEG)
    ee = jnp.where(sel, jnp.exp(le - jnp.max(le, axis=-1, keepdims=True)), 0.0)
    pe = ee / jnp.sum(ee, axis=-1, keepdims=True)
    m1 = jnp.max(jnp.where(sel, pe, -1.0), axis=-1, keepdims=True)
    i1 = jnp.min(jnp.where(sel & (pe == m1), lane_f, far), axis=-1, keepdims=True)
    rest = sel & (lane_f != i1)
    m2 = jnp.max(jnp.where(rest, pe, -1.0), axis=-1, keepdims=True)
    i2 = jnp.min(jnp.where(rest & (pe == m2), lane_f, far), axis=-1, keepdims=True)
    tot = m1 + m2
    w = jnp.where(lane_f == i1, m1 / tot, jnp.where(lane_f == i2, m2 / tot, 0.0))
    return w * g_w


def _router_kernel(x_ref, g_ref, wh_ref, wl_ref, b_ref, gate_ref):
    x = x_ref[...]
    xn = x * lax.rsqrt(jnp.mean(x * x, axis=-1, keepdims=True) + EPS) * g_ref[...]
    xh = xn.astype(BF16)
    xl = (xn - xh.astype(F32)).astype(BF16)
    wh = wh_ref[...]
    logits = (jnp.dot(xh, wh, preferred_element_type=F32) + jnp.dot(xl, wh, preferred_element_type=F32)
              + jnp.dot(xh, wl_ref[...], preferred_element_type=F32))
    gate_ref[...] = _router_gate(logits + b_ref[...])


def moe_router(x, g, w_r, b_r, tm):
    n, d = x.shape
    w_hi = w_r.astype(BF16)
    w_lo = (w_r - w_hi.astype(F32)).astype(BF16)
    w_spec = pl.BlockSpec((d, LANES), lambda i: (0, 0))
    return pl.pallas_call(
        _router_kernel, grid=(n // tm,),
        in_specs=[pl.BlockSpec((tm, d), lambda i: (i, 0)), pl.BlockSpec((1, d), lambda i: (0, 0)),
                  w_spec, w_spec, pl.BlockSpec((1, LANES), lambda i: (0, 0))],
        out_specs=pl.BlockSpec((tm, LANES), lambda i: (i, 0)),
        out_shape=_sds((n, LANES), F32), compiler_params=_params("parallel"), name="moe_router",
    )(x, g.reshape(1, d), w_hi, w_lo, b_r)


def _expert_kernel(x_ref, gate_ref, wg_ref, wu_ref, wd_ref, res_ref, *rest, precise, next_norm):
    if next_norm:
        g_ref, o_ref, h_ref, acc_s = rest
    else:
        o_ref, acc_s = rest
    e = pl.program_id(1)

    @pl.when(e == 0)
    def _():
        acc_s[...] = jnp.zeros(acc_s.shape, F32)

    x = x_ref[...]
    gate = gate_ref[...]
    lane = lax.broadcasted_iota(jnp.int32, gate.shape, 1)
    parts = []
    for u in range(MOE_EXPERTS_PER_STEP):
        a = jax.nn.silu(_mxu(x, wg_ref[u], precise)) * _mxu(x, wu_ref[u], precise)
        g_col = jnp.sum(jnp.where(lane == e * MOE_EXPERTS_PER_STEP + u, gate, 0.0), axis=-1, keepdims=True)
        a = a * g_col
        parts.append(a if precise else a.astype(BF16))
    a_cat = parts[0] if len(parts) == 1 else jnp.concatenate(parts, axis=1)
    w_down = wd_ref[...].reshape(MOE_EXPERTS_PER_STEP * D_EXPERT, wd_ref.shape[-1])
    acc_s[...] += _mxu(a_cat, w_down, precise)

    @pl.when(e == pl.num_programs(1) - 1)
    def _():
        x_new = res_ref[...] + acc_s[...]
        o_ref[...] = x_new
        if next_norm:
            y = x_new * lax.rsqrt(jnp.mean(x_new * x_new, axis=-1, keepdims=True) + EPS)
            h_ref[...] = (y * g_ref[...]).astype(h_ref.dtype)


def moe_experts(x, gate, w_gate, w_up, w_down, layer, res, tm, precise=False, g_next=None):
    n, d = x.shape
    row = pl.BlockSpec((tm, d), lambda i, e: (i, 0))
    in_specs = [row, pl.BlockSpec((tm, LANES), lambda i, e: (i, 0)),
                pl.BlockSpec((None, MOE_EXPERTS_PER_STEP, d, D_EXPERT), lambda i, e: (layer, e, 0, 0)),
                pl.BlockSpec((None, MOE_EXPERTS_PER_STEP, d, D_EXPERT), lambda i, e: (layer, e, 0, 0)),
                pl.BlockSpec((None, MOE_EXPERTS_PER_STEP, D_EXPERT, d), lambda i, e: (layer, e, 0, 0)), row]
    args = [x, gate, w_gate, w_up, w_down, res]
    out_specs, out_shape = row, _sds((n, d), F32)
    if g_next is not None:
        in_specs.append(pl.BlockSpec((1, d), lambda i, e: (0, 0)))
        args.append(g_next.reshape(1, d))
        out_specs, out_shape = [row, row], [_sds((n, d), F32), _sds((n, d), BF16)]
    return pl.pallas_call(
        functools.partial(_expert_kernel, precise=precise, next_norm=g_next is not None),
        grid=(n // tm, N_EXPERTS // MOE_EXPERTS_PER_STEP),
        in_specs=in_specs, out_specs=out_specs, out_shape=out_shape, scratch_shapes=[pltpu.VMEM((tm, d), F32)],
        compiler_params=_params("parallel", "arbitrary"), name="moe_experts",
    )(*args)


_BF16_STACKS = ("w_in", "w_out", "mem_w_q", "mem_w_k", "mem_w_v", "mem_w_o", "moe_w_gate", "moe_w_up", "moe_w_down")


def _layer_weights(l, p, pb):
    w_in = pb["w_in"][l]
    c = np.cumsum([0, FOX_W, FOX_W, FOX_W, FOX_HEADS, S5_W, GLA_QK, GLA_QK, GLA_W, GLA_RANK, GLA_W]).tolist()
    col = lambda i: w_in[:, c[i]:c[i + 1]]
    w = {}
    w["wrest"] = jnp.concatenate([col(4), col(5), col(6), col(7), col(9)], axis=1)
    w["wtail"] = jnp.concatenate(
        [col(3), col(8), jnp.zeros((D_MODEL, TAIL_W - FOX_HEADS - GLA_RANK), BF16)], axis=1)
    w["tail_bias"] = jnp.concatenate([p["fox_f_bias"][l], jnp.zeros((TAIL_W - FOX_HEADS,), F32)])
    w["q_gain"] = jnp.tile(p["fox_q_gain"][l], FOX_HEADS)
    w["k_gain"] = jnp.tile(p["fox_k_gain"][l], FOX_HEADS)
    lam_re, lam_im, bb_re, bb_im = s5_discretise(
        p["s5_a_re"][l], p["s5_a_im"][l], p["s5_log_dt"][l],
        jnp.swapaxes(p["s5_b_re"][l], 1, 2), jnp.swapaxes(p["s5_b_im"][l], 1, 2))
    gpb = S5_GROUPS // S5_BLOCKS
    w["s5_lam_re"], w["s5_lam_im"] = lam_re.reshape(1, S5_NS), lam_im.reshape(1, S5_NS)
    w["s5_wb_f32"] = jnp.concatenate([_block_diag(bb_re, gpb), _block_diag(bb_im, gpb)], axis=2)
    w["s5_wb"] = w["s5_wb_f32"].astype(BF16)
    c_re_t = jnp.swapaxes(p["s5_c_re"][l], 1, 2)
    c_im_t = jnp.swapaxes(p["s5_c_im"][l], 1, 2)
    w["s5_wc_f32"] = jnp.concatenate([_block_diag(c_re_t, gpb), -_block_diag(c_im_t, gpb)], axis=1)
    w["s5_wc"] = w["s5_wc_f32"].astype(BF16)
    w["s5_wglu"] = p["s5_w_glu"][l].astype(BF16)
    w["gla_wa_f32"] = jnp.concatenate(
        [jnp.zeros((FOX_HEADS, GLA_QK), F32), p["gla_w_a2"][l],
         jnp.zeros((TAIL_W - FOX_HEADS - GLA_RANK, GLA_QK), F32)], axis=0)
    w["gla_wa"] = w["gla_wa_f32"].astype(BF16)
    w["mem_q_gain"] = jnp.tile(p["mem_q_gain"][l], MEM_HEADS)
    w["mem_k_gain"] = jnp.tile(p["mem_k_gain"][l], MEM_HEADS)
    w["w_router"] = jnp.concatenate(
        [p["moe_w_expert"][l], p["moe_w_group"][l],
         jnp.zeros((D_MODEL, LANES - N_EXPERTS - N_GROUPS), F32)], axis=1)
    w["b_router"] = jnp.concatenate(
        [p["moe_b_expert"][l], p["moe_b_group"][l], jnp.zeros((LANES - N_EXPERTS - N_GROUPS,), F32)]).reshape(1, LANES)
    return w


def _in_projection(h, w, pb, l, tm):
    tn = PROJ_TN
    (q_b,) = matmul(h, pb["w_in"], tm=tm, tn=tn, out_dtypes=(BF16,), gain=w["q_gain"],
                    post_scale=FOX_HD ** -0.5 * LOG2E, layer=l, cols=(0, FOX_W), name="proj_q")
    k_f, k_b = matmul(h, pb["w_in"], tm=tm, tn=tn, out_dtypes=(F32, BF16), gain=w["k_gain"], layer=l,
                      cols=(FOX_W, FOX_W), name="proj_k")
    v_f, v_b = matmul(h, pb["w_in"], tm=tm, tn=tn, out_dtypes=(F32, BF16), layer=l, cols=(2 * FOX_W, FOX_W),
                      name="proj_v")
    (rest,) = matmul(h, w["wrest"], tm=tm, tn=tn, name="proj_rest")
    (tail,) = matmul(h, w["wtail"], tm=tm, tn=TAIL_W, logsig_bias=w["tail_bias"], logsig_lanes=FOX_HEADS,
                     name="proj_tail")
    return q_b, k_f, k_b, v_f, v_b, rest, tail


def _post_mixer(x, branches, mk, mv, w, p, pb, l, b, t, tm, tq_mem, g_next):
    n = x.shape[0]
    tm_full = min(256, n)
    x, hq = proj_res_norm(branches, pb["w_out"], l, x, p["g_mem_pre"][l], tm_full, name="proj_out")
    (q,) = matmul(hq, pb["mem_w_q"], tm=tm, tn=512, out_dtypes=(BF16,), gain=w["mem_q_gain"],
                  post_scale=MEM_HD ** -0.5, layer=l, name="mem_q")
    o = memory_attention(q.reshape(b, t, MEM_W), mk, mv, tq_mem).reshape(n, MEM_W)
    x, hm = proj_res_norm([o], pb["mem_w_o"], l, x, p["g_moe"][l], tm_full, name="mem_o")
    gate = moe_router(x, p["g_moe"][l], w["w_router"], w["b_router"], tm)
    out = moe_experts(hm, gate, pb["moe_w_gate"], pb["moe_w_up"], pb["moe_w_down"], l, x, tm, g_next=g_next)
    return out if g_next is not None else (out, None)


def kernel(x_prompt, x_sample, cache_fox_k, cache_fox_v, cache_fox_logf, state_s5_re, state_s5_im, state_gla,
           cache_mem_k, cache_mem_v, page_table, mem_prompt, g_mix, w_in, fox_q_gain, fox_k_gain, fox_f_bias,
           s5_a_re, s5_a_im, s5_log_dt, s5_b_re, s5_b_im, s5_c_re, s5_c_im, s5_d, s5_w_glu, s5_b_glu,
           gla_w_a2, gla_b_a, gla_gain, g_fox_out, g_s5_out, w_out, g_mem_pre, g_mem_tok, mem_w_q, mem_w_k,
           mem_w_v, mem_q_gain, mem_k_gain, mem_w_o, g_moe, moe_w_group, moe_b_group, moe_w_expert,
           moe_b_expert, moe_w_gate, moe_w_up, moe_w_down):
    p = dict(g_mix=g_mix, w_in=w_in, fox_q_gain=fox_q_gain, fox_k_gain=fox_k_gain, fox_f_bias=fox_f_bias,
             s5_a_re=s5_a_re, s5_a_im=s5_a_im, s5_log_dt=s5_log_dt, s5_b_re=s5_b_re, s5_b_im=s5_b_im,
             s5_c_re=s5_c_re, s5_c_im=s5_c_im, s5_d=s5_d, s5_w_glu=s5_w_glu, s5_b_glu=s5_b_glu,
             gla_w_a2=gla_w_a2, gla_b_a=gla_b_a, gla_gain=gla_gain, g_fox_out=g_fox_out, g_s5_out=g_s5_out,
             w_out=w_out, g_mem_pre=g_mem_pre, g_mem_tok=g_mem_tok, mem_w_q=mem_w_q, mem_w_k=mem_w_k,
             mem_w_v=mem_w_v, mem_q_gain=mem_q_gain, mem_k_gain=mem_k_gain, mem_w_o=mem_w_o, g_moe=g_moe,
             moe_w_group=moe_w_group, moe_b_group=moe_b_group, moe_w_expert=moe_w_expert,
             moe_b_expert=moe_b_expert, moe_w_gate=moe_w_gate, moe_w_up=moe_w_up, moe_w_down=moe_w_down)
    depth = w_in.shape[0]
    b, t, d = x_prompt.shape
    db = x_sample.shape[0]
    n = b * t
    n_mem = b * MEM_TOKENS
    pb = {name: p[name].astype(BF16) for name in _BF16_STACKS}
    w_in_t = jnp.swapaxes(w_in, 1, 2)
    tm = min(512, n)
    tm_proj = min(1024, n)
    tq_fox = min(512, t)
    s5_rows = min(256, t)
    gla_rows = min(256, t)
    tq_mem = min(512, t)

    xp = x_prompt.reshape(n, d)
    xs = jnp.pad(x_sample.reshape(db, d), ((0, SAMPLE_ROWS - db), (0, 0)))
    mem2d = mem_prompt.reshape(n_mem, d)
    cache_ft = jnp.swapaxes(cache_fox_logf, 2, 3)
    zero_s5 = jnp.zeros((b, 1, S5_NS), F32)
    zero_gla = jnp.zeros((b, GLA_HEADS, GLA_DV, GLA_DK), F32)
    outs = {k: [] for k in ("pk", "pv", "pf", "sk", "sv", "sf", "p5r", "p5i", "s5r", "s5i", "pg", "sg", "pmk", "pmv")}

    h_prompt = rmsnorm_cast(xp, g_mix[0], tm)
    for l in range(depth):
        w = _layer_weights(l, p, pb)

        q_b, k_f, k_b, v_f, v_b, rest, tail = _in_projection(h_prompt, w, pb, l, tm_proj)
        crow, ccol = fox_gate_cumsum(tail, b, t)
        fox_o = fox_prompt_attention(q_b, k_b, v_b, ccol, crow, g_fox_out[l], b, t, tq_fox)
        rest3, tail3 = rest.reshape(b, t, 4 * S5_W), tail.reshape(b, t, TAIL_W)
        seg_steps = s5_rows // S5_SCAN_BLOCK
        su = rest3[:, :, :S5_W].reshape(b, t // s5_rows, S5_SCAN_BLOCK, seg_steps, S5_W)
        su = jnp.swapaxes(su, 2, 3).reshape(b, t, S5_W)
        s5_o, hre, him = s5_mixer(su, zero_s5, zero_s5, w["s5_lam_re"], w["s5_lam_im"], w["s5_wb"], w["s5_wc"],
                                  s5_d[l], w["s5_wglu"], s5_b_glu[l], g_s5_out[l],
                                  nseq=b, rows=s5_rows, nsteps=t // s5_rows, scan=True, col_block=0)
        s5_o = s5_o.reshape(b, t // s5_rows, seg_steps, S5_SCAN_BLOCK, S5_W)
        s5_o = jnp.swapaxes(s5_o, 2, 3).reshape(b, t, S5_W)
        gla_o, st = gla_mixer(rest3, tail3, w["gla_wa"], gla_b_a[l], gla_gain[l], zero_gla,
                              nseq=b, rows=gla_rows, nsteps=t // gla_rows, t_valid=gla_rows)
        branches = [fox_o, s5_o.reshape(n, S5_W), gla_o.reshape(n, GLA_W)]
        outs["pk"].append(k_f.reshape(b, t, FOX_HEADS, FOX_HD))
        outs["pv"].append(v_f.reshape(b, t, FOX_HEADS, FOX_HD))
        outs["pf"].append(tail3[:, :, :FOX_HEADS])
        outs["p5r"].append(hre.reshape(b, S5_GROUPS, S5_STATE))
        outs["p5i"].append(him.reshape(b, S5_GROUPS, S5_STATE))
        outs["pg"].append(jnp.swapaxes(st, 2, 3))
        hmem = rmsnorm_cast(mem2d, g_mem_tok[l], tm=min(512, n_mem))
        (mk,) = matmul(hmem, pb["mem_w_k"], tm=min(512, n_mem), tn=512, gain=w["mem_k_gain"], layer=l, name="mem_k")
        (mv,) = matmul(hmem, pb["mem_w_v"], tm=min(512, n_mem), tn=512, layer=l, name="mem_v")
        outs["pmk"].append(mk.reshape(b, MEM_TOKENS, MEM_HEADS, MEM_HD))
        outs["pmv"].append(mv.reshape(b, MEM_TOKENS, MEM_HEADS, MEM_HD))
        xp, h_prompt = _post_mixer(xp, branches, mk.reshape(b, MEM_TOKENS, MEM_W), mv.reshape(b, MEM_TOKENS, MEM_W),
                                   w, p, pb, l, b, t, tm, tq_mem, g_mix[l + 1] if l + 1 < depth else None)

        ns = SAMPLE_ROWS
        h = rmsnorm_cast(xs, g_mix[l], ns, out_dtype=F32)
        c0 = np.cumsum([0, FOX_W, FOX_W, FOX_W, FOX_HEADS, S5_W, GLA_QK, GLA_QK, GLA_W, GLA_RANK, GLA_W]).tolist()
        in_cols = c0[-1]
        qk_gain = jnp.concatenate([w["q_gain"], w["k_gain"]]).reshape(1, 2 * FOX_W)
        f_bias = jnp.concatenate([fox_f_bias[l], jnp.zeros((in_cols - 3 * FOX_W - FOX_HEADS,), F32)]).reshape(1, -1)
        proj = pmatmul(h, w_in_t, l, tk=256, epi="inproj", aux=(qk_gain, f_bias), w_cols_first=True,
                       name="proj_in_sample")
        col = lambda i: proj[:, c0[i]:c0[i + 1]]
        hd3 = lambda a: a[:db].reshape(db, FOX_HEADS, FOX_HD)
        q_f, k_f, v_f, logf_new = col(0), col(1), col(2), col(3)[:db]
        rest = jnp.concatenate([col(4), col(5), col(6), col(7), col(9)], axis=1)
        tail = jnp.concatenate([col(3), col(8), jnp.zeros((ns, TAIL_W - FOX_HEADS - GLA_RANK), F32)], axis=1)
        fox_o = fox_sample_attention(l, hd3(q_f), hd3(k_f), hd3(v_f), logf_new.reshape(db, FOX_HEADS, 1),
                                     cache_fox_k, cache_fox_v, cache_ft, page_table, g_fox_out[l],
                                     pages_per_step=DECODE_PAGES_PER_STEP)
        fox_o = jnp.pad(fox_o.reshape(db, FOX_W), ((0, ns - db), (0, 0)))
        pad_state = lambda a: jnp.pad(a.reshape(1, db, S5_NS), ((0, 0), (0, ns - db), (0, 0)))
        s5_o, hre, him = s5_mixer(rest.reshape(1, ns, 4 * S5_W), pad_state(state_s5_re[l]), pad_state(state_s5_im[l]),
                                  w["s5_lam_re"], w["s5_lam_im"], w["s5_wb_f32"], w["s5_wc_f32"], s5_d[l],
                                  s5_w_glu[l], s5_b_glu[l], g_s5_out[l], nseq=1, rows=ns, nsteps=1, scan=False,
                                  col_block=0, precise=True)
        seq_pad = lambda a: jnp.pad(a[:db].reshape(db, 1, a.shape[1]), ((0, 0), (0, GLA_CHUNK - 1), (0, 0)))
        gla_o, st = gla_mixer(seq_pad(rest), seq_pad(tail), w["gla_wa_f32"], gla_b_a[l], gla_gain[l],
                              jnp.swapaxes(state_gla[l], 2, 3), nseq=db, rows=GLA_CHUNK, nsteps=1, t_valid=1,
                              precise=True)
        gla_o = jnp.pad(gla_o[:, 0, :], ((0, ns - db), (0, 0)))
        cat = jnp.concatenate([fox_o, s5_o.reshape(ns, S5_W), gla_o], axis=1)
        outs["sk"].append(hd3(k_f).reshape(db, 1, FOX_HEADS, FOX_HD))
        outs["sv"].append(hd3(v_f).reshape(db, 1, FOX_HEADS, FOX_HD))
        outs["sf"].append(logf_new.reshape(db, 1, FOX_HEADS))
        outs["s5r"].append(hre[0, :db].reshape(db, S5_GROUPS, S5_STATE))
        outs["s5i"].append(him[0, :db].reshape(db, S5_GROUPS, S5_STATE))
        outs["sg"].append(jnp.swapaxes(st, 2, 3))
        xs = _post_mixer_sample(xs, cat, cache_mem_k[l].reshape(db, MEM_TOKENS, MEM_W),
                                cache_mem_v[l].reshape(db, MEM_TOKENS, MEM_W), w, p, l, db)

    stk = lambda k: jnp.stack(outs[k])
    return (xp.reshape(b, t, d), xs[:db].reshape(db, 1, d),
            stk("pk"), stk("pv"), stk("pf"), stk("sk"), stk("sv"), stk("sf"),
            stk("p5r"), stk("p5i"), stk("s5r"), stk("s5i"), stk("pg"), stk("sg"), stk("pmk"), stk("pmv"))


def _post_mixer_sample(x, cat, mk, mv, w, p, l, db):
    ns = x.shape[0]
    x = pmatmul(cat, p["w_out"], l, tk=512, epi="res", aux=(x,), name="proj_out_sample")
    hq = rmsnorm_cast(x, p["g_mem_pre"][l], ns, out_dtype=F32)
    q = pmatmul(hq, p["mem_w_q"], l, tk=512, epi="headnorm", aux=(w["mem_q_gain"].reshape(1, MEM_W),),
                post_scale=MEM_HD ** -0.5, name="mem_q_sample")
    q3 = jnp.pad(q[:db].reshape(db, 1, MEM_W), ((0, 0), (0, SAMPLE_ROWS - 1), (0, 0)))
    o = memory_attention(q3, mk, mv, SAMPLE_ROWS, precise=True)[:, 0, :]
    o = jnp.pad(o, ((0, ns - db), (0, 0)))
    x = pmatmul(o, p["mem_w_o"], l, tk=MEM_W, epi="res", aux=(x,), name="mem_o_sample")
    hm = rmsnorm_cast(x, p["g_moe"][l], ns, out_dtype=F32)
    gate = pmatmul(hm, w["w_router"][None], 0, tk=512, epi="router", aux=(w["b_router"],), name="router_sample")
    return moe_experts(hm, gate, p["moe_w_gate"], p["moe_w_up"], p["moe_w_down"], l, x, ns, precise=True)
```
